```python
import jax
import jax.numpy as jnp
from jax import lax
import numpy as np

D_MODEL = 1024
BATCH = 8
SEQ = 4096
DEPTH = 2

N_MIXERS = 2
N_A_LAYERS = (DEPTH + 1) // 2
N_B_LAYERS = DEPTH // 2
RMS_EPS = 1e-6
DILATION_PATTERNS = ((128, 1), (512, 4), (2048, 16))
N_GROUPS_A = 3
HEADS_A = 16
HEAD_DIM_A = D_MODEL // HEADS_A
WIDTH_A = HEADS_A * HEAD_DIM_A
BAND_BLOCK = 128
ROPE_THETA = 10000.0
MAX_POS_OFFSET = 1024
NEG_INF = -1e30
HEADS_B = 4
KEY_DIM_B = D_MODEL // 2 // HEADS_B
VAL_DIM_B = D_MODEL // HEADS_B
HK_B = HEADS_B * KEY_DIM_B
HV_B = HEADS_B * VAL_DIM_B
GATE_RANK = 16
GATE_TAU = 16.0
GLA_CHUNK = 64
B_IN_WIDTH = 2 * HK_B + 2 * HV_B + GATE_RANK
N_EXPERTS = 32
TOP_K = 4
D_FF = D_MODEL
SWIGLU_LIMIT = 7.0
SWIGLU_ALPHA = 1.702
EXPERT_BLOCK = 128

kernel_name = 'hybrid_dilated_gla_moe_adaln'


def rmsnorm(t, gain):
    tf = t.astype(jnp.float32)
    y = tf * lax.rsqrt(jnp.mean(tf * tf, axis=-1, keepdims=True) + RMS_EPS)
    return (y * gain.astype(jnp.float32)).astype(t.dtype)


def rope(t, positions):
    half = t.shape[-1] // 2
    inv_freq = ROPE_THETA ** (-jnp.arange(half, dtype=jnp.float32) / half)
    ang = positions.astype(jnp.float32)[..., None] * inv_freq
    cos = jnp.cos(ang)[:, :, None, :]
    sin = jnp.sin(ang)[:, :, None, :]
    tf = t.astype(jnp.float32)
    t1, t2 = tf[..., :half], tf[..., half:]
    return jnp.concatenate([t1 * cos - t2 * sin, t2 * cos + t1 * sin], axis=-1).astype(t.dtype)


def dilated_window_attention(q, k, v, steps, dil):
    B, S, H, E = q.shape
    L = S // dil
    nb = -(-L // BAND_BLOCK)
    Lp = nb * BAND_BLOCK

    def residue_blocks(a):
        a = a.reshape(B, L, dil, H, E).transpose(0, 2, 1, 3, 4)
        a = jnp.pad(a, ((0, 0), (0, 0), (0, Lp - L), (0, 0), (0, 0)))
        return a.reshape(B, dil, nb, BAND_BLOCK, H, E)

    def with_prev(a):
        prev = jnp.pad(a[:, :, :-1], ((0, 0), (0, 0), (1, 0), (0, 0), (0, 0), (0, 0)))
        return jnp.concatenate([prev, a], axis=3)

    qb = residue_blocks(q)
    kk = with_prev(residue_blocks(k))
    vv = with_prev(residue_blocks(v))
    s = jnp.einsum('brnqhe,brnkhe->brnhqk', qb, kk,
                   preferred_element_type=jnp.float32) * (E ** -0.5)
    qi = jnp.arange(BAND_BLOCK)[:, None]
    kj = jnp.arange(2 * BAND_BLOCK)[None, :]
    delta = qi + BAND_BLOCK - kj
    blk = jnp.arange(nb)[:, None, None]
    valid = (delta >= 0) & (delta <= steps) & ((blk > 0) | (kj >= BAND_BLOCK))
    s = jnp.where(valid[:, None], s, NEG_INF)
    m = jnp.max(s, axis=-1, keepdims=True)
    p = jnp.exp(s - m)
    den = jnp.sum(p, axis=-1, keepdims=True)
    o = jnp.einsum('brnhqk,brnkhe->brnhqe', p, vv.astype(jnp.float32)) / den
    lse = (m + jnp.log(den))[..., 0]
    o = o.transpose(0, 1, 2, 4, 3, 5).reshape(B, dil, Lp, H, E)[:, :, :L]
    o = o.transpose(0, 2, 1, 3, 4).reshape(B, S, H, E)
    lse = lse.transpose(0, 1, 2, 4, 3).reshape(B, dil, Lp, H)[:, :, :L]
    lse = lse.transpose(0, 2, 1, 3).reshape(B, S, H)
    return o, lse


def dilated_mixer(h, positions, w_in, q_gain, k_gain, w_out):
    B, S, _ = h.shape
    qkv = (h @ w_in).reshape(B, S, N_GROUPS_A, 3, HEADS_A, HEAD_DIM_A)
    outs, lses = [], []
    for g, (window, dil) in enumerate(DILATION_PATTERNS):
        q = rope(rmsnorm(qkv[:, :, g, 0], q_gain[g]), positions)
        k = rope(rmsnorm(qkv[:, :, g, 1], k_gain[g]), positions)
        o, lse = dilated_window_attention(q, k, qkv[:, :, g, 2], window // dil, dil)
        outs.append(o)
        lses.append(lse)
    w = jax.nn.softmax(jnp.stack(lses, axis=0), axis=0)
    o = jnp.sum(w[..., None] * jnp.stack(outs, axis=0), axis=0)
    return o.astype(h.dtype).reshape(B, S, WIDTH_A) @ w_out


def gla_mixer(h, w_in, w_gate_up, gate_bias, out_gain, w_out):
    B, S, _ = h.shape
    nc = S // GLA_CHUNK
    proj = h @ w_in
    q, k, v, r, a = jnp.split(proj, [HK_B, 2 * HK_B, 2 * HK_B + HV_B, 2 * HK_B + 2 * HV_B], axis=-1)
    log_alpha = jax.nn.log_sigmoid((a @ w_gate_up + gate_bias).astype(jnp.float32)) / GATE_TAU

    def chunks(t, dim):
        return t.astype(jnp.float32).reshape(B, nc, GLA_CHUNK, HEADS_B, dim).transpose(0, 3, 1, 2, 4)

    q = chunks(q, KEY_DIM_B) * (KEY_DIM_B ** -0.5)
    k = chunks(k, KEY_DIM_B)
    v = chunks(v, VAL_DIM_B)
    b = jnp.cumsum(chunks(log_alpha, KEY_DIM_B), axis=3)
    b_last = b[:, :, :, -1:]
    q_dec = q * jnp.exp(b)
    att = jnp.einsum('bhncd,bhnjd->bhncj', q_dec, k * jnp.exp(-b))
    causal = jnp.tril(jnp.ones((GLA_CHUNK, GLA_CHUNK), dtype=bool))
    att = jnp.where(causal, att, 0.0)
    o_intra = jnp.einsum('bhncj,bhnjv->bhncv', att, v)
    k_dec = k * jnp.exp(b_last - b)
    chunk_decay = jnp.exp(b_last[:, :, :, 0])

    def step(state, inp):
        qd, kd, vc, dec = inp
        o = jnp.einsum('bhcd,bhdv->bhcv', qd, state)
        state = dec[..., None] * state + jnp.einsum('bhcd,bhcv->bhdv', kd, vc)
        return state, o

    xs = (jnp.moveaxis(q_dec, 2, 0), jnp.moveaxis(k_dec, 2, 0),
          jnp.moveaxis(v, 2, 0), jnp.moveaxis(chunk_decay, 2, 0))
    state0 = jnp.zeros((B, HEADS_B, KEY_DIM_B, VAL_DIM_B), jnp.float32)
    _, o_inter = lax.scan(step, state0, xs)
    o = o_intra + jnp.moveaxis(o_inter, 0, 2)
    o = o.transpose(0, 2, 3, 1, 4).reshape(B, S, HEADS_B, VAL_DIM_B)
    o = o * lax.rsqrt(jnp.mean(o * o, axis=-1, keepdims=True) + RMS_EPS) * out_gain.astype(jnp.float32)
    o = o.reshape(B, S, HV_B) * jax.nn.silu(r.astype(jnp.float32))
    return o.astype(h.dtype) @ w_out


def moe_ffn(h, router_w, router_b, w_gu, b_gu, w_down, b_down):
    B, S, D = h.shape
    n_tok = B * S
    t = h.reshape(n_tok, D)
    logits = (t @ router_w + router_b).astype(jnp.float32)
    top_v, top_i = lax.top_k(logits, TOP_K)
    gates = jax.nn.softmax(top_v, axis=-1)
    n_assign = n_tok * TOP_K
    flat_e = top_i.reshape(-1).astype(jnp.int32)
    flat_tok = jnp.repeat(jnp.arange(n_tok, dtype=jnp.int32), TOP_K)
    flat_w = gates.reshape(-1)
    order = jnp.argsort(flat_e)
    sorted_e = flat_e[order]
    counts = jnp.bincount(flat_e, length=N_EXPERTS)
    padded = (counts + EXPERT_BLOCK - 1) // EXPERT_BLOCK * EXPERT_BLOCK
    seg_end = jnp.cumsum(padded)
    rank = jnp.arange(n_assign, dtype=jnp.int32) - (jnp.cumsum(counts) - counts)[sorted_e]
    dest = (seg_end - padded)[sorted_e] + rank
    cap = n_assign + N_EXPERTS * EXPERT_BLOCK
    n_blocks = cap // EXPERT_BLOCK
    tok_buf = jnp.full((cap,), n_tok, jnp.int32).at[dest].set(flat_tok[order])
    w_buf = jnp.zeros((cap,), jnp.float32).at[dest].set(flat_w[order])
    block_e = jnp.minimum(
        jnp.searchsorted(seg_end, jnp.arange(n_blocks, dtype=jnp.int32) * EXPERT_BLOCK, side='right'),
        N_EXPERTS - 1)
    t_pad = jnp.concatenate([t, jnp.zeros((1, D), t.dtype)], axis=0)

    def expert_block(args):
        idx, e = args
        gu = (t_pad[idx] @ w_gu[e] + b_gu[e]).astype(jnp.float32)
        gate = jnp.minimum(gu[:, :D_FF], SWIGLU_LIMIT)
        up = jnp.clip(gu[:, D_FF:], -SWIGLU_LIMIT, SWIGLU_LIMIT)
        act = (up + 1.0) * (gate * jax.nn.sigmoid(SWIGLU_ALPHA * gate))
        return (act.astype(h.dtype) @ w_down[e] + b_down[e]).astype(jnp.float32)

    outs = lax.map(expert_block, (tok_buf.reshape(n_blocks, EXPERT_BLOCK), block_e))
    outs = outs.reshape(cap, D) * w_buf[:, None]
    y = jnp.zeros((n_tok + 1, D), jnp.float32).at[tok_buf].add(outs)[:n_tok]
    return y.astype(h.dtype).reshape(B, S, D)


def setup_inputs(seed: int = 0) -> dict:
    key = jax.random.key(seed)
    ks = jax.random.split(key, 22)
    D, E, F = D_MODEL, N_EXPERTS, D_FF

    def nrm(k, shape, scale):
        return jax.random.normal(k, shape, jnp.float32) * scale

    offsets = jax.random.randint(ks[2], (BATCH, 1), 0, MAX_POS_OFFSET, dtype=jnp.int32)
    positions = (offsets + jnp.arange(SEQ, dtype=jnp.int32)[None, :]).astype(jnp.int32)
    return {
        'x': nrm(ks[0], (BATCH, SEQ, D), 1.0),
        'c': nrm(ks[1], (BATCH, D), 1.0),
        'positions': positions,
        'ada_w': nrm(ks[3], (DEPTH, D, 6 * D), 0.5 * D ** -0.5),
        'ada_b': nrm(ks[4], (DEPTH, 6 * D), 0.02),
        'norm1_g': 1.0 + nrm(ks[5], (DEPTH, D), 0.02),
        'norm2_g': 1.0 + nrm(ks[6], (DEPTH, D), 0.02),
        'a_w_in': nrm(ks[7], (N_A_LAYERS, D, N_GROUPS_A * 3 * WIDTH_A), D ** -0.5),
        'a_q_gain': 1.0 + nrm(ks[8], (N_A_LAYERS, N_GROUPS_A, HEAD_DIM_A), 0.02),
        'a_k_gain': 1.0 + nrm(ks[9], (N_A_LAYERS, N_GROUPS_A, HEAD_DIM_A), 0.02),
        'a_w_out': nrm(ks[10], (N_A_LAYERS, WIDTH_A, D), WIDTH_A ** -0.5),
        'b_w_in': nrm(ks[11], (N_B_LAYERS, D, B_IN_WIDTH), D ** -0.5),
        'b_w_gate_up': nrm(ks[12], (N_B_LAYERS, GATE_RANK, HK_B), GATE_RANK ** -0.5),
        'b_gate_bias': nrm(ks[13], (N_B_LAYERS, HK_B), 0.1),
        'b_out_gain': 1.0 + nrm(ks[14], (N_B_LAYERS, VAL_DIM_B), 0.02),
        'b_w_out': nrm(ks[15], (N_B_LAYERS, HV_B, D), HV_B ** -0.5),
        'router_w': nrm(ks[16], (DEPTH, D, E), D ** -0.5),
        'router_b': nrm(ks[17], (DEPTH, E), 0.01),
        'moe_w_gu': nrm(ks[18], (DEPTH, E, D, 2 * F), D ** -0.5),
        'moe_b_gu': nrm(ks[19], (DEPTH, E, 2 * F), 0.02),
        'moe_w_down': nrm(ks[20], (DEPTH, E, F, D), F ** -0.5),
        'moe_b_down': nrm(ks[21], (DEPTH, E, D), 0.02),
    }


def reference(x, c, positions, ada_w, ada_b, norm1_g, norm2_g, a_w_in, a_q_gain, a_k_gain,
              a_w_out, b_w_in, b_w_gate_up, b_gate_bias, b_out_gain, b_w_out, router_w,
              router_b, moe_w_gu, moe_b_gu, moe_w_down, moe_b_down):
    cond = jax.nn.silu(c)
    for layer in range(DEPTH):
        mod = cond @ ada_w[layer] + ada_b[layer]
        shift1, scale1, gate1, shift2, scale2, gate2 = [m[:, None, :] for m in jnp.split(mod, 6, axis=-1)]
        h = rmsnorm(x, norm1_g[layer]) * (1.0 + scale1) + shift1
        j = layer // N_MIXERS
        if layer % N_MIXERS == 0:
            mix = dilated_mixer(h, positions, a_w_in[j], a_q_gain[j], a_k_gain[j], a_w_out[j])
        else:
            mix = gla_mixer(h, b_w_in[j], b_w_gate_up[j], b_gate_bias[j], b_out_gain[j], b_w_out[j])
        x = x + gate1 * mix
        h = rmsnorm(x, norm2_g[layer]) * (1.0 + scale2) + shift2
        x = x + gate2 * moe_ffn(h, router_w[layer], router_b[layer], moe_w_gu[layer],
                                moe_b_gu[layer], moe_w_down[layer], moe_b_down[layer])
    return x
```

```python
import functools

import jax
import jax.numpy as jnp
from jax import lax
from jax.experimental import pallas as pl
from jax.experimental.pallas import tpu as pltpu

F32 = jnp.float32
BF16 = jnp.bfloat16
HIGHEST = lax.Precision.HIGHEST

RMS_EPS = 1e-6
DILATIONS = ((128, 1), (512, 4), (2048, 16))
N_GROUPS = 3
HEADS_A = 16
HEAD_DIM_A = 64
BAND = 128
ROPE_THETA = 10000.0
NEG_INF = -1e30
HEADS_B = 4
KEY_DIM_B = 128
VAL_DIM_B = 256
GATE_RANK = 16
GATE_TAU = 16.0
GLA_CHUNK = 64
N_EXPERTS = 32
TOP_K = 4
SWIGLU_LIMIT = 7.0
SWIGLU_ALPHA = 1.702

LANES = 128
VMEM_LIMIT = 56 * 1024 * 1024

TM_PROJ = 512
TM_ROUTE = 512
TM_EXPERT = 512
TM_MOVE = 256
T_GLA = 512


def _params(*sem):
    return pltpu.CompilerParams(dimension_semantics=sem, vmem_limit_bytes=VMEM_LIMIT)


def _norm_mod(x, gain, shift, scale):
    ms = jnp.mean(x * x, axis=-1, keepdims=True)
    y = x * lax.rsqrt(ms + RMS_EPS) * gain
    return y * (1.0 + scale) + shift


def _adaln_kernel(c_ref, w_ref, b_ref, o_ref):
    c = c_ref[...]
    cond = c * jax.nn.sigmoid(c)
    o_ref[0] = jnp.dot(cond, w_ref[0], precision=HIGHEST, preferred_element_type=F32) + b_ref[0]


def _adaln(c, ada_w, ada_b):
    depth, d, n6 = ada_w.shape
    b = c.shape[0]
    tn = 1536
    return pl.pallas_call(
        _adaln_kernel,
        grid=(depth, n6 // tn),
        in_specs=[
            pl.BlockSpec((b, d), lambda l, j: (0, 0)),
            pl.BlockSpec((1, d, tn), lambda l, j: (l, 0, j)),
            pl.BlockSpec((1, 1, tn), lambda l, j: (l, 0, j)),
        ],
        out_specs=pl.BlockSpec((1, b, tn), lambda l, j: (l, 0, j)),
        out_shape=jax.ShapeDtypeStruct((depth, b, n6), F32),
        compiler_params=_params("parallel", "parallel"),
        name="adaln",
    )(c, ada_w, ada_b.reshape(depth, 1, n6))


def _rope_kernel(pos_ref, cos_ref, sin_ref):
    pos = pos_ref[...].astype(F32)
    lane = lax.broadcasted_iota(jnp.int32, (1, LANES), 1)
    half = HEAD_DIM_A // 2
    fidx = (lane & (half - 1)).astype(F32)
    inv_freq = jnp.power(jnp.full((1, LANES), ROPE_THETA, F32), -(fidx / half))
    ang = pos * inv_freq
    cos_ref[...] = jnp.cos(ang)
    s = jnp.sin(ang)
    sin_ref[...] = jnp.where((lane & (HEAD_DIM_A - 1)) < half, -s, s)


def _rope_tables(positions):
    n = positions.size
    tm = 1024
    return pl.pallas_call(
        _rope_kernel,
        grid=(n // tm,),
        in_specs=[pl.BlockSpec((tm, 1), lambda i: (i, 0))],
        out_specs=[pl.BlockSpec((tm, LANES), lambda i: (i, 0))] * 2,
        out_shape=[jax.ShapeDtypeStruct((n, LANES), F32)] * 2,
        compiler_params=_params("parallel"),
        name="rope_tables",
    )(positions.reshape(n, 1))


def _qkv_kernel(x_ref, mod_ref, g_ref, w_ref, gain_ref, cos_ref, sin_ref, bd_ref, o_ref, h_ref):
    j = pl.program_id(1)

    @pl.when(j == 0)
    def _():
        h = _norm_mod(x_ref[...], g_ref[...], mod_ref[0, 0:1, :], mod_ref[0, 1:2, :])
        h_ref[...] = h.astype(BF16)

    acc = jnp.dot(h_ref[...], w_ref[...], preferred_element_type=F32)
    comp = j % 3

    @pl.when(comp == 2)
    def _():
        o_ref[...] = acc.astype(BF16)

    @pl.when(comp != 2)
    def _():
        gain = gain_ref[0]
        cos = cos_ref[...]
        sin = sin_ref[...]
        lane = lax.broadcasted_iota(jnp.int32, (1, LANES), 1)
        first_half = (lane & (HEAD_DIM_A - 1)) < (HEAD_DIM_A // 2)
        for cc in range(4):
            a = acc[:, cc * 256:(cc + 1) * 256]
            ss = jnp.dot((a * a).astype(BF16), bd_ref[...], preferred_element_type=F32)
            qn = a * lax.rsqrt(ss * (1.0 / HEAD_DIM_A) + RMS_EPS) * gain[:, cc * 256:(cc + 1) * 256]
            for hh in range(2):
                xx = qn[:, hh * LANES:(hh + 1) * LANES]
                swapped = jnp.where(first_half, pltpu.roll(xx, 96, 1), pltpu.roll(xx, 32, 1))
                lo = cc * 256 + hh * LANES
                o_ref[:, lo:lo + LANES] = (xx * cos + swapped * sin).astype(BF16)


def _qkv_proj(x2, mod, g1, w_bf, gains, cos_t, sin_t, seq):
    n, d = x2.shape
    ncol = w_bf.shape[1]
    tm = TM_PROJ
    per_b = seq // tm
    bd = (jnp.arange(256)[:, None] // HEAD_DIM_A == jnp.arange(256)[None, :] // HEAD_DIM_A).astype(BF16)
    return pl.pallas_call(
        _qkv_kernel,
        grid=(n // tm, ncol // 1024),
        in_specs=[
            pl.BlockSpec((tm, d), lambda i, j: (i, 0)),
            pl.BlockSpec((1, 6, d), lambda i, j: (i // per_b, 0, 0)),
            pl.BlockSpec((1, d), lambda i, j: (0, 0)),
            pl.BlockSpec((d, 1024), lambda i, j: (0, j)),
            pl.BlockSpec((1, 1, 1024), lambda i, j: (j, 0, 0)),
            pl.BlockSpec((tm, LANES), lambda i, j: (i, 0)),
            pl.BlockSpec((tm, LANES), lambda i, j: (i, 0)),
            pl.BlockSpec((256, 256), lambda i, j: (0, 0)),
        ],
        out_specs=pl.BlockSpec((tm, 1024), lambda i, j: (i, j)),
        out_shape=jax.ShapeDtypeStruct((n, ncol), BF16),
        scratch_shapes=[pltpu.VMEM((tm, d), BF16)],
        compiler_params=_params("parallel", "arbitrary"),
        name="qkv_proj",
    )(x2, mod, g1, w_bf, gains, cos_t, sin_t, bd)


def _attn_kernel(q_ref, kp_ref, kc_ref, vp_ref, vc_ref, o_ref, lse_ref, *, steps):
    n = pl.program_id(2)
    row = lax.broadcasted_iota(jnp.int32, (BAND, 2 * BAND), 0)
    col = lax.broadcasted_iota(jnp.int32, (BAND, 2 * BAND), 1)
    delta = row + BAND - col
    valid = (delta >= 0) & (delta <= steps) & ((col >= BAND) | (n > 0))
    lane = lax.broadcasted_iota(jnp.int32, (BAND, LANES), 1)
    first_head = lane < HEAD_DIM_A
    lse_tile = jnp.zeros((BAND, LANES), F32)
    for p in range(HEADS_A // 2):
        sl = slice(p * LANES, (p + 1) * LANES)
        qp = q_ref[0, :, sl]
        kcat = jnp.concatenate([kp_ref[0, :, sl], kc_ref[0, :, sl]], axis=0)
        vcat = jnp.concatenate([vp_ref[0, :, sl], vc_ref[0, :, sl]], axis=0)
        outs = []
        for hh in range(2):
            keep = first_head if hh == 0 else jnp.logical_not(first_head)
            qm = jnp.where(keep, qp, jnp.zeros_like(qp))
            s = lax.dot_general(qm, kcat, (((1,), (1,)), ((), ())), preferred_element_type=F32)
            s = jnp.where(valid, s, NEG_INF)
            m = jnp.max(s, axis=-1, keepdims=True)
            pexp = jnp.exp(s - m)
            den = jnp.sum(pexp, axis=-1, keepdims=True)
            o = jnp.dot(pexp.astype(BF16), vcat, preferred_element_type=F32) / den
            outs.append(o)
            lse_tile = jnp.where(lane == (2 * p + hh), m + jnp.log(den), lse_tile)
        o_ref[0, :, sl] = jnp.where(first_head, outs[0], outs[1]).astype(BF16)
    lse_ref[0] = lse_tile


def _dilated_attention(qkv, batch, seq, g):
    window, dil = DILATIONS[g]
    n, ncol = qkv.shape
    l = seq // dil
    nb = l // BAND
    nblk = ncol // 1024
    view = qkv.reshape(batch, l, dil * ncol)
    base = g * 3

    def spec(comp, prev):
        if prev:
            return pl.BlockSpec((1, BAND, 1024), lambda b, r, i: (b, jnp.maximum(i - 1, 0), r * nblk + base + comp))
        return pl.BlockSpec((1, BAND, 1024), lambda b, r, i: (b, i, r * nblk + base + comp))

    o, lse = pl.pallas_call(
        functools.partial(_attn_kernel, steps=window // dil),
        grid=(batch, dil, nb),
        in_specs=[spec(0, False), spec(1, True), spec(1, False), spec(2, True), spec(2, False)],
        out_specs=[
            pl.BlockSpec((1, BAND, 1024), lambda b, r, i: (b, i, r)),
            pl.BlockSpec((1, BAND, LANES), lambda b, r, i: (b, i, r)),
        ],
        out_shape=[
            jax.ShapeDtypeStruct((batch, l, dil * 1024), BF16),
            jax.ShapeDtypeStruct((batch, l, dil * LANES), F32),
        ],
        compiler_params=_params("parallel", "parallel", "arbitrary"),
        name=f"dilated_attn_g{g}",
    )(view, view, view, view, view)
    return o.reshape(n, 1024), lse.reshape(n, LANES)


def _merge_proj_kernel(o0_ref, o1_ref, o2_ref, l0_ref, l1_ref, l2_ref, x_ref, mod_ref, e_ref, w_ref, out_ref):
    l0, l1, l2 = l0_ref[...], l1_ref[...], l2_ref[...]
    m = jnp.maximum(jnp.maximum(l0, l1), l2)
    e0, e1, e2 = jnp.exp(l0 - m), jnp.exp(l1 - m), jnp.exp(l2 - m)
    inv = 1.0 / (e0 + e1 + e2)
    acc = None
    for e, o_ref in ((e0, o0_ref), (e1, o1_ref), (e2, o2_ref)):
        w = e * inv
        w_hi = w.astype(BF16)
        w_lo = (w - w_hi.astype(F32)).astype(BF16)
        wide = (jnp.dot(w_hi, e_ref[...], preferred_element_type=F32)
                + jnp.dot(w_lo, e_ref[...], preferred_element_type=F32))
        term = wide * o_ref[...].astype(F32)
        acc = term if acc is None else acc + term
    y = jnp.dot(acc.astype(BF16), w_ref[...], preferred_element_type=F32)
    out_ref[...] = x_ref[...] + mod_ref[0, 2:3, :] * y


def _merge_proj(outs, lses, x2, mod, w_out_bf, seq):
    n, d = x2.shape
    tm = TM_PROJ
    per_b = seq // tm
    expand = (jnp.arange(LANES)[:, None] == jnp.arange(d)[None, :] // HEAD_DIM_A).astype(BF16)
    row = lambda width: pl.BlockSpec((tm, width), lambda i: (i, 0))
    return pl.pallas_call(
        _merge_proj_kernel,
        grid=(n // tm,),
        in_specs=[row(d)] * 3 + [row(LANES)] * 3 + [
            row(d),
            pl.BlockSpec((1, 6, d), lambda i: (i // per_b, 0, 0)),
            pl.BlockSpec((LANES, d), lambda i: (0, 0)),
            pl.BlockSpec((d, d), lambda i: (0, 0)),
        ],
        out_specs=row(d),
        out_shape=jax.ShapeDtypeStruct((n, d), F32),
        compiler_params=_params("parallel"),
        name="merge_out_proj",
    )(*outs, *lses, x2, mod, expand, w_out_bf)


def _gla_in_kernel(x_ref, mod_ref, g_ref, w_ref, o_ref):
    h = _norm_mod(x_ref[...], g_ref[...], mod_ref[0, 0:1, :], mod_ref[0, 1:2, :])
    o_ref[...] = jnp.dot(h.astype(BF16), w_ref[...], preferred_element_type=F32).astype(BF16)


def _gla_in_proj(x2, mod, g1, w_bf, seq):
    n, d = x2.shape
    ncol = w_bf.shape[1]
    tm = TM_PROJ
    per_b = seq // tm
    return pl.pallas_call(
        _gla_in_kernel,
        grid=(n // tm,),
        in_specs=[
            pl.BlockSpec((tm, d), lambda i: (i, 0)),
            pl.BlockSpec((1, 6, d), lambda i: (i // per_b, 0, 0)),
            pl.BlockSpec((1, d), lambda i: (0, 0)),
            pl.BlockSpec((d, ncol), lambda i: (0, 0)),
        ],
        out_specs=pl.BlockSpec((tm, ncol), lambda i: (i, 0)),
        out_shape=jax.ShapeDtypeStruct((n, ncol), BF16),
        compiler_params=_params("parallel"),
        name="gla_in_proj",
    )(x2, mod, g1, w_bf)


def _gla_kernel(q_ref, k_ref, v_ref, r_ref, a_ref, wg_ref, gb_ref, og_ref, tri_ref, x_ref, mod_ref, wo_ref,
                out_ref, st_ref, o_scr, la_scr):
    t = pl.program_id(1)
    hk = HEADS_B * KEY_DIM_B
    c = GLA_CHUNK

    @pl.when(t == 0)
    def _():
        st_ref[...] = jnp.zeros_like(st_ref)

    g = jnp.dot(a_ref[0], wg_ref[...], preferred_element_type=F32) + gb_ref[...]
    la_scr[...] = (jnp.minimum(g, 0.0) - jnp.log(1.0 + jnp.exp(-jnp.abs(g)))) * (1.0 / GATE_TAU)

    rr = lax.broadcasted_iota(jnp.int32, (c, c), 0)
    cc = lax.broadcasted_iota(jnp.int32, (c, c), 1)
    causal = cc <= rr

    def chunk(ci, carry):
        rows = pl.ds(pl.multiple_of(ci * c, c), c)
        la = la_scr[rows, :]
        la_hi = la.astype(BF16)
        rem = la - la_hi.astype(F32)
        la_mid = rem.astype(BF16)
        la_lo = (rem - la_mid.astype(F32)).astype(BF16)
        tri = tri_ref[...]
        b = (jnp.dot(tri, la_hi, preferred_element_type=F32) + jnp.dot(tri, la_mid, preferred_element_type=F32)
             + jnp.dot(tri, la_lo, preferred_element_type=F32))
        b_last = b[c - 1:c, :]
        q = q_ref[0, rows, :].astype(F32) * (KEY_DIM_B ** -0.5)
        k = k_ref[0, rows, :].astype(F32)
        q_dec = (q * jnp.exp(b)).astype(BF16)
        k_inv = (k * jnp.exp(-b)).astype(BF16)
        k_dec = (k * jnp.exp(b_last - b)).astype(BF16)
        decay = jnp.exp(b_last)
        v = v_ref[0, rows, :]
        for h in range(HEADS_B):
            ks = slice(h * KEY_DIM_B, (h + 1) * KEY_DIM_B)
            vs = slice(h * VAL_DIM_B, (h + 1) * VAL_DIM_B)
            qd, ki, kd, vh = q_dec[:, ks], k_inv[:, ks], k_dec[:, ks], v[:, vs]
            att = lax.dot_general(qd, ki, (((1,), (1,)), ((), ())), preferred_element_type=F32)
            att = jnp.where(causal, att, 0.0)
            st = st_ref[h]
            o = (jnp.dot(att.astype(BF16), vh, preferred_element_type=F32)
                 + lax.dot_general(qd, st.astype(BF16), (((1,), (1,)), ((), ())), preferred_element_type=F32))
            o_scr[rows, vs] = o
            st_ref[h] = st * decay[:, ks] + lax.dot_general(vh, kd, (((0,), (0,)), ((), ())),
                                                             preferred_element_type=F32)
        return carry

    lax.fori_loop(0, q_ref.shape[1] // c, chunk, 0)

    r = r_ref[0].astype(F32)
    parts = []
    for h in range(HEADS_B):
        vs = slice(h * VAL_DIM_B, (h + 1) * VAL_DIM_B)
        oh = o_scr[:, vs]
        ms = jnp.mean(oh * oh, axis=-1, keepdims=True)
        rh = r[:, vs]
        parts.append((oh * lax.rsqrt(ms + RMS_EPS) * og_ref[...] * (rh * jax.nn.sigmoid(rh))).astype(BF16))
    y = jnp.dot(jnp.concatenate(parts, axis=1), wo_ref[...], preferred_element_type=F32)
    out_ref[0] = x_ref[0] + mod_ref[0, 2:3, :] * y
    del hk


def _gla(proj, x3, mod, wg_bf, gate_bias, out_gain, w_out_bf):
    batch, seq, d = x3.shape
    ncol = proj.shape[1]
    t = T_GLA
    hk = HEADS_B * KEY_DIM_B
    hv = HEADS_B * VAL_DIM_B
    p3 = proj.reshape(batch, seq, ncol)
    tri = (jnp.arange(GLA_CHUNK)[None, :] <= jnp.arange(GLA_CHUNK)[:, None]).astype(BF16)
    a_blk = (2 * hk + 2 * hv) // LANES
    const = lambda shape: pl.BlockSpec(shape, lambda b, i: (0,) * len(shape))
    return pl.pallas_call(
        _gla_kernel,
        grid=(batch, seq // t),
        in_specs=[
            pl.BlockSpec((1, t, hk), lambda b, i: (b, i, 0)),
            pl.BlockSpec((1, t, hk), lambda b, i: (b, i, 1)),
            pl.BlockSpec((1, t, hv), lambda b, i: (b, i, (2 * hk) // hv)),
            pl.BlockSpec((1, t, hv), lambda b, i: (b, i, (2 * hk + hv) // hv)),
            pl.BlockSpec((1, t, LANES), lambda b, i: (b, i, a_blk)),
            const((LANES, hk)),
            const((1, hk)),
            const((1, VAL_DIM_B)),
            const((GLA_CHUNK, GLA_CHUNK)),
            pl.BlockSpec((1, t, d), lambda b, i: (b, i, 0)),
            pl.BlockSpec((1, 6, d), lambda b, i: (b, 0, 0)),
            const((hv, d)),
        ],
        out_specs=pl.BlockSpec((1, t, d), lambda b, i: (b, i, 0)),
        out_shape=jax.ShapeDtypeStruct((batch, seq, d), F32),
        scratch_shapes=[
            pltpu.VMEM((HEADS_B, VAL_DIM_B, KEY_DIM_B), F32),
            pltpu.VMEM((t, hv), F32),
            pltpu.VMEM((t, hk), F32),
        ],
        compiler_params=_params("parallel", "arbitrary"),
        name="gla",
    )(p3, p3, p3, p3, p3, wg_bf, gate_bias, out_gain, tri, x3, mod, w_out_bf)


def _router_kernel(x_ref, mod_ref, g_ref, rw_ref, rb_ref, tri_ref, h_ref, info_ref, cnt_ref, carry_ref):
    i = pl.program_id(0)

    @pl.when(i == 0)
    def _():
        carry_ref[...] = jnp.zeros_like(carry_ref)

    h = _norm_mod(x_ref[...], g_ref[...], mod_ref[0, 3:4, :], mod_ref[0, 4:5, :])
    h_ref[...] = h
    tm = h.shape[0]
    lane = lax.broadcasted_iota(jnp.int32, (tm, LANES), 1)
    lane_f = lane.astype(F32)
    logits = jnp.dot(h, rw_ref[...], precision=HIGHEST, preferred_element_type=F32) + rb_ref[...]
    work = jnp.where(lane < N_EXPERTS, logits, -jnp.inf)
    picks, vals, idxs = [], [], []
    for _ in range(TOP_K):
        m = jnp.max(work, axis=-1, keepdims=True)
        idx = jnp.min(jnp.where(work == m, lane_f, float(LANES)), axis=-1, keepdims=True)
        pick = lane_f == idx
        work = jnp.where(pick, -jnp.inf, work)
        picks.append(pick)
        vals.append(m)
        idxs.append(idx)
    exps = [jnp.exp(v - vals[0]) for v in vals]
    inv = 1.0 / (exps[0] + exps[1] + exps[2] + exps[3])
    chosen = jnp.zeros((tm, LANES), F32)
    for pick in picks:
        chosen = jnp.where(pick, 1.0, chosen)
    before = jnp.dot(tri_ref[...], chosen.astype(BF16), preferred_element_type=F32) + carry_ref[0:1, :]
    info = jnp.zeros((tm, LANES), F32)
    for kk in range(TOP_K):
        rank = jnp.sum(jnp.where(picks[kk], before, 0.0), axis=-1, keepdims=True)
        info = jnp.where(lane == kk, idxs[kk], info)
        info = jnp.where(lane == TOP_K + kk, rank, info)
        info = jnp.where(lane == 2 * TOP_K + kk, exps[kk] * inv, info)
    info_ref[...] = info
    total = carry_ref[0:1, :] + jnp.sum(chosen, axis=0, keepdims=True)
    carry_ref[...] = jnp.broadcast_to(total, carry_ref.shape)
    cnt_ref[...] = jnp.broadcast_to(total, cnt_ref.shape)


def _router(x2, mod, g2, router_w, router_b, seq):
    n, d = x2.shape
    tm = TM_ROUTE
    per_b = seq // tm
    rw = jnp.zeros((d, LANES), F32).at[:, :N_EXPERTS].set(router_w)
    rb = jnp.zeros((1, LANES), F32).at[0, :N_EXPERTS].set(router_b)
    tri = (jnp.arange(tm)[None, :] < jnp.arange(tm)[:, None]).astype(BF16)
    return pl.pallas_call(
        _router_kernel,
        grid=(n // tm,),
        in_specs=[
            pl.BlockSpec((tm, d), lambda i: (i, 0)),
            pl.BlockSpec((1, 6, d), lambda i: (i // per_b, 0, 0)),
            pl.BlockSpec((1, d), lambda i: (0, 0)),
            pl.BlockSpec((d, LANES), lambda i: (0, 0)),
            pl.BlockSpec((1, LANES), lambda i: (0, 0)),
            pl.BlockSpec((tm, tm), lambda i: (0, 0)),
        ],
        out_specs=[
            pl.BlockSpec((tm, d), lambda i: (i, 0)),
            pl.BlockSpec((tm, LANES), lambda i: (i, 0)),
            pl.BlockSpec((8, LANES), lambda i: (0, 0)),
        ],
        out_shape=[
            jax.ShapeDtypeStruct((n, d), F32),
            jax.ShapeDtypeStruct((n, LANES), F32),
            jax.ShapeDtypeStruct((8, LANES), F32),
        ],
        scratch_shapes=[pltpu.VMEM((8, LANES), F32)],
        compiler_params=_params("arbitrary"),
        name="moe_router",
    )(x2, mod, g2, rw, rb, tri)


def _row_copy(src, src_row, dst, dst_row, sem):
    return pltpu.make_async_copy(src.at[pl.ds(src_row, 1)], dst.at[pl.ds(dst_row, 1)], sem)


def _dispatch_kernel(dest_ref, h_hbm, init_hbm, xs_hbm, sem):
    del init_hbm
    tm = dest_ref.shape[2] // TOP_K
    base = pl.program_id(0) * tm

    def start(t, carry):
        for kk in range(TOP_K):
            _row_copy(h_hbm, base + t, xs_hbm, dest_ref[0, 0, TOP_K * t + kk], sem).start()
        return carry

    lax.fori_loop(0, tm, start, 0)

    def wait(t, carry):
        for kk in range(TOP_K):
            _row_copy(h_hbm, 0, xs_hbm, 0, sem).wait()
        return carry

    lax.fori_loop(0, tm, wait, 0)


def _dispatch(h, dest, cap):
    n, d = h.shape
    tm = TM_MOVE
    dest3 = dest.reshape(n // tm, 1, tm * TOP_K)
    return pl.pallas_call(
        _dispatch_kernel,
        grid=(n // tm,),
        in_specs=[
            pl.BlockSpec((1, 1, tm * TOP_K), lambda i: (i, 0, 0), memory_space=pltpu.SMEM),
            pl.BlockSpec(memory_space=pl.ANY),
            pl.BlockSpec(memory_space=pl.ANY),
        ],
        out_specs=pl.BlockSpec(memory_space=pl.ANY),
        out_shape=jax.ShapeDtypeStruct((cap, d), F32),
        scratch_shapes=[pltpu.SemaphoreType.DMA(())],
        input_output_aliases={2: 0},
        compiler_params=_params("arbitrary"),
        name="moe_dispatch",
    )(dest3, h, jnp.zeros((cap, d), F32))


def _expert_kernel(be_ref, na_ref, x_ref, wgu_ref, bgu_ref, wd_ref, bd_ref, o_ref):
    del be_ref

    @pl.when(pl.program_id(0) < na_ref[0])
    def _():
        f = wd_ref.shape[1]
        gu = jnp.dot(x_ref[...].astype(BF16), wgu_ref[0], preferred_element_type=F32) + bgu_ref[0]
        gate = jnp.minimum(gu[:, :f], SWIGLU_LIMIT)
        up = jnp.clip(gu[:, f:], -SWIGLU_LIMIT, SWIGLU_LIMIT)
        act = (up + 1.0) * (gate * jax.nn.sigmoid(SWIGLU_ALPHA * gate))
        o_ref[...] = jnp.dot(act.astype(BF16), wd_ref[0], preferred_element_type=F32) + bd_ref[0]


def _experts(xs, block_e, n_active, wgu_bf, b_gu, wd_bf, b_down):
    cap, d = xs.shape
    e, _, f2 = wgu_bf.shape
    f = f2 // 2
    tm = TM_EXPERT
    nb = cap // tm
    live = lambda i, be, na: (jnp.minimum(i, na[0] - 1), 0)
    return pl.pallas_call(
        _expert_kernel,
        grid_spec=pltpu.PrefetchScalarGridSpec(
            num_scalar_prefetch=2,
            grid=(nb,),
            in_specs=[
                pl.BlockSpec((tm, d), live),
                pl.BlockSpec((1, d, f2), lambda i, be, na: (be[i], 0, 0)),
                pl.BlockSpec((1, 1, f2), lambda i, be, na: (be[i], 0, 0)),
                pl.BlockSpec((1, f, d), lambda i, be, na: (be[i], 0, 0)),
                pl.BlockSpec((1, 1, d), lambda i, be, na: (be[i], 0, 0)),
            ],
            out_specs=pl.BlockSpec((tm, d), live),
        ),
        out_shape=jax.ShapeDtypeStruct((cap, d), F32),
        compiler_params=_params("arbitrary"),
        name="moe_experts",
    )(block_e, n_active, xs, wgu_bf, b_gu.reshape(e, 1, f2), wd_bf, b_down.reshape(e, 1, d))


def _combine_kernel(dest_ref, info_ref, x_ref, mod_ref, ys_hbm, out_ref, buf, sem):
    tm = x_ref.shape[0]

    def start(t, carry):
        for kk in range(TOP_K):
            _row_copy(ys_hbm, dest_ref[0, 0, TOP_K * t + kk], buf.at[kk], t, sem).start()
        return carry

    lax.fori_loop(0, tm, start, 0)

    def wait(t, carry):
        for kk in range(TOP_K):
            _row_copy(ys_hbm, 0, buf.at[kk], 0, sem).wait()
        return carry

    lax.fori_loop(0, tm, wait, 0)

    info = info_ref[...]
    y = None
    for kk in range(TOP_K):
        term = info[:, 2 * TOP_K + kk:2 * TOP_K + kk + 1] * buf[kk]
        y = term if y is None else y + term
    out_ref[...] = x_ref[...] + mod_ref[0, 5:6, :] * y


def _combine(ys, dest, info, x2, mod, seq):
    n, d = x2.shape
    tm = TM_MOVE
    per_b = seq // tm
    dest3 = dest.reshape(n // tm, 1, tm * TOP_K)
    return pl.pallas_call(
        _combine_kernel,
        grid=(n // tm,),
        in_specs=[
            pl.BlockSpec((1, 1, tm * TOP_K), lambda i: (i, 0, 0), memory_space=pltpu.SMEM),
            pl.BlockSpec((tm, LANES), lambda i: (i, 0)),
            pl.BlockSpec((tm, d), lambda i: (i, 0)),
            pl.BlockSpec((1, 6, d), lambda i: (i // per_b, 0, 0)),
            pl.BlockSpec(memory_space=pl.ANY),
        ],
        out_specs=pl.BlockSpec((tm, d), lambda i: (i, 0)),
        out_shape=jax.ShapeDtypeStruct((n, d), F32),
        scratch_shapes=[pltpu.VMEM((TOP_K, tm, d), F32), pltpu.SemaphoreType.DMA(())],
        compiler_params=_params("arbitrary"),
        name="moe_combine",
    )(dest3, info, x2, mod, ys)


def _moe(x2, mod, g2, router_w, router_b, w_gu, b_gu, w_down, b_down, seq):
    n, d = x2.shape
    h, info, cnt = _router(x2, mod, g2, router_w, router_b, seq)
    counts = cnt[0, :N_EXPERTS].astype(jnp.int32)
    padded = (counts + TM_EXPERT - 1) // TM_EXPERT * TM_EXPERT
    seg_end = jnp.cumsum(padded)
    seg_start = seg_end - padded
    cap = n * TOP_K + N_EXPERTS * TM_EXPERT
    nb = cap // TM_EXPERT
    block_e = jnp.minimum(jnp.searchsorted(seg_end, jnp.arange(nb, dtype=jnp.int32) * TM_EXPERT, side="right"),
                          N_EXPERTS - 1).astype(jnp.int32)
    n_active = (seg_end[-1:] // TM_EXPERT).astype(jnp.int32)
    expert = info[:, :TOP_K].astype(jnp.int32)
    rank = info[:, TOP_K:2 * TOP_K].astype(jnp.int32)
    dest = (seg_start[expert] + rank).reshape(-1)
    xs = _dispatch(h, dest, cap)
    ys = _experts(xs, block_e, n_active, w_gu.astype(BF16), b_gu, w_down.astype(BF16), b_down)
    return _combine(ys, dest, info, x2, mod, seq)


def kernel(x, c, positions, ada_w, ada_b, norm1_g, norm2_g, a_w_in, a_q_gain, a_k_gain, a_w_out, b_w_in,
           b_w_gate_up, b_gate_bias, b_out_gain, b_w_out, router_w, router_b, moe_w_gu, moe_b_gu, moe_w_down,
           moe_b_down):
    batch, seq, d = x.shape
    depth = ada_w.shape[0]
    n = batch * seq
    mods = _adaln(c, ada_w, ada_b).reshape(depth, batch, 6, d)
    cos_t, sin_t = _rope_tables(positions)
    x2 = x.reshape(n, d)
    for layer in range(depth):
        mod = mods[layer]
        j = layer // 2
        g1 = norm1_g[layer].reshape(1, d)
        g2 = norm2_g[layer].reshape(1, d)
        if layer % 2 == 0:
            reps = d // HEAD_DIM_A
            ones = jnp.ones((d,), F32)
            gains = []
            for g in range(N_GROUPS):
                gains += [jnp.tile(a_q_gain[j, g], reps) * (HEAD_DIM_A ** -0.5), jnp.tile(a_k_gain[j, g], reps), ones]
            gains = jnp.stack(gains).reshape(3 * N_GROUPS, 1, d)
            qkv = _qkv_proj(x2, mod, g1, a_w_in[j].astype(BF16), gains, cos_t, sin_t, seq)
            outs, lses = zip(*[_dilated_attention(qkv, batch, seq, g) for g in range(N_GROUPS)])
            x2 = _merge_proj(outs, lses, x2, mod, a_w_out[j].astype(BF16), seq)
        else:
            hk = HEADS_B * KEY_DIM_B
            hv = HEADS_B * VAL_DIM_B
            width = 2 * hk + 2 * hv
            w_in = jnp.zeros((d, width + LANES), BF16).at[:, :width + GATE_RANK].set(b_w_in[j].astype(BF16))
            wg = jnp.zeros((LANES, hk), BF16).at[:GATE_RANK].set(b_w_gate_up[j].astype(BF16))
            proj = _gla_in_proj(x2, mod, g1, w_in, seq)
            x3 = _gla(proj, x2.reshape(batch, seq, d), mod, wg, b_gate_bias[j].reshape(1, hk),
                      b_out_gain[j].reshape(1, VAL_DIM_B), b_w_out[j].astype(BF16))
            x2 = x3.reshape(n, d)
        x2 = _moe(x2, mod, g2, router_w[layer], router_b[layer], moe_w_gu[layer], moe_b_gu[layer],
                  moe_w_down[layer], moe_b_down[layer], seq)
    return x2.reshape(batch, seq, d)
```

```python
import functools

import jax
import jax.numpy as jnp
from jax import lax
from jax.experimental import pallas as pl
from jax.experimental.pallas import tpu as pltpu

F32 = jnp.float32
BF16 = jnp.bfloat16
HIGHEST = lax.Precision.HIGHEST

RMS_EPS = 1e-6
DILATIONS = ((128, 1), (512, 4), (2048, 16))
N_GROUPS = 3
HEADS_A = 16
HEAD_DIM_A = 64
BAND = 128
ROPE_THETA = 10000.0
NEG_INF = -1e30
HEADS_B = 4
KEY_DIM_B = 128
VAL_DIM_B = 256
GATE_RANK = 16
GATE_TAU = 16.0
GLA_CHUNK = 64
N_EXPERTS = 32
TOP_K = 4
SWIGLU_LIMIT = 7.0
SWIGLU_ALPHA = 1.702

LANES = 128
VMEM_LIMIT = 56 * 1024 * 1024

TM_PROJ = 512
TM_ROUTE = 512
TM_EXPERT = 512
TM_MOVE = 256
T_GLA = 512


def _params(*sem):
    return pltpu.CompilerParams(dimension_semantics=sem, vmem_limit_bytes=VMEM_LIMIT)


def _norm_mod(x, gain, shift, scale):
    ms = jnp.mean(x * x, axis=-1, keepdims=True)
    y = x * lax.rsqrt(ms + RMS_EPS) * gain
    return y * (1.0 + scale) + shift


def _adaln_kernel(c_ref, w_ref, b_ref, o_ref):
    c = c_ref[...]
    cond = c * jax.nn.sigmoid(c)
    o_ref[0] = jnp.dot(cond, w_ref[0], precision=HIGHEST, preferred_element_type=F32) + b_ref[0]


def _adaln(c, ada_w, ada_b):
    depth, d, n6 = ada_w.shape
    b = c.shape[0]
    tn = 1536
    return pl.pallas_call(
        _adaln_kernel,
        grid=(depth, n6 // tn),
        in_specs=[
            pl.BlockSpec((b, d), lambda l, j: (0, 0)),
            pl.BlockSpec((1, d, tn), lambda l, j: (l, 0, j)),
            pl.BlockSpec((1, 1, tn), lambda l, j: (l, 0, j)),
        ],
        out_specs=pl.BlockSpec((1, b, tn), lambda l, j: (l, 0, j)),
        out_shape=jax.ShapeDtypeStruct((depth, b, n6), F32),
        compiler_params=_params("parallel", "parallel"),
        name="adaln",
    )(c, ada_w, ada_b.reshape(depth, 1, n6))


def _rope_kernel(pos_ref, cos_ref, sin_ref):
    pos = pos_ref[...].astype(F32)
    lane = lax.broadcasted_iota(jnp.int32, (1, LANES), 1)
    half = HEAD_DIM_A // 2
    fidx = (lane & (half - 1)).astype(F32)
    inv_freq = jnp.power(jnp.full((1, LANES), ROPE_THETA, F32), -(fidx / half))
    ang = pos * inv_freq
    cos_ref[...] = jnp.cos(ang)
    s = jnp.sin(ang)
    sin_ref[...] = jnp.where((lane & (HEAD_DIM_A - 1)) < half, -s, s)


def _rope_tables(positions):
    n = positions.size
    tm = 1024
    return pl.pallas_call(
        _rope_kernel,
        grid=(n // tm,),
        in_specs=[pl.BlockSpec((tm, 1), lambda i: (i, 0))],
        out_specs=[pl.BlockSpec((tm, LANES), lambda i: (i, 0))] * 2,
        out_shape=[jax.ShapeDtypeStruct((n, LANES), F32)] * 2,
        compiler_params=_params("parallel"),
        name="rope_tables",
    )(positions.reshape(n, 1))


def _qkv_kernel(x_ref, mod_ref, g_ref, w_ref, gain_ref, cos_ref, sin_ref, bd_ref, *rest):
    o_refs, (h_ref, stage_ref) = rest[:3 * N_GROUPS], rest[3 * N_GROUPS:]
    j = pl.program_id(1)
    tm = x_ref.shape[0]

    @pl.when(j == 0)
    def _():
        h = _norm_mod(x_ref[...], g_ref[...], mod_ref[0, 0:1, :], mod_ref[0, 1:2, :])
        h_ref[...] = h.astype(BF16)

    acc = jnp.dot(h_ref[...], w_ref[...], preferred_element_type=F32)

    def emit(jj):
        g, comp = divmod(jj, 3)
        dil = DILATIONS[g][1]
        out = o_refs[jj]

        def put(chunk, val):
            if dil == 1:
                out[0, 0, :, chunk * LANES:(chunk + 1) * LANES] = val.astype(BF16)
                return
            stage_ref[chunk] = val
            for r in range(dil):
                out[0, r, :, chunk * LANES:(chunk + 1) * LANES] = (
                    stage_ref[chunk, pl.ds(r, tm // dil, stride=dil), :].astype(BF16))

        if comp == 2:
            for chunk in range(acc.shape[1] // LANES):
                put(chunk, acc[:, chunk * LANES:(chunk + 1) * LANES])
        else:
            gain = gain_ref[0]
            cos = cos_ref[...]
            sin = sin_ref[...]
            lane = lax.broadcasted_iota(jnp.int32, (1, LANES), 1)
            first_half = (lane & (HEAD_DIM_A - 1)) < (HEAD_DIM_A // 2)
            for cc in range(4):
                a = acc[:, cc * 256:(cc + 1) * 256]
                ss = jnp.dot((a * a).astype(BF16), bd_ref[...], preferred_element_type=F32)
                qn = a * lax.rsqrt(ss * (1.0 / HEAD_DIM_A) + RMS_EPS) * gain[:, cc * 256:(cc + 1) * 256]
                for hh in range(2):
                    xx = qn[:, hh * LANES:(hh + 1) * LANES]
                    swapped = jnp.where(first_half, pltpu.roll(xx, 96, 1), pltpu.roll(xx, 32, 1))
                    put(2 * cc + hh, xx * cos + swapped * sin)

    for jj in range(3 * N_GROUPS):
        pl.when(j == jj)(functools.partial(emit, jj))


def _qkv_proj(x2, mod, g1, w_bf, gains, cos_t, sin_t, batch, seq):
    n, d = x2.shape
    ncol = w_bf.shape[1]
    tm = TM_PROJ
    per_b = seq // tm
    bd = (jnp.arange(256)[:, None] // HEAD_DIM_A == jnp.arange(256)[None, :] // HEAD_DIM_A).astype(BF16)
    dils = [DILATIONS[jj // 3][1] for jj in range(3 * N_GROUPS)]
    flat = pl.pallas_call(
        _qkv_kernel,
        grid=(n // tm, ncol // 1024),
        in_specs=[
            pl.BlockSpec((tm, d), lambda i, j: (i, 0)),
            pl.BlockSpec((1, 6, d), lambda i, j: (i // per_b, 0, 0)),
            pl.BlockSpec((1, d), lambda i, j: (0, 0)),
            pl.BlockSpec((d, 1024), lambda i, j: (0, j)),
            pl.BlockSpec((1, 1, 1024), lambda i, j: (j, 0, 0)),
            pl.BlockSpec((tm, LANES), lambda i, j: (i, 0)),
            pl.BlockSpec((tm, LANES), lambda i, j: (i, 0)),
            pl.BlockSpec((256, 256), lambda i, j: (0, 0)),
        ],
        out_specs=[pl.BlockSpec((1, dil, tm // dil, 1024), lambda i, j: (i // per_b, 0, i % per_b, 0))
                   for dil in dils],
        out_shape=[jax.ShapeDtypeStruct((batch, dil, seq // dil, 1024), BF16) for dil in dils],
        scratch_shapes=[pltpu.VMEM((tm, d), BF16), pltpu.VMEM((d // LANES, tm, LANES), F32)],
        compiler_params=_params("parallel", "arbitrary"),
        name="qkv_proj",
    )(x2, mod, g1, w_bf, gains, cos_t, sin_t, bd)
    return [flat[3 * g:3 * g + 3] for g in range(N_GROUPS)]


def _attn_kernel(q_ref, kp_ref, kc_ref, vp_ref, vc_ref, o_ref, lse_ref, *, steps):
    q_ref, kp_ref, kc_ref, vp_ref, vc_ref, o_ref, lse_ref = (
        r.at[0] for r in (q_ref, kp_ref, kc_ref, vp_ref, vc_ref, o_ref, lse_ref))
    n = pl.program_id(2)
    row = lax.broadcasted_iota(jnp.int32, (BAND, 2 * BAND), 0)
    col = lax.broadcasted_iota(jnp.int32, (BAND, 2 * BAND), 1)
    delta = row + BAND - col
    valid = (delta >= 0) & (delta <= steps) & ((col >= BAND) | (n > 0))
    lane = lax.broadcasted_iota(jnp.int32, (BAND, LANES), 1)
    first_head = lane < HEAD_DIM_A
    lse_tile = jnp.zeros((BAND, LANES), F32)
    for p in range(HEADS_A // 2):
        sl = slice(p * LANES, (p + 1) * LANES)
        qp = q_ref[0, :, sl]
        kcat = jnp.concatenate([kp_ref[0, :, sl], kc_ref[0, :, sl]], axis=0)
        vcat = jnp.concatenate([vp_ref[0, :, sl], vc_ref[0, :, sl]], axis=0)
        outs = []
        for hh in range(2):
            keep = first_head if hh == 0 else jnp.logical_not(first_head)
            qm = jnp.where(keep, qp, jnp.zeros_like(qp))
            s = lax.dot_general(qm, kcat, (((1,), (1,)), ((), ())), preferred_element_type=F32)
            s = jnp.where(valid, s, NEG_INF)
            m = jnp.max(s, axis=-1, keepdims=True)
            pexp = jnp.exp(s - m)
            den = jnp.sum(pexp, axis=-1, keepdims=True)
            o = jnp.dot(pexp.astype(BF16), vcat, preferred_element_type=F32) / den
            outs.append(o)
            lse_tile = jnp.where(lane == (2 * p + hh), m + jnp.log(den), lse_tile)
        o_ref[0, :, sl] = jnp.where(first_head, outs[0], outs[1]).astype(BF16)
    lse_ref[0] = lse_tile


def _dilated_attention(q, k, v, g):
    window, dil = DILATIONS[g]
    batch, _, l, d = q.shape
    cur = pl.BlockSpec((1, 1, BAND, d), lambda b, r, i: (b, r, i, 0))
    prev = pl.BlockSpec((1, 1, BAND, d), lambda b, r, i: (b, r, jnp.maximum(i - 1, 0), 0))
    return pl.pallas_call(
        functools.partial(_attn_kernel, steps=window // dil),
        grid=(batch, dil, l // BAND),
        in_specs=[cur, prev, cur, prev, cur],
        out_specs=[cur, pl.BlockSpec((1, 1, BAND, LANES), lambda b, r, i: (b, r, i, 0))],
        out_shape=[
            jax.ShapeDtypeStruct((batch, dil, l, d), BF16),
            jax.ShapeDtypeStruct((batch, dil, l, LANES), F32),
        ],
        compiler_params=_params("parallel", "parallel", "arbitrary"),
        name=f"dilated_attn_g{g}",
    )(q, k, k, v, v)


def _split3(v):
    hi = v.astype(BF16)
    rem = v - hi.astype(F32)
    mid = rem.astype(BF16)
    return hi, mid, (rem - mid.astype(F32)).astype(BF16)


def _dot3(mat_bf, v):
    return sum(jnp.dot(mat_bf, part, preferred_element_type=F32) for part in _split3(v))


def _merge_proj_kernel(o0_ref, o1_ref, o2_ref, l0_ref, l1_ref, l2_ref, p1_ref, p2_ref, x_ref, mod_ref, e_ref,
                       w_ref, out_ref):
    tm, d = x_ref.shape
    perms = (None, p1_ref, p2_ref)
    lses, outs = [], []
    for perm, l_ref, o_ref in zip(perms, (l0_ref, l1_ref, l2_ref), (o0_ref, o1_ref, o2_ref)):
        lse = l_ref[0].reshape(tm, LANES)
        o = o_ref[0].reshape(tm, d)
        if perm is None:
            lses.append(lse)
            outs.append(o.astype(F32))
        else:
            lses.append(_dot3(perm[...], lse))
            outs.append(jnp.dot(perm[...], o, preferred_element_type=F32))
    m = jnp.maximum(jnp.maximum(lses[0], lses[1]), lses[2])
    exps = [jnp.exp(l - m) for l in lses]
    inv = 1.0 / (exps[0] + exps[1] + exps[2])
    acc = None
    for e, o in zip(exps, outs):
        w = e * inv
        w_hi = w.astype(BF16)
        w_lo = (w - w_hi.astype(F32)).astype(BF16)
        wide = (jnp.dot(w_hi, e_ref[...], preferred_element_type=F32)
                + jnp.dot(w_lo, e_ref[...], preferred_element_type=F32))
        acc = wide * o if acc is None else acc + wide * o
    y = jnp.dot(acc.astype(BF16), w_ref[...], preferred_element_type=F32)
    out_ref[...] = x_ref[...] + mod_ref[0, 2:3, :] * y


def _merge_proj(outs, lses, x2, mod, w_out_bf, seq):
    n, d = x2.shape
    tm = TM_PROJ
    per_b = seq // tm
    expand = (jnp.arange(LANES)[:, None] == jnp.arange(d)[None, :] // HEAD_DIM_A).astype(BF16)
    tok = jnp.arange(tm)
    perms = [(((tok % dil) * (tm // dil) + tok // dil)[:, None] == tok[None, :]).astype(BF16)
             for _, dil in DILATIONS[1:]]
    slab = lambda dil, width: pl.BlockSpec((1, dil, tm // dil, width), lambda i: (i // per_b, 0, i % per_b, 0))
    const = lambda shape: pl.BlockSpec(shape, lambda i: (0, 0))
    return pl.pallas_call(
        _merge_proj_kernel,
        grid=(n // tm,),
        in_specs=[slab(dil, d) for _, dil in DILATIONS] + [slab(dil, LANES) for _, dil in DILATIONS] + [
            const((tm, tm)),
            const((tm, tm)),
            pl.BlockSpec((tm, d), lambda i: (i, 0)),
            pl.BlockSpec((1, 6, d), lambda i: (i // per_b, 0, 0)),
            const((LANES, d)),
            const((d, d)),
        ],
        out_specs=pl.BlockSpec((tm, d), lambda i: (i, 0)),
        out_shape=jax.ShapeDtypeStruct((n, d), F32),
        compiler_params=_params("parallel"),
        name="merge_out_proj",
    )(*outs, *lses, *perms, x2, mod, expand, w_out_bf)


def _gla_in_kernel(x_ref, mod_ref, g_ref, w_ref, o_ref):
    h = _norm_mod(x_ref[...], g_ref[...], mod_ref[0, 0:1, :], mod_ref[0, 1:2, :])
    o_ref[...] = jnp.dot(h.astype(BF16), w_ref[...], preferred_element_type=F32).astype(BF16)


def _gla_in_proj(x2, mod, g1, w_bf, seq):
    n, d = x2.shape
    ncol = w_bf.shape[1]
    tm = TM_PROJ
    per_b = seq // tm
    return pl.pallas_call(
        _gla_in_kernel,
        grid=(n // tm,),
        in_specs=[
            pl.BlockSpec((tm, d), lambda i: (i, 0)),
            pl.BlockSpec((1, 6, d), lambda i: (i // per_b, 0, 0)),
            pl.BlockSpec((1, d), lambda i: (0, 0)),
            pl.BlockSpec((d, ncol), lambda i: (0, 0)),
        ],
        out_specs=pl.BlockSpec((tm, ncol), lambda i: (i, 0)),
        out_shape=jax.ShapeDtypeStruct((n, ncol), BF16),
        compiler_params=_params("parallel"),
        name="gla_in_proj",
    )(x2, mod, g1, w_bf)


def _gla_kernel(q_ref, k_ref, v_ref, r_ref, a_ref, wg_ref, gb_ref, og_ref, tri_ref, x_ref, mod_ref, wo_ref,
                out_ref, st_ref, o_scr, la_scr):
    t = pl.program_id(1)
    c = GLA_CHUNK

    @pl.when(t == 0)
    def _():
        st_ref[...] = jnp.zeros_like(st_ref)

    g = jnp.dot(a_ref[0], wg_ref[...], preferred_element_type=F32) + gb_ref[...]
    la_scr[...] = (jnp.minimum(g, 0.0) - jnp.log(1.0 + jnp.exp(-jnp.abs(g)))) * (1.0 / GATE_TAU)

    rr = lax.broadcasted_iota(jnp.int32, (c, c), 0)
    cc = lax.broadcasted_iota(jnp.int32, (c, c), 1)
    causal = cc <= rr

    def chunk(ci, carry):
        rows = pl.ds(pl.multiple_of(ci * c, c), c)
        la = la_scr[rows, :]
        b = _dot3(tri_ref[...], la)
        b_last = b[c - 1:c, :]
        q = q_ref[0, rows, :].astype(F32) * (KEY_DIM_B ** -0.5)
        k = k_ref[0, rows, :].astype(F32)
        q_dec = (q * jnp.exp(b)).astype(BF16)
        k_inv = (k * jnp.exp(-b)).astype(BF16)
        k_dec = (k * jnp.exp(b_last - b)).astype(BF16)
        decay = jnp.exp(b_last)
        v = v_ref[0, rows, :]
        for h in range(HEADS_B):
            ks = slice(h * KEY_DIM_B, (h + 1) * KEY_DIM_B)
            vs = slice(h * VAL_DIM_B, (h + 1) * VAL_DIM_B)
            qd, ki, kd, vh = q_dec[:, ks], k_inv[:, ks], k_dec[:, ks], v[:, vs]
            att = lax.dot_general(qd, ki, (((1,), (1,)), ((), ())), preferred_element_type=F32)
            att = jnp.where(causal, att, 0.0)
            st = st_ref[h]
            o = (jnp.dot(att.astype(BF16), vh, preferred_element_type=F32)
                 + lax.dot_general(qd, st.astype(BF16), (((1,), (1,)), ((), ())), preferred_element_type=F32))
            o_scr[rows, vs] = o
            st_ref[h] = st * decay[:, ks] + lax.dot_general(vh, kd, (((0,), (0,)), ((), ())),
                                                             preferred_element_type=F32)
        return carry

    lax.fori_loop(0, q_ref.shape[1] // c, chunk, 0)

    r = r_ref[0].astype(F32)
    parts = []
    for h in range(HEADS_B):
        vs = slice(h * VAL_DIM_B, (h + 1) * VAL_DIM_B)
        oh = o_scr[:, vs]
        ms = jnp.mean(oh * oh, axis=-1, keepdims=True)
        rh = r[:, vs]
        parts.append((oh * lax.rsqrt(ms + RMS_EPS) * og_ref[...] * (rh * jax.nn.sigmoid(rh))).astype(BF16))
    y = jnp.dot(jnp.concatenate(parts, axis=1), wo_ref[...], preferred_element_type=F32)
    out_ref[0] = x_ref[0] + mod_ref[0, 2:3, :] * y


def _gla(proj, x3, mod, wg_bf, gate_bias, out_gain, w_out_bf):
    batch, seq, d = x3.shape
    ncol = proj.shape[1]
    t = T_GLA
    hk = HEADS_B * KEY_DIM_B
    hv = HEADS_B * VAL_DIM_B
    p3 = proj.reshape(batch, seq, ncol)
    tri = (jnp.arange(GLA_CHUNK)[None, :] <= jnp.arange(GLA_CHUNK)[:, None]).astype(BF16)
    a_blk = (2 * hk + 2 * hv) // LANES
    const = lambda shape: pl.BlockSpec(shape, lambda b, i: (0,) * len(shape))
    return pl.pallas_call(
        _gla_kernel,
        grid=(batch, seq // t),
        in_specs=[
            pl.BlockSpec((1, t, hk), lambda b, i: (b, i, 0)),
            pl.BlockSpec((1, t, hk), lambda b, i: (b, i, 1)),
            pl.BlockSpec((1, t, hv), lambda b, i: (b, i, (2 * hk) // hv)),
            pl.BlockSpec((1, t, hv), lambda b, i: (b, i, (2 * hk + hv) // hv)),
            pl.BlockSpec((1, t, LANES), lambda b, i: (b, i, a_blk)),
            const((LANES, hk)),
            const((1, hk)),
            const((1, VAL_DIM_B)),
            const((GLA_CHUNK, GLA_CHUNK)),
            pl.BlockSpec((1, t, d), lambda b, i: (b, i, 0)),
            pl.BlockSpec((1, 6, d), lambda b, i: (b, 0, 0)),
            const((hv, d)),
        ],
        out_specs=pl.BlockSpec((1, t, d), lambda b, i: (b, i, 0)),
        out_shape=jax.ShapeDtypeStruct((batch, seq, d), F32),
        scratch_shapes=[
            pltpu.VMEM((HEADS_B, VAL_DIM_B, KEY_DIM_B), F32),
            pltpu.VMEM((t, hv), F32),
            pltpu.VMEM((t, hk), F32),
        ],
        compiler_params=_params("parallel", "arbitrary"),
        name="gla",
    )(p3, p3, p3, p3, p3, wg_bf, gate_bias, out_gain, tri, x3, mod, w_out_bf)


def _router_kernel(x_ref, mod_ref, g_ref, rw_ref, rb_ref, tri_ref, h_ref, info_ref, cnt_ref, carry_ref):
    i = pl.program_id(0)

    @pl.when(i == 0)
    def _():
        carry_ref[...] = jnp.zeros_like(carry_ref)

    h = _norm_mod(x_ref[...], g_ref[...], mod_ref[0, 3:4, :], mod_ref[0, 4:5, :])
    h_ref[...] = h
    tm = h.shape[0]
    lane = lax.broadcasted_iota(jnp.int32, (tm, LANES), 1)
    lane_f = lane.astype(F32)
    logits = jnp.dot(h, rw_ref[...], precision=HIGHEST, preferred_element_type=F32) + rb_ref[...]
    work = jnp.where(lane < N_EXPERTS, logits, -jnp.inf)
    picks, vals, idxs = [], [], []
    for _ in range(TOP_K):
        m = jnp.max(work, axis=-1, keepdims=True)
        idx = jnp.min(jnp.where(work == m, lane_f, float(LANES)), axis=-1, keepdims=True)
        pick = lane_f == idx
        work = jnp.where(pick, -jnp.inf, work)
        picks.append(pick)
        vals.append(m)
        idxs.append(idx)
    exps = [jnp.exp(v - vals[0]) for v in vals]
    inv = 1.0 / (exps[0] + exps[1] + exps[2] + exps[3])
    chosen = jnp.zeros((tm, LANES), F32)
    for pick in picks:
        chosen = jnp.where(pick, 1.0, chosen)
    before = jnp.dot(tri_ref[...], chosen.astype(BF16), preferred_element_type=F32) + carry_ref[0:1, :]
    info = jnp.zeros((tm, LANES), F32)
    for kk in range(TOP_K):
        rank = jnp.sum(jnp.where(picks[kk], before, 0.0), axis=-1, keepdims=True)
        info = jnp.where(lane == kk, idxs[kk], info)
        info = jnp.where(lane == TOP_K + kk, rank, info)
        info = jnp.where(lane == 2 * TOP_K + kk, exps[kk] * inv, info)
    info_ref[...] = info
    total = carry_ref[0:1, :] + jnp.sum(chosen, axis=0, keepdims=True)
    carry_ref[...] = jnp.broadcast_to(total, carry_ref.shape)
    cnt_ref[...] = jnp.broadcast_to(total, cnt_ref.shape)


def _router(x2, mod, g2, router_w, router_b, seq):
    n, d = x2.shape
    tm = TM_ROUTE
    per_b = seq // tm
    rw = jnp.zeros((d, LANES), F32).at[:, :N_EXPERTS].set(router_w)
    rb = jnp.zeros((1, LANES), F32).at[0, :N_EXPERTS].set(router_b)
    tri = (jnp.arange(tm)[None, :] < jnp.arange(tm)[:, None]).astype(BF16)
    return pl.pallas_call(
        _router_kernel,
        grid=(n // tm,),
        in_specs=[
            pl.BlockSpec((tm, d), lambda i: (i, 0)),
            pl.BlockSpec((1, 6, d), lambda i: (i // per_b, 0, 0)),
            pl.BlockSpec((1, d), lambda i: (0, 0)),
            pl.BlockSpec((d, LANES), lambda i: (0, 0)),
            pl.BlockSpec((1, LANES), lambda i: (0, 0)),
            pl.BlockSpec((tm, tm), lambda i: (0, 0)),
        ],
        out_specs=[
            pl.BlockSpec((tm, d), lambda i: (i, 0)),
            pl.BlockSpec((tm, LANES), lambda i: (i, 0)),
            pl.BlockSpec((8, LANES), lambda i: (0, 0)),
        ],
        out_shape=[
            jax.ShapeDtypeStruct((n, d), F32),
            jax.ShapeDtypeStruct((n, LANES), F32),
            jax.ShapeDtypeStruct((8, LANES), F32),
        ],
        scratch_shapes=[pltpu.VMEM((8, LANES), F32)],
        compiler_params=_params("arbitrary"),
        name="moe_router",
    )(x2, mod, g2, rw, rb, tri)


def _row_copy(src, src_row, dst, dst_row, sem):
    return pltpu.make_async_copy(src.at[pl.ds(src_row, 1)], dst.at[pl.ds(dst_row, 1)], sem)


def _dispatch_kernel(seg_end_ref, padded_ref, dest_ref, h_ref, xs_hbm, zero_ref, sem, zero_sem):
    tm = h_ref.shape[0]

    @pl.when(pl.program_id(0) == 0)
    def _():
        zero_ref[...] = jnp.zeros_like(zero_ref)

        def tail(e):
            return pltpu.make_async_copy(
                zero_ref, xs_hbm.at[pl.ds(pl.multiple_of(seg_end_ref[e] - TM_EXPERT, TM_EXPERT), TM_EXPERT)], zero_sem)

        def start(e, carry):
            pl.when(padded_ref[e] > 0)(lambda: tail(e).start())
            return carry

        def wait(e, carry):
            pl.when(padded_ref[e] > 0)(lambda: tail(e).wait())
            return carry

        lax.fori_loop(0, N_EXPERTS, start, 0)
        lax.fori_loop(0, N_EXPERTS, wait, 0)

        def spare(blk):
            return pltpu.make_async_copy(
                zero_ref, xs_hbm.at[pl.ds(pl.multiple_of(blk * TM_EXPERT, TM_EXPERT), TM_EXPERT)], zero_sem)

        used = seg_end_ref[N_EXPERTS - 1] // TM_EXPERT
        total = xs_hbm.shape[0] // TM_EXPERT
        lax.fori_loop(used, total, lambda blk, carry: (spare(blk).start(), carry)[1], 0)
        lax.fori_loop(used, total, lambda blk, carry: (spare(blk).wait(), carry)[1], 0)

    def start(t, carry):
        for kk in range(TOP_K):
            _row_copy(h_ref, t, xs_hbm, dest_ref[0, 0, TOP_K * t + kk], sem).start()
        return carry

    lax.fori_loop(0, tm, start, 0)

    def wait(t, carry):
        for kk in range(TOP_K):
            _row_copy(h_ref, 0, xs_hbm, 0, sem).wait()
        return carry

    lax.fori_loop(0, tm, wait, 0)


def _dispatch(h, dest, seg_end, padded, cap):
    n, d = h.shape
    tm = TM_MOVE
    dest3 = dest.reshape(n // tm, 1, tm * TOP_K)
    return pl.pallas_call(
        _dispatch_kernel,
        grid_spec=pltpu.PrefetchScalarGridSpec(
            num_scalar_prefetch=2,
            grid=(n // tm,),
            in_specs=[
                pl.BlockSpec((1, 1, tm * TOP_K), lambda i, se, pd: (i, 0, 0), memory_space=pltpu.SMEM),
                pl.BlockSpec((tm, d), lambda i, se, pd: (i, 0)),
            ],
            out_specs=pl.BlockSpec(memory_space=pl.ANY),
            scratch_shapes=[pltpu.VMEM((TM_EXPERT, d), F32), pltpu.SemaphoreType.DMA(()),
                            pltpu.SemaphoreType.DMA(())],
        ),
        out_shape=jax.ShapeDtypeStruct((cap, d), F32),
        compiler_params=_params("arbitrary"),
        name="moe_dispatch",
    )(seg_end, padded, dest3, h)


def _expert_kernel(be_ref, na_ref, x_ref, wgu_ref, bgu_ref, wd_ref, bd_ref, o_ref):
    del be_ref
    live = pl.program_id(0) < na_ref[0]

    @pl.when(jnp.logical_not(live))
    def _():
        o_ref[...] = jnp.zeros_like(o_ref)

    @pl.when(live)
    def _():
        f = wd_ref.shape[1]
        gu = jnp.dot(x_ref[...].astype(BF16), wgu_ref[0], preferred_element_type=F32) + bgu_ref[0]
        gate = jnp.minimum(gu[:, :f], SWIGLU_LIMIT)
        up = jnp.clip(gu[:, f:], -SWIGLU_LIMIT, SWIGLU_LIMIT)
        act = (up + 1.0) * (gate * jax.nn.sigmoid(SWIGLU_ALPHA * gate))
        o_ref[...] = jnp.dot(act.astype(BF16), wd_ref[0], preferred_element_type=F32) + bd_ref[0]


def _experts(xs, block_e, n_active, wgu_bf, b_gu, wd_bf, b_down):
    cap, d = xs.shape
    e, _, f2 = wgu_bf.shape
    f = f2 // 2
    tm = TM_EXPERT
    nb = cap // tm
    live = lambda i, be, na: (jnp.minimum(i, na[0] - 1), 0)
    return pl.pallas_call(
        _expert_kernel,
        grid_spec=pltpu.PrefetchScalarGridSpec(
            num_scalar_prefetch=2,
            grid=(nb,),
            in_specs=[
                pl.BlockSpec((tm, d), live),
                pl.BlockSpec((1, d, f2), lambda i, be, na: (be[i], 0, 0)),
                pl.BlockSpec((1, 1, f2), lambda i, be, na: (be[i], 0, 0)),
                pl.BlockSpec((1, f, d), lambda i, be, na: (be[i], 0, 0)),
                pl.BlockSpec((1, 1, d), lambda i, be, na: (be[i], 0, 0)),
            ],
            out_specs=pl.BlockSpec((tm, d), lambda i, be, na: (i, 0)),
        ),
        out_shape=jax.ShapeDtypeStruct((cap, d), F32),
        compiler_params=_params("arbitrary"),
        name="moe_experts",
    )(block_e, n_active, xs, wgu_bf, b_gu.reshape(e, 1, f2), wd_bf, b_down.reshape(e, 1, d))


def _combine_kernel(dest_ref, info_ref, x_ref, mod_ref, ys_hbm, out_ref, buf, sem):
    tm = x_ref.shape[0]

    def start(t, carry):
        for kk in range(TOP_K):
            _row_copy(ys_hbm, dest_ref[0, 0, TOP_K * t + kk], buf.at[kk], t, sem).start()
        return carry

    lax.fori_loop(0, tm, start, 0)

    def wait(t, carry):
        for kk in range(TOP_K):
            _row_copy(ys_hbm, 0, buf.at[kk], 0, sem).wait()
        return carry

    lax.fori_loop(0, tm, wait, 0)

    info = info_ref[...]
    y = None
    for kk in range(TOP_K):
        term = info[:, 2 * TOP_K + kk:2 * TOP_K + kk + 1] * buf[kk]
        y = term if y is None else y + term
    out_ref[...] = x_ref[...] + mod_ref[0, 5:6, :] * y


def _combine(ys, dest, info, x2, mod, seq):
    n, d = x2.shape
    tm = TM_MOVE
    per_b = seq // tm
    dest3 = dest.reshape(n // tm, 1, tm * TOP_K)
    return pl.pallas_call(
        _combine_kernel,
        grid=(n // tm,),
        in_specs=[
            pl.BlockSpec((1, 1, tm * TOP_K), lambda i: (i, 0, 0), memory_space=pltpu.SMEM),
            pl.BlockSpec((tm, LANES), lambda i: (i, 0)),
            pl.BlockSpec((tm, d), lambda i: (i, 0)),
            pl.BlockSpec((1, 6, d), lambda i: (i // per_b, 0, 0)),
            pl.BlockSpec(memory_space=pl.ANY),
        ],
        out_specs=pl.BlockSpec((tm, d), lambda i: (i, 0)),
        out_shape=jax.ShapeDtypeStruct((n, d), F32),
        scratch_shapes=[pltpu.VMEM((TOP_K, tm, d), F32), pltpu.SemaphoreType.DMA(())],
        compiler_params=_params("arbitrary"),
        name="moe_combine",
    )(dest3, info, x2, mod, ys)


def _moe(x2, mod, g2, router_w, router_b, w_gu, b_gu, w_down, b_down, seq):
    n, d = x2.shape
    h, info, cnt = _router(x2, mod, g2, router_w, router_b, seq)
    counts = cnt[0, :N_EXPERTS].astype(jnp.int32)
    padded = (counts + TM_EXPERT - 1) // TM_EXPERT * TM_EXPERT
    seg_end = jnp.cumsum(padded)
    seg_start = seg_end - padded
    cap = n * TOP_K + N_EXPERTS * TM_EXPERT
    nb = cap // TM_EXPERT
    block_start = jnp.arange(nb, dtype=jnp.int32) * TM_EXPERT
    block_e = jnp.minimum(jnp.sum(seg_end[None, :] <= block_start[:, None], axis=1), N_EXPERTS - 1).astype(jnp.int32)
    n_active = (seg_end[-1:] // TM_EXPERT).astype(jnp.int32)
    expert = info[:, :TOP_K].astype(jnp.int32)
    rank = info[:, TOP_K:2 * TOP_K].astype(jnp.int32)
    dest = (seg_start[expert] + rank).reshape(-1)
    xs = _dispatch(h, dest, seg_end.astype(jnp.int32), padded.astype(jnp.int32), cap)
    ys = _experts(xs, block_e, n_active, w_gu.astype(BF16), b_gu, w_down.astype(BF16), b_down)
    return _combine(ys, dest, info, x2, mod, seq)


def kernel(x, c, positions, ada_w, ada_b, norm1_g, norm2_g, a_w_in, a_q_gain, a_k_gain, a_w_out, b_w_in,
           b_w_gate_up, b_gate_bias, b_out_gain, b_w_out, router_w, router_b, moe_w_gu, moe_b_gu, moe_w_down,
           moe_b_down):
    batch, seq, d = x.shape
    depth = ada_w.shape[0]
    n = batch * seq
    mods = _adaln(c, ada_w, ada_b).reshape(depth, batch, 6, d)
    cos_t, sin_t = _rope_tables(positions)
    x2 = x.reshape(n, d)
    for layer in range(depth):
        mod = mods[layer]
        j = layer // 2
        g1 = norm1_g[layer].reshape(1, d)
        g2 = norm2_g[layer].reshape(1, d)
        if layer % 2 == 0:
            reps = d // HEAD_DIM_A
            ones = jnp.ones((d,), F32)
            gains = []
            for g in range(N_GROUPS):
                gains += [jnp.tile(a_q_gain[j, g], reps) * (HEAD_DIM_A ** -0.5), jnp.tile(a_k_gain[j, g], reps), ones]
            gains = jnp.stack(gains).reshape(3 * N_GROUPS, 1, d)
            qkv = _qkv_proj(x2, mod, g1, a_w_in[j].astype(BF16), gains, cos_t, sin_t, batch, seq)
            outs, lses = zip(*[_dilated_attention(*qkv[g], g) for g in range(N_GROUPS)])
            x2 = _merge_proj(outs, lses, x2, mod, a_w_out[j].astype(BF16), seq)
        else:
            hk = HEADS_B * KEY_DIM_B
            hv = HEADS_B * VAL_DIM_B
            width = 2 * hk + 2 * hv
            w_in = jnp.zeros((d, width + LANES), BF16).at[:, :width + GATE_RANK].set(b_w_in[j].astype(BF16))
            wg = jnp.zeros((LANES, hk), BF16).at[:GATE_RANK].set(b_w_gate_up[j].astype(BF16))
            proj = _gla_in_proj(x2, mod, g1, w_in, seq)
            x3 = _gla(proj, x2.reshape(batch, seq, d), mod, wg, b_gate_bias[j].reshape(1, hk),
                      b_out_gain[j].reshape(1, VAL_DIM_B), b_w_out[j].astype(BF16))
            x2 = x3.reshape(n, d)
        x2 = _moe(x2, mod, g2, router_w[layer], router_b[layer], moe_w_gu[layer], moe_b_gu[layer],
                  moe_w_down[layer], moe_b_down[layer], seq)
    return x2.reshape(batch, seq, d)
```

```python
import functools

import jax
import jax.numpy as jnp
from jax import lax
from jax.experimental import pallas as pl
from jax.experimental.pallas import tpu as pltpu

F32 = jnp.float32
BF16 = jnp.bfloat16
HIGHEST = lax.Precision.HIGHEST

RMS_EPS = 1e-6
DILATIONS = ((128, 1), (512, 4), (2048, 16))
N_GROUPS = 3
HEADS_A = 16
HEAD_DIM_A = 64
BAND = 128
ROPE_THETA = 10000.0
NEG_INF = -1e30
HEADS_B = 4
KEY_DIM_B = 128
VAL_DIM_B = 256
GATE_RANK = 16
GATE_TAU = 16.0
GLA_CHUNK = 64
N_EXPERTS = 32
TOP_K = 4
SWIGLU_LIMIT = 7.0
SWIGLU_ALPHA = 1.702

LANES = 128
VMEM_LIMIT = 56 * 1024 * 1024

TM_PROJ = 512
TM_QKV = 256
TM_ROUTE = 512
TM_EXPERT = 512
TM_MOVE = 256
T_GLA = 512


def _params(*sem):
    return pltpu.CompilerParams(dimension_semantics=sem, vmem_limit_bytes=VMEM_LIMIT)


def _norm_mod(x, gain, shift, scale):
    ms = jnp.mean(x * x, axis=-1, keepdims=True)
    y = x * lax.rsqrt(ms + RMS_EPS) * gain
    return y * (1.0 + scale) + shift


def _adaln_kernel(c_ref, w_ref, b_ref, o_ref):
    c = c_ref[...]
    cond = c * jax.nn.sigmoid(c)
    o_ref[0] = jnp.dot(cond, w_ref[0], precision=HIGHEST, preferred_element_type=F32) + b_ref[0]


def _adaln(c, ada_w, ada_b):
    depth, d, n6 = ada_w.shape
    b = c.shape[0]
    tn = 1536
    return pl.pallas_call(
        _adaln_kernel,
        grid=(depth, n6 // tn),
        in_specs=[
            pl.BlockSpec((b, d), lambda l, j: (0, 0)),
            pl.BlockSpec((1, d, tn), lambda l, j: (l, 0, j)),
            pl.BlockSpec((1, 1, tn), lambda l, j: (l, 0, j)),
        ],
        out_specs=pl.BlockSpec((1, b, tn), lambda l, j: (l, 0, j)),
        out_shape=jax.ShapeDtypeStruct((depth, b, n6), F32),
        compiler_params=_params("parallel", "parallel"),
        name="adaln",
    )(c, ada_w, ada_b.reshape(depth, 1, n6))


def _rope_kernel(pos_ref, cos_ref, sin_ref):
    pos = pos_ref[...].astype(F32)
    lane = lax.broadcasted_iota(jnp.int32, (1, LANES), 1)
    half = HEAD_DIM_A // 2
    fidx = (lane & (half - 1)).astype(F32)
    inv_freq = jnp.power(jnp.full((1, LANES), ROPE_THETA, F32), -(fidx / half))
    ang = pos * inv_freq
    cos_ref[...] = jnp.cos(ang)
    sin_ref[...] = jnp.sin(ang)


def _rope_tables(positions):
    n = positions.size
    tm = 1024
    return pl.pallas_call(
        _rope_kernel,
        grid=(n // tm,),
        in_specs=[pl.BlockSpec((tm, 1), lambda i: (i, 0))],
        out_specs=[pl.BlockSpec((tm, LANES), lambda i: (i, 0))] * 2,
        out_shape=[jax.ShapeDtypeStruct((n, LANES), F32)] * 2,
        compiler_params=_params("parallel"),
        name="rope_tables",
    )(positions.reshape(n, 1))


def _stream_order(tm, dil):
    s = jnp.arange(tm)
    return (s % (tm // dil)) * dil + s // (tm // dil)


def _qkv_kernel(x_ref, mod_ref, g_ref, w_hbm, gain_ref, cos_ref, sin_ref, bd_ref, p1_ref, p2_ref, *rest):
    o_refs, (w_ref, w_sem) = rest[:3 * N_GROUPS], rest[3 * N_GROUPS:]
    tm, d = x_ref.shape
    half = d // 2

    @pl.when(pl.program_id(0) == 0)
    def _():
        copy = pltpu.make_async_copy(w_hbm, w_ref, w_sem)
        copy.start()
        copy.wait()

    h = _norm_mod(x_ref[...], g_ref[...], mod_ref[0, 0:1, :], mod_ref[0, 1:2, :]).astype(BF16)
    hs = [h] + [jnp.dot(p[...], h, preferred_element_type=F32).astype(BF16) for p in (p1_ref, p2_ref)]
    for jj in range(3 * N_GROUPS):
        g, comp = divmod(jj, 3)
        dil = DILATIONS[g][1]
        rows = tm // dil
        out = o_refs[jj]
        acc = jnp.dot(hs[g], w_ref[:, jj * d:(jj + 1) * d], preferred_element_type=F32)
        if comp == 2:
            val = acc.astype(BF16)
            for r in range(dil):
                out[0, r] = val[r * rows:(r + 1) * rows]
            continue
        gain = gain_ref[jj]
        cos = cos_ref[g]
        sin = sin_ref[g]
        for cc in range(half // 256):
            lo, hi = slice(cc * 256, (cc + 1) * 256), slice(half + cc * 256, half + (cc + 1) * 256)
            a1, a2 = acc[:, lo], acc[:, hi]
            ss = jnp.dot((a1 * a1 + a2 * a2).astype(BF16), bd_ref[...], preferred_element_type=F32)
            inv = lax.rsqrt(ss * (1.0 / HEAD_DIM_A) + RMS_EPS)
            n1 = a1 * inv * gain[:, lo]
            n2 = a2 * inv * gain[:, hi]
            for hh in range(2):
                ls = slice(hh * LANES, (hh + 1) * LANES)
                o1 = (n1[:, ls] * cos - n2[:, ls] * sin).astype(BF16)
                o2 = (n2[:, ls] * cos + n1[:, ls] * sin).astype(BF16)
                c1 = cc * 256 + hh * LANES
                for r in range(dil):
                    out[0, r, :, c1:c1 + LANES] = o1[r * rows:(r + 1) * rows]
                    out[0, r, :, half + c1:half + c1 + LANES] = o2[r * rows:(r + 1) * rows]


def _split_half_columns(w):
    lead = w.shape[:-1]
    w = w.reshape(*lead, HEADS_A, 2, HEAD_DIM_A // 2)
    return jnp.swapaxes(w, -3, -2).reshape(*lead, HEADS_A * HEAD_DIM_A)


def _qkv_proj(x2, mod, g1, w_in, q_gain, k_gain, positions, batch, seq):
    n, d = x2.shape
    tm = TM_QKV
    per_b = seq // tm
    reps = d // HEAD_DIM_A
    w = w_in.reshape(d, N_GROUPS, 3, d)
    w = jnp.concatenate([_split_half_columns(w[:, :, :2]), w[:, :, 2:]], axis=2).reshape(d, 3 * N_GROUPS * d)
    gains = []
    for g in range(N_GROUPS):
        gains += [_split_half_columns(jnp.tile(q_gain[g], reps)) * (HEAD_DIM_A ** -0.5),
                  _split_half_columns(jnp.tile(k_gain[g], reps)), jnp.ones((d,), F32)]
    gains = jnp.stack(gains).reshape(3 * N_GROUPS, 1, d)
    orders = [_stream_order(tm, dil) for _, dil in DILATIONS]
    pos_tiles = positions.reshape(n // tm, tm)
    cos_t, sin_t = _rope_tables(jnp.stack([pos_tiles[:, o] for o in orders]))
    cos_t, sin_t = cos_t.reshape(N_GROUPS, n, LANES), sin_t.reshape(N_GROUPS, n, LANES)
    perms = [(o[:, None] == jnp.arange(tm)[None, :]).astype(BF16) for o in orders[1:]]
    bd = (jnp.arange(256)[:, None] // 32 == jnp.arange(256)[None, :] // 32).astype(BF16)
    dils = [DILATIONS[jj // 3][1] for jj in range(3 * N_GROUPS)]
    const = lambda shape: pl.BlockSpec(shape, lambda i: (0,) * len(shape))
    flat = pl.pallas_call(
        _qkv_kernel,
        grid=(n // tm,),
        in_specs=[
            pl.BlockSpec((tm, d), lambda i: (i, 0)),
            pl.BlockSpec((1, 6, d), lambda i: (i // per_b, 0, 0)),
            const((1, d)),
            pl.BlockSpec(memory_space=pl.ANY),
            const((3 * N_GROUPS, 1, d)),
            pl.BlockSpec((N_GROUPS, tm, LANES), lambda i: (0, i, 0)),
            pl.BlockSpec((N_GROUPS, tm, LANES), lambda i: (0, i, 0)),
            const((256, 256)),
            const((tm, tm)),
            const((tm, tm)),
        ],
        out_specs=[pl.BlockSpec((1, dil, tm // dil, d), lambda i: (i // per_b, 0, i % per_b, 0)) for dil in dils],
        out_shape=[jax.ShapeDtypeStruct((batch, dil, seq // dil, d), BF16) for dil in dils],
        scratch_shapes=[pltpu.VMEM((d, 3 * N_GROUPS * d), BF16), pltpu.SemaphoreType.DMA(())],
        compiler_params=_params("arbitrary"),
        name="qkv_proj",
    )(x2, mod, g1, w.astype(BF16), gains, cos_t, sin_t, bd, *perms)
    return [flat[3 * g:3 * g + 3] for g in range(N_GROUPS)]


def _attn_kernel(q_ref, kp_ref, kc_ref, vp_ref, vc_ref, o_ref, lse_ref, *, steps):
    q_ref, kp_ref, kc_ref, vp_ref, vc_ref, o_ref, lse_ref = (
        r.at[0] for r in (q_ref, kp_ref, kc_ref, vp_ref, vc_ref, o_ref, lse_ref))
    n = pl.program_id(2)
    row = lax.broadcasted_iota(jnp.int32, (BAND, 2 * BAND), 0)
    col = lax.broadcasted_iota(jnp.int32, (BAND, 2 * BAND), 1)
    delta = row + BAND - col
    valid = (delta >= 0) & (delta <= steps) & ((col >= BAND) | (n > 0))
    lane = lax.broadcasted_iota(jnp.int32, (BAND, LANES), 1)
    first_head = lane < HEAD_DIM_A
    lane2 = lax.broadcasted_iota(jnp.int32, (BAND, 2 * LANES), 1)
    half = q_ref.shape[2] // 2
    per_slab = LANES // (HEAD_DIM_A // 2)
    lse_tile = jnp.zeros((BAND, LANES), F32)
    for hg in range(HEADS_A // per_slab):
        sa, sb = slice(hg * LANES, (hg + 1) * LANES), slice(half + hg * LANES, half + (hg + 1) * LANES)
        q4 = jnp.concatenate([q_ref[0, :, sa], q_ref[0, :, sb]], axis=1)
        k4 = jnp.concatenate([jnp.concatenate([kp_ref[0, :, sa], kp_ref[0, :, sb]], axis=1),
                              jnp.concatenate([kc_ref[0, :, sa], kc_ref[0, :, sb]], axis=1)], axis=0)
        for pp in range(per_slab // 2):
            pair = hg * (per_slab // 2) + pp
            sl = slice(pair * LANES, (pair + 1) * LANES)
            vcat = jnp.concatenate([vp_ref[0, :, sl], vc_ref[0, :, sl]], axis=0)
            outs = []
            for hh in range(2):
                j = 2 * pp + hh
                keep = ((lane2 & (LANES - 1)) >> 5) == j
                qm = jnp.where(keep, q4, jnp.zeros_like(q4))
                s = lax.dot_general(qm, k4, (((1,), (1,)), ((), ())), preferred_element_type=F32)
                s = jnp.where(valid, s, NEG_INF)
                m = jnp.max(s, axis=-1, keepdims=True)
                pexp = jnp.exp(s - m)
                den = jnp.sum(pexp, axis=-1, keepdims=True)
                o = jnp.dot(pexp.astype(BF16), vcat, preferred_element_type=F32) / den
                outs.append(o)
                lse_tile = jnp.where(lane == (per_slab * hg + j), m + jnp.log(den), lse_tile)
            o_ref[0, :, sl] = jnp.where(first_head, outs[0], outs[1]).astype(BF16)
    lse_ref[0] = lse_tile


def _dilated_attention(q, k, v, g):
    window, dil = DILATIONS[g]
    batch, _, l, d = q.shape
    cur = pl.BlockSpec((1, 1, BAND, d), lambda b, r, i: (b, r, i, 0))
    prev = pl.BlockSpec((1, 1, BAND, d), lambda b, r, i: (b, r, jnp.maximum(i - 1, 0), 0))
    return pl.pallas_call(
        functools.partial(_attn_kernel, steps=window // dil),
        grid=(batch, dil, l // BAND),
        in_specs=[cur, prev, cur, prev, cur],
        out_specs=[cur, pl.BlockSpec((1, 1, BAND, LANES), lambda b, r, i: (b, r, i, 0))],
        out_shape=[
            jax.ShapeDtypeStruct((batch, dil, l, d), BF16),
            jax.ShapeDtypeStruct((batch, dil, l, LANES), F32),
        ],
        compiler_params=_params("parallel", "parallel", "arbitrary"),
        name=f"dilated_attn_g{g}",
    )(q, k, k, v, v)


def _split3(v):
    hi = v.astype(BF16)
    rem = v - hi.astype(F32)
    mid = rem.astype(BF16)
    return hi, mid, (rem - mid.astype(F32)).astype(BF16)


def _dot3(mat_bf, v):
    return sum(jnp.dot(mat_bf, part, preferred_element_type=F32) for part in _split3(v))


def _merge_proj_kernel(o0_ref, o1_ref, o2_ref, l0_ref, l1_ref, l2_ref, p1_ref, p2_ref, x_ref, mod_ref, e_ref,
                       w_ref, out_ref):
    tm, d = x_ref.shape
    perms = (None, p1_ref, p2_ref)
    lses, outs = [], []
    for perm, l_ref, o_ref in zip(perms, (l0_ref, l1_ref, l2_ref), (o0_ref, o1_ref, o2_ref)):
        lse = l_ref[0].reshape(tm, LANES)
        o = o_ref[0].reshape(tm, d)
        if perm is None:
            lses.append(lse)
            outs.append(o.astype(F32))
        else:
            lses.append(_dot3(perm[...], lse))
            outs.append(jnp.dot(perm[...], o, preferred_element_type=F32))
    m = jnp.maximum(jnp.maximum(lses[0], lses[1]), lses[2])
    exps = [jnp.exp(l - m) for l in lses]
    inv = 1.0 / (exps[0] + exps[1] + exps[2])
    acc = None
    for e, o in zip(exps, outs):
        w = e * inv
        w_hi = w.astype(BF16)
        w_lo = (w - w_hi.astype(F32)).astype(BF16)
        wide = (jnp.dot(w_hi, e_ref[...], preferred_element_type=F32)
                + jnp.dot(w_lo, e_ref[...], preferred_element_type=F32))
        acc = wide * o if acc is None else acc + wide * o
    y = jnp.dot(acc.astype(BF16), w_ref[...], preferred_element_type=F32)
    out_ref[...] = x_ref[...] + mod_ref[0, 2:3, :] * y


def _merge_proj(outs, lses, x2, mod, w_out_bf, seq):
    n, d = x2.shape
    tm = TM_PROJ
    per_b = seq // tm
    expand = (jnp.arange(LANES)[:, None] == jnp.arange(d)[None, :] // HEAD_DIM_A).astype(BF16)
    tok = jnp.arange(tm)
    perms = [(((tok % dil) * (tm // dil) + tok // dil)[:, None] == tok[None, :]).astype(BF16)
             for _, dil in DILATIONS[1:]]
    slab = lambda dil, width: pl.BlockSpec((1, dil, tm // dil, width), lambda i: (i // per_b, 0, i % per_b, 0))
    const = lambda shape: pl.BlockSpec(shape, lambda i: (0, 0))
    return pl.pallas_call(
        _merge_proj_kernel,
        grid=(n // tm,),
        in_specs=[slab(dil, d) for _, dil in DILATIONS] + [slab(dil, LANES) for _, dil in DILATIONS] + [
            const((tm, tm)),
            const((tm, tm)),
            pl.BlockSpec((tm, d), lambda i: (i, 0)),
            pl.BlockSpec((1, 6, d), lambda i: (i // per_b, 0, 0)),
            const((LANES, d)),
            const((d, d)),
        ],
        out_specs=pl.BlockSpec((tm, d), lambda i: (i, 0)),
        out_shape=jax.ShapeDtypeStruct((n, d), F32),
        compiler_params=_params("parallel"),
        name="merge_out_proj",
    )(*outs, *lses, *perms, x2, mod, expand, w_out_bf)


def _gla_in_kernel(x_ref, mod_ref, g_ref, w_ref, o_ref):
    h = _norm_mod(x_ref[...], g_ref[...], mod_ref[0, 0:1, :], mod_ref[0, 1:2, :])
    o_ref[...] = jnp.dot(h.astype(BF16), w_ref[...], preferred_element_type=F32).astype(BF16)


def _gla_in_proj(x2, mod, g1, w_bf, seq):
    n, d = x2.shape
    ncol = w_bf.shape[1]
    tm = TM_PROJ
    per_b = seq // tm
    return pl.pallas_call(
        _gla_in_kernel,
        grid=(n // tm,),
        in_specs=[
            pl.BlockSpec((tm, d), lambda i: (i, 0)),
            pl.BlockSpec((1, 6, d), lambda i: (i // per_b, 0, 0)),
            pl.BlockSpec((1, d), lambda i: (0, 0)),
            pl.BlockSpec((d, ncol), lambda i: (0, 0)),
        ],
        out_specs=pl.BlockSpec((tm, ncol), lambda i: (i, 0)),
        out_shape=jax.ShapeDtypeStruct((n, ncol), BF16),
        compiler_params=_params("parallel"),
        name="gla_in_proj",
    )(x2, mod, g1, w_bf)


def _gla_kernel(q_ref, k_ref, v_ref, r_ref, a_ref, wg_ref, gb_ref, og_ref, tri_ref, x_ref, mod_ref, wo_ref,
                out_ref, st_ref, o_scr, la_scr):
    t = pl.program_id(1)
    c = GLA_CHUNK

    @pl.when(t == 0)
    def _():
        st_ref[...] = jnp.zeros_like(st_ref)

    g = jnp.dot(a_ref[0], wg_ref[...], preferred_element_type=F32) + gb_ref[...]
    la_scr[...] = (jnp.minimum(g, 0.0) - jnp.log(1.0 + jnp.exp(-jnp.abs(g)))) * (1.0 / GATE_TAU)

    rr = lax.broadcasted_iota(jnp.int32, (c, c), 0)
    cc = lax.broadcasted_iota(jnp.int32, (c, c), 1)
    causal = cc <= rr

    def chunk(ci, carry):
        rows = pl.ds(pl.multiple_of(ci * c, c), c)
        la = la_scr[rows, :]
        b = _dot3(tri_ref[...], la)
        b_last = b[c - 1:c, :]
        q = q_ref[0, rows, :].astype(F32) * (KEY_DIM_B ** -0.5)
        k = k_ref[0, rows, :].astype(F32)
        q_dec = (q * jnp.exp(b)).astype(BF16)
        k_inv = (k * jnp.exp(-b)).astype(BF16)
        k_dec = (k * jnp.exp(b_last - b)).astype(BF16)
        decay = jnp.exp(b_last)
        v = v_ref[0, rows, :]
        for h in range(HEADS_B):
            ks = slice(h * KEY_DIM_B, (h + 1) * KEY_DIM_B)
            vs = slice(h * VAL_DIM_B, (h + 1) * VAL_DIM_B)
            qd, ki, kd, vh = q_dec[:, ks], k_inv[:, ks], k_dec[:, ks], v[:, vs]
            att = lax.dot_general(qd, ki, (((1,), (1,)), ((), ())), preferred_element_type=F32)
            att = jnp.where(causal, att, 0.0)
            st = st_ref[h]
            o = (jnp.dot(att.astype(BF16), vh, preferred_element_type=F32)
                 + lax.dot_general(qd, st.astype(BF16), (((1,), (1,)), ((), ())), preferred_element_type=F32))
            o_scr[rows, vs] = o
            st_ref[h] = st * decay[:, ks] + lax.dot_general(vh, kd, (((0,), (0,)), ((), ())),
                                                             preferred_element_type=F32)
        return carry

    lax.fori_loop(0, q_ref.shape[1] // c, chunk, 0)

    r = r_ref[0].astype(F32)
    parts = []
    for h in range(HEADS_B):
        vs = slice(h * VAL_DIM_B, (h + 1) * VAL_DIM_B)
        oh = o_scr[:, vs]
        ms = jnp.mean(oh * oh, axis=-1, keepdims=True)
        rh = r[:, vs]
        parts.append((oh * lax.rsqrt(ms + RMS_EPS) * og_ref[...] * (rh * jax.nn.sigmoid(rh))).astype(BF16))
    y = jnp.dot(jnp.concatenate(parts, axis=1), wo_ref[...], preferred_element_type=F32)
    out_ref[0] = x_ref[0] + mod_ref[0, 2:3, :] * y


def _gla(proj, x3, mod, wg_bf, gate_bias, out_gain, w_out_bf):
    batch, seq, d = x3.shape
    ncol = proj.shape[1]
    t = T_GLA
    hk = HEADS_B * KEY_DIM_B
    hv = HEADS_B * VAL_DIM_B
    p3 = proj.reshape(batch, seq, ncol)
    tri = (jnp.arange(GLA_CHUNK)[None, :] <= jnp.arange(GLA_CHUNK)[:, None]).astype(BF16)
    a_blk = (2 * hk + 2 * hv) // LANES
    const = lambda shape: pl.BlockSpec(shape, lambda b, i: (0,) * len(shape))
    return pl.pallas_call(
        _gla_kernel,
        grid=(batch, seq // t),
        in_specs=[
            pl.BlockSpec((1, t, hk), lambda b, i: (b, i, 0)),
            pl.BlockSpec((1, t, hk), lambda b, i: (b, i, 1)),
            pl.BlockSpec((1, t, hv), lambda b, i: (b, i, (2 * hk) // hv)),
            pl.BlockSpec((1, t, hv), lambda b, i: (b, i, (2 * hk + hv) // hv)),
            pl.BlockSpec((1, t, LANES), lambda b, i: (b, i, a_blk)),
            const((LANES, hk)),
            const((1, hk)),
            const((1, VAL_DIM_B)),
            const((GLA_CHUNK, GLA_CHUNK)),
            pl.BlockSpec((1, t, d), lambda b, i: (b, i, 0)),
            pl.BlockSpec((1, 6, d), lambda b, i: (b, 0, 0)),
            const((hv, d)),
        ],
        out_specs=pl.BlockSpec((1, t, d), lambda b, i: (b, i, 0)),
        out_shape=jax.ShapeDtypeStruct((batch, seq, d), F32),
        scratch_shapes=[
            pltpu.VMEM((HEADS_B, VAL_DIM_B, KEY_DIM_B), F32),
            pltpu.VMEM((t, hv), F32),
            pltpu.VMEM((t, hk), F32),
        ],
        compiler_params=_params("parallel", "arbitrary"),
        name="gla",
    )(p3, p3, p3, p3, p3, wg_bf, gate_bias, out_gain, tri, x3, mod, w_out_bf)


def _router_kernel(x_ref, mod_ref, g_ref, rw_ref, rb_ref, tri_ref, h_ref, info_ref, cnt_ref, carry_ref):
    i = pl.program_id(0)

    @pl.when(i == 0)
    def _():
        carry_ref[...] = jnp.zeros_like(carry_ref)

    h = _norm_mod(x_ref[...], g_ref[...], mod_ref[0, 3:4, :], mod_ref[0, 4:5, :])
    h_ref[...] = h
    tm = h.shape[0]
    lane = lax.broadcasted_iota(jnp.int32, (tm, LANES), 1)
    lane_f = lane.astype(F32)
    logits = jnp.dot(h, rw_ref[...], precision=HIGHEST, preferred_element_type=F32) + rb_ref[...]
    work = jnp.where(lane < N_EXPERTS, logits, -jnp.inf)
    picks, vals, idxs = [], [], []
    for _ in range(TOP_K):
        m = jnp.max(work, axis=-1, keepdims=True)
        idx = jnp.min(jnp.where(work == m, lane_f, float(LANES)), axis=-1, keepdims=True)
        pick = lane_f == idx
        work = jnp.where(pick, -jnp.inf, work)
        picks.append(pick)
        vals.append(m)
        idxs.append(idx)
    exps = [jnp.exp(v - vals[0]) for v in vals]
    inv = 1.0 / (exps[0] + exps[1] + exps[2] + exps[3])
    chosen = jnp.zeros((tm, LANES), F32)
    for pick in picks:
        chosen = jnp.where(pick, 1.0, chosen)
    before = jnp.dot(tri_ref[...], chosen.astype(BF16), preferred_element_type=F32) + carry_ref[0:1, :]
    info = jnp.zeros((tm, LANES), F32)
    for kk in range(TOP_K):
        rank = jnp.sum(jnp.where(picks[kk], before, 0.0), axis=-1, keepdims=True)
        info = jnp.where(lane == kk, idxs[kk], info)
        info = jnp.where(lane == TOP_K + kk, rank, info)
        info = jnp.where(lane == 2 * TOP_K + kk, exps[kk] * inv, info)
    info_ref[...] = info
    total = carry_ref[0:1, :] + jnp.sum(chosen, axis=0, keepdims=True)
    carry_ref[...] = jnp.broadcast_to(total, carry_ref.shape)
    cnt_ref[...] = jnp.broadcast_to(total, cnt_ref.shape)


def _router(x2, mod, g2, router_w, router_b, seq):
    n, d = x2.shape
    tm = TM_ROUTE
    per_b = seq // tm
    rw = jnp.zeros((d, LANES), F32).at[:, :N_EXPERTS].set(router_w)
    rb = jnp.zeros((1, LANES), F32).at[0, :N_EXPERTS].set(router_b)
    tri = (jnp.arange(tm)[None, :] < jnp.arange(tm)[:, None]).astype(BF16)
    return pl.pallas_call(
        _router_kernel,
        grid=(n // tm,),
        in_specs=[
            pl.BlockSpec((tm, d), lambda i: (i, 0)),
            pl.BlockSpec((1, 6, d), lambda i: (i // per_b, 0, 0)),
            pl.BlockSpec((1, d), lambda i: (0, 0)),
            pl.BlockSpec((d, LANES), lambda i: (0, 0)),
            pl.BlockSpec((1, LANES), lambda i: (0, 0)),
            pl.BlockSpec((tm, tm), lambda i: (0, 0)),
        ],
        out_specs=[
            pl.BlockSpec((tm, d), lambda i: (i, 0)),
            pl.BlockSpec((tm, LANES), lambda i: (i, 0)),
            pl.BlockSpec((8, LANES), lambda i: (0, 0)),
        ],
        out_shape=[
            jax.ShapeDtypeStruct((n, d), F32),
            jax.ShapeDtypeStruct((n, LANES), F32),
            jax.ShapeDtypeStruct((8, LANES), F32),
        ],
        scratch_shapes=[pltpu.VMEM((8, LANES), F32)],
        compiler_params=_params("arbitrary"),
        name="moe_router",
    )(x2, mod, g2, rw, rb, tri)


def _row_copy(src, src_row, dst, dst_row, sem):
    return pltpu.make_async_copy(src.at[pl.ds(src_row, 1)], dst.at[pl.ds(dst_row, 1)], sem)


def _dispatch_kernel(seg_end_ref, padded_ref, dest_ref, h_ref, xs_hbm, zero_ref, sem, zero_sem):
    tm = h_ref.shape[0]

    @pl.when(pl.program_id(0) == 0)
    def _():
        zero_ref[...] = jnp.zeros_like(zero_ref)

        def tail(e):
            return pltpu.make_async_copy(
                zero_ref, xs_hbm.at[pl.ds(pl.multiple_of(seg_end_ref[e] - TM_EXPERT, TM_EXPERT), TM_EXPERT)], zero_sem)

        def start(e, carry):
            pl.when(padded_ref[e] > 0)(lambda: tail(e).start())
            return carry

        def wait(e, carry):
            pl.when(padded_ref[e] > 0)(lambda: tail(e).wait())
            return carry

        lax.fori_loop(0, N_EXPERTS, start, 0)
        lax.fori_loop(0, N_EXPERTS, wait, 0)

        def spare(blk):
            return pltpu.make_async_copy(
                zero_ref, xs_hbm.at[pl.ds(pl.multiple_of(blk * TM_EXPERT, TM_EXPERT), TM_EXPERT)], zero_sem)

        used = seg_end_ref[N_EXPERTS - 1] // TM_EXPERT
        total = xs_hbm.shape[0] // TM_EXPERT
        lax.fori_loop(used, total, lambda blk, carry: (spare(blk).start(), carry)[1], 0)
        lax.fori_loop(used, total, lambda blk, carry: (spare(blk).wait(), carry)[1], 0)

    def start(t, carry):
        for kk in range(TOP_K):
            _row_copy(h_ref, t, xs_hbm, dest_ref[0, 0, TOP_K * t + kk], sem).start(priority=kk % 2)
        return carry

    lax.fori_loop(0, tm, start, 0)

    def wait(t, carry):
        for kk in range(TOP_K):
            _row_copy(h_ref, 0, xs_hbm, 0, sem).wait()
        return carry

    lax.fori_loop(0, tm, wait, 0)


def _dispatch(h, dest, seg_end, padded, cap):
    n, d = h.shape
    tm = TM_MOVE
    dest3 = dest.reshape(n // tm, 1, tm * TOP_K)
    return pl.pallas_call(
        _dispatch_kernel,
        grid_spec=pltpu.PrefetchScalarGridSpec(
            num_scalar_prefetch=2,
            grid=(n // tm,),
            in_specs=[
                pl.BlockSpec((1, 1, tm * TOP_K), lambda i, se, pd: (i, 0, 0), memory_space=pltpu.SMEM),
                pl.BlockSpec((tm, d), lambda i, se, pd: (i, 0)),
            ],
            out_specs=pl.BlockSpec(memory_space=pl.ANY),
            scratch_shapes=[pltpu.VMEM((TM_EXPERT, d), F32), pltpu.SemaphoreType.DMA(()),
                            pltpu.SemaphoreType.DMA(())],
        ),
        out_shape=jax.ShapeDtypeStruct((cap, d), F32),
        compiler_params=_params("arbitrary"),
        name="moe_dispatch",
    )(seg_end, padded, dest3, h)


def _expert_kernel(be_ref, nxt_ref, na_ref, x_ref, wgu_hbm, bgu_ref, wd_hbm, bd_ref, o_ref,
                   gu_stage, d_stage, gu_bf, d_bf, sems):
    i = pl.program_id(0)
    live = i < na_ref[0]
    e = be_ref[i]

    def fetch(expert):
        return (pltpu.make_async_copy(wgu_hbm.at[expert], gu_stage, sems.at[0]),
                pltpu.make_async_copy(wd_hbm.at[expert], d_stage, sems.at[1]))

    @pl.when(i == 0)
    def _():
        for c in fetch(e):
            c.start()

    @pl.when(live & ((i == 0) | (e != be_ref[jnp.maximum(i - 1, 0)])))
    def _():
        for c in fetch(e):
            c.wait()
        gu_bf[...] = gu_stage[...].astype(BF16)
        d_bf[...] = d_stage[...].astype(BF16)

        @pl.when(nxt_ref[i] >= 0)
        def _():
            for c in fetch(nxt_ref[i]):
                c.start()

    @pl.when(jnp.logical_not(live))
    def _():
        o_ref[...] = jnp.zeros_like(o_ref)

    @pl.when(live)
    def _():
        f = d_bf.shape[0]
        gu = jnp.dot(x_ref[...].astype(BF16), gu_bf[...], preferred_element_type=F32) + bgu_ref[0]
        gate = jnp.minimum(gu[:, :f], SWIGLU_LIMIT)
        up = jnp.clip(gu[:, f:], -SWIGLU_LIMIT, SWIGLU_LIMIT)
        act = (up + 1.0) * (gate * jax.nn.sigmoid(SWIGLU_ALPHA * gate))
        o_ref[...] = jnp.dot(act.astype(BF16), d_bf[...], preferred_element_type=F32) + bd_ref[0]


def _experts(xs, block_e, next_e, n_active, w_gu, b_gu, w_down, b_down):
    cap, d = xs.shape
    e, _, f2 = w_gu.shape
    f = f2 // 2
    tm = TM_EXPERT
    nb = cap // tm
    return pl.pallas_call(
        _expert_kernel,
        grid_spec=pltpu.PrefetchScalarGridSpec(
            num_scalar_prefetch=3,
            grid=(nb,),
            in_specs=[
                pl.BlockSpec((tm, d), lambda i, be, nx, na: (jnp.minimum(i, na[0] - 1), 0)),
                pl.BlockSpec(memory_space=pl.ANY),
                pl.BlockSpec((1, 1, f2), lambda i, be, nx, na: (be[i], 0, 0)),
                pl.BlockSpec(memory_space=pl.ANY),
                pl.BlockSpec((1, 1, d), lambda i, be, nx, na: (be[i], 0, 0)),
            ],
            out_specs=pl.BlockSpec((tm, d), lambda i, be, nx, na: (i, 0)),
            scratch_shapes=[pltpu.VMEM((d, f2), F32), pltpu.VMEM((f, d), F32), pltpu.VMEM((d, f2), BF16),
                            pltpu.VMEM((f, d), BF16), pltpu.SemaphoreType.DMA((2,))],
        ),
        out_shape=jax.ShapeDtypeStruct((cap, d), F32),
        compiler_params=_params("arbitrary"),
        name="moe_experts",
    )(block_e, next_e, n_active, xs, w_gu, b_gu.reshape(e, 1, f2), w_down, b_down.reshape(e, 1, d))


def _combine_kernel(dest_ref, info_ref, x_ref, mod_ref, ys_hbm, out_ref, buf, sem):
    tm = x_ref.shape[0]

    def start(t, carry):
        for kk in range(TOP_K):
            _row_copy(ys_hbm, dest_ref[0, 0, TOP_K * t + kk], buf.at[kk], t, sem).start(priority=kk % 2)
        return carry

    lax.fori_loop(0, tm, start, 0)

    def wait(t, carry):
        for kk in range(TOP_K):
            _row_copy(ys_hbm, 0, buf.at[kk], 0, sem).wait()
        return carry

    lax.fori_loop(0, tm, wait, 0)

    info = info_ref[...]
    y = None
    for kk in range(TOP_K):
        term = info[:, 2 * TOP_K + kk:2 * TOP_K + kk + 1] * buf[kk]
        y = term if y is None else y + term
    out_ref[...] = x_ref[...] + mod_ref[0, 5:6, :] * y


def _combine(ys, dest, info, x2, mod, seq):
    n, d = x2.shape
    tm = TM_MOVE
    per_b = seq // tm
    dest3 = dest.reshape(n // tm, 1, tm * TOP_K)
    return pl.pallas_call(
        _combine_kernel,
        grid=(n // tm,),
        in_specs=[
            pl.BlockSpec((1, 1, tm * TOP_K), lambda i: (i, 0, 0), memory_space=pltpu.SMEM),
            pl.BlockSpec((tm, LANES), lambda i: (i, 0)),
            pl.BlockSpec((tm, d), lambda i: (i, 0)),
            pl.BlockSpec((1, 6, d), lambda i: (i // per_b, 0, 0)),
            pl.BlockSpec(memory_space=pl.ANY),
        ],
        out_specs=pl.BlockSpec((tm, d), lambda i: (i, 0)),
        out_shape=jax.ShapeDtypeStruct((n, d), F32),
        scratch_shapes=[pltpu.VMEM((TOP_K, tm, d), F32), pltpu.SemaphoreType.DMA(())],
        compiler_params=_params("arbitrary"),
        name="moe_combine",
    )(dest3, info, x2, mod, ys)


def _moe(x2, mod, g2, router_w, router_b, w_gu, b_gu, w_down, b_down, seq):
    n, d = x2.shape
    h, info, cnt = _router(x2, mod, g2, router_w, router_b, seq)
    counts = cnt[0, :N_EXPERTS].astype(jnp.int32)
    padded = (counts + TM_EXPERT - 1) // TM_EXPERT * TM_EXPERT
    seg_end = jnp.cumsum(padded)
    seg_start = seg_end - padded
    cap = n * TOP_K + N_EXPERTS * TM_EXPERT
    nb = cap // TM_EXPERT
    block_start = jnp.arange(nb, dtype=jnp.int32) * TM_EXPERT
    block_e = jnp.minimum(jnp.sum(seg_end[None, :] <= block_start[:, None], axis=1), N_EXPERTS - 1).astype(jnp.int32)
    n_active = (seg_end[-1:] // TM_EXPERT).astype(jnp.int32)
    expert = info[:, :TOP_K].astype(jnp.int32)
    rank = info[:, TOP_K:2 * TOP_K].astype(jnp.int32)
    dest = (seg_start[expert] + rank).reshape(-1)
    ids = jnp.arange(N_EXPERTS, dtype=jnp.int32)
    later = (ids[None, :] > ids[:, None]) & (padded[None, :] > 0)
    next_of = jnp.min(jnp.where(later, ids[None, :], N_EXPERTS), axis=1)
    next_e = jnp.where(next_of < N_EXPERTS, next_of, -1).astype(jnp.int32)[block_e]
    xs = _dispatch(h, dest, seg_end.astype(jnp.int32), padded.astype(jnp.int32), cap)
    ys = _experts(xs, block_e, next_e, n_active, w_gu, b_gu, w_down, b_down)
    return _combine(ys, dest, info, x2, mod, seq)


def kernel(x, c, positions, ada_w, ada_b, norm1_g, norm2_g, a_w_in, a_q_gain, a_k_gain, a_w_out, b_w_in,
           b_w_gate_up, b_gate_bias, b_out_gain, b_w_out, router_w, router_b, moe_w_gu, moe_b_gu, moe_w_down,
           moe_b_down):
    batch, seq, d = x.shape
    depth = ada_w.shape[0]
    n = batch * seq
    mods = _adaln(c, ada_w, ada_b).reshape(depth, batch, 6, d)
    x2 = x.reshape(n, d)
    for layer in range(depth):
        mod = mods[layer]
        j = layer // 2
        g1 = norm1_g[layer].reshape(1, d)
        g2 = norm2_g[layer].reshape(1, d)
        if layer % 2 == 0:
            qkv = _qkv_proj(x2, mod, g1, a_w_in[j], a_q_gain[j], a_k_gain[j], positions, batch, seq)
            outs, lses = zip(*[_dilated_attention(*qkv[g], g) for g in range(N_GROUPS)])
            x2 = _merge_proj(outs, lses, x2, mod, a_w_out[j].astype(BF16), seq)
        else:
            hk = HEADS_B * KEY_DIM_B
            hv = HEADS_B * VAL_DIM_B
            width = 2 * hk + 2 * hv
            w_in = jnp.zeros((d, width + LANES), BF16).at[:, :width + GATE_RANK].set(b_w_in[j].astype(BF16))
            wg = jnp.zeros((LANES, hk), BF16).at[:GATE_RANK].set(b_w_gate_up[j].astype(BF16))
            proj = _gla_in_proj(x2, mod, g1, w_in, seq)
            x3 = _gla(proj, x2.reshape(batch, seq, d), mod, wg, b_gate_bias[j].reshape(1, hk),
                      b_out_gain[j].reshape(1, VAL_DIM_B), b_w_out[j].astype(BF16))
            x2 = x3.reshape(n, d)
        x2 = _moe(x2, mod, g2, router_w[layer], router_b[layer], moe_w_gu[layer], moe_b_gu[layer],
                  moe_w_down[layer], moe_b_down[layer], seq)
    return x2.reshape(batch, seq, d)
```

```python
import functools

import jax
import jax.numpy as jnp
from jax import lax
from jax.experimental import pallas as pl
from jax.experimental.pallas import tpu as pltpu

F32 = jnp.float32
BF16 = jnp.bfloat16
HIGHEST = lax.Precision.HIGHEST

RMS_EPS = 1e-6
DILATIONS = ((128, 1), (512, 4), (2048, 16))
N_GROUPS = 3
HEADS_A = 16
HEAD_DIM_A = 64
BAND = 128
ROPE_THETA = 10000.0
NEG_INF = -1e30
HEADS_B = 4
KEY_DIM_B = 128
VAL_DIM_B = 256
GATE_RANK = 16
GATE_TAU = 16.0
GLA_CHUNK = 64
N_EXPERTS = 32
TOP_K = 4
SWIGLU_LIMIT = 7.0
SWIGLU_ALPHA = 1.702

LANES = 128
VMEM_LIMIT = 56 * 1024 * 1024

TM_PROJ = 512
TM_QKV = 256
TM_ROUTE = 512
TM_EXPERT = 512
TM_MOVE = 256
T_GLA = 512


def _params(*sem):
    return pltpu.CompilerParams(dimension_semantics=sem, vmem_limit_bytes=VMEM_LIMIT)


def _norm_mod(x, gain, shift, scale):
    ms = jnp.mean(x * x, axis=-1, keepdims=True)
    y = x * lax.rsqrt(ms + RMS_EPS) * gain
    return y * (1.0 + scale) + shift


def _adaln_kernel(c_ref, w_ref, b_ref, o_ref):
    c = c_ref[...]
    cond = c * jax.nn.sigmoid(c)
    o_ref[0] = jnp.dot(cond, w_ref[0], precision=HIGHEST, preferred_element_type=F32) + b_ref[0]


def _adaln(c, ada_w, ada_b):
    depth, d, n6 = ada_w.shape
    b = c.shape[0]
    tn = 1536
    return pl.pallas_call(
        _adaln_kernel,
        grid=(depth, n6 // tn),
        in_specs=[
            pl.BlockSpec((b, d), lambda l, j: (0, 0)),
            pl.BlockSpec((1, d, tn), lambda l, j: (l, 0, j)),
            pl.BlockSpec((1, 1, tn), lambda l, j: (l, 0, j)),
        ],
        out_specs=pl.BlockSpec((1, b, tn), lambda l, j: (l, 0, j)),
        out_shape=jax.ShapeDtypeStruct((depth, b, n6), F32),
        compiler_params=_params("parallel", "parallel"),
        name="adaln",
    )(c, ada_w, ada_b.reshape(depth, 1, n6))


def _rope_kernel(pos_ref, cos_ref, sin_ref):
    pos = pos_ref[...].astype(F32)
    lane = lax.broadcasted_iota(jnp.int32, (1, LANES), 1)
    half = HEAD_DIM_A // 2
    fidx = (lane & (half - 1)).astype(F32)
    inv_freq = jnp.power(jnp.full((1, LANES), ROPE_THETA, F32), -(fidx / half))
    ang = pos * inv_freq
    cos_ref[...] = jnp.cos(ang)
    s = jnp.sin(ang)
    sin_ref[...] = jnp.where(lane < LANES // 2, -s, s)


def _rope_tables(positions):
    n = positions.size
    tm = 1024
    return pl.pallas_call(
        _rope_kernel,
        grid=(n // tm,),
        in_specs=[pl.BlockSpec((tm, 1), lambda i: (i, 0))],
        out_specs=[pl.BlockSpec((tm, LANES), lambda i: (i, 0))] * 2,
        out_shape=[jax.ShapeDtypeStruct((n, LANES), F32)] * 2,
        compiler_params=_params("parallel"),
        name="rope_tables",
    )(positions.reshape(n, 1))


def _stream_order(tm, dil):
    s = jnp.arange(tm)
    return (s % (tm // dil)) * dil + s // (tm // dil)


def _qkv_kernel(x_ref, mod_ref, g_ref, w_hbm, gain_ref, cos_ref, sin_ref, bd_ref, p1_ref, p2_ref, *rest):
    o_refs, (w_ref, w_sem) = rest[:3 * N_GROUPS], rest[3 * N_GROUPS:]
    tm, d = x_ref.shape

    @pl.when(pl.program_id(0) == 0)
    def _():
        copy = pltpu.make_async_copy(w_hbm, w_ref, w_sem)
        copy.start()
        copy.wait()

    h = _norm_mod(x_ref[...], g_ref[...], mod_ref[0, 0:1, :], mod_ref[0, 1:2, :]).astype(BF16)
    hs = [h] + [jnp.dot(p[...], h, preferred_element_type=F32).astype(BF16) for p in (p1_ref, p2_ref)]
    for jj in range(3 * N_GROUPS):
        g, comp = divmod(jj, 3)
        dil = DILATIONS[g][1]
        rows = tm // dil
        out = o_refs[jj]
        acc = jnp.dot(hs[g], w_ref[:, jj * d:(jj + 1) * d], preferred_element_type=F32)
        if comp == 2:
            val = acc.astype(BF16)
            for r in range(dil):
                out[0, r] = val[r * rows:(r + 1) * rows]
            continue
        gain = gain_ref[jj]
        cos = cos_ref[g]
        sin = sin_ref[g]
        for cc in range(d // 256):
            a = acc[:, cc * 256:(cc + 1) * 256]
            ss = jnp.dot((a * a).astype(BF16), bd_ref[...], preferred_element_type=F32)
            qn = a * lax.rsqrt(ss * (1.0 / HEAD_DIM_A) + RMS_EPS) * gain[:, cc * 256:(cc + 1) * 256]
            for hh in range(2):
                xx = qn[:, hh * LANES:(hh + 1) * LANES]
                val = (xx * cos + pltpu.roll(xx, 64, 1) * sin).astype(BF16)
                c1 = cc * 256 + hh * LANES
                for r in range(dil):
                    out[0, r, :, c1:c1 + LANES] = val[r * rows:(r + 1) * rows]


def _split_half_columns(w):
    lead = w.shape[:-1]
    w = w.reshape(*lead, HEADS_A // 2, 2, 2, HEAD_DIM_A // 2)
    return jnp.swapaxes(w, -3, -2).reshape(*lead, HEADS_A * HEAD_DIM_A)


def _qkv_proj(x2, mod, g1, w_in, q_gain, k_gain, positions, batch, seq):
    n, d = x2.shape
    tm = TM_QKV
    per_b = seq // tm
    reps = d // HEAD_DIM_A
    w = w_in.reshape(d, N_GROUPS, 3, d)
    w = jnp.concatenate([_split_half_columns(w[:, :, :2]), w[:, :, 2:]], axis=2).reshape(d, 3 * N_GROUPS * d)
    gains = []
    for g in range(N_GROUPS):
        gains += [_split_half_columns(jnp.tile(q_gain[g], reps)) * (HEAD_DIM_A ** -0.5),
                  _split_half_columns(jnp.tile(k_gain[g], reps)), jnp.ones((d,), F32)]
    gains = jnp.stack(gains).reshape(3 * N_GROUPS, 1, d)
    orders = [_stream_order(tm, dil) for _, dil in DILATIONS]
    pos_tiles = positions.reshape(n // tm, tm)
    cos_t, sin_t = _rope_tables(jnp.stack([pos_tiles[:, o] for o in orders]))
    cos_t, sin_t = cos_t.reshape(N_GROUPS, n, LANES), sin_t.reshape(N_GROUPS, n, LANES)
    perms = [(o[:, None] == jnp.arange(tm)[None, :]).astype(BF16) for o in orders[1:]]
    lane = jnp.arange(256)
    head_of = (lane // LANES) * 2 + (lane // 32) % 2
    bd = (head_of[:, None] == head_of[None, :]).astype(BF16)
    dils = [DILATIONS[jj // 3][1] for jj in range(3 * N_GROUPS)]
    const = lambda shape: pl.BlockSpec(shape, lambda i: (0,) * len(shape))
    flat = pl.pallas_call(
        _qkv_kernel,
        grid=(n // tm,),
        in_specs=[
            pl.BlockSpec((tm, d), lambda i: (i, 0)),
            pl.BlockSpec((1, 6, d), lambda i: (i // per_b, 0, 0)),
            const((1, d)),
            pl.BlockSpec(memory_space=pl.ANY),
            const((3 * N_GROUPS, 1, d)),
            pl.BlockSpec((N_GROUPS, tm, LANES), lambda i: (0, i, 0)),
            pl.BlockSpec((N_GROUPS, tm, LANES), lambda i: (0, i, 0)),
            const((256, 256)),
            const((tm, tm)),
            const((tm, tm)),
        ],
        out_specs=[pl.BlockSpec((1, dil, tm // dil, d), lambda i: (i // per_b, 0, i % per_b, 0)) for dil in dils],
        out_shape=[jax.ShapeDtypeStruct((batch, dil, seq // dil, d), BF16) for dil in dils],
        scratch_shapes=[pltpu.VMEM((d, 3 * N_GROUPS * d), BF16), pltpu.SemaphoreType.DMA(())],
        compiler_params=_params("arbitrary"),
        name="qkv_proj",
    )(x2, mod, g1, w.astype(BF16), gains, cos_t, sin_t, bd, *perms)
    return [flat[3 * g:3 * g + 3] for g in range(N_GROUPS)]


def _attn_kernel(q_ref, kp_ref, kc_ref, vp_ref, vc_ref, o_ref, lse_ref, *, steps):
    q_ref, kp_ref, kc_ref, vp_ref, vc_ref, o_ref, lse_ref = (
        r.at[0] for r in (q_ref, kp_ref, kc_ref, vp_ref, vc_ref, o_ref, lse_ref))
    n = pl.program_id(2)
    row = lax.broadcasted_iota(jnp.int32, (BAND, 2 * BAND), 0)
    col = lax.broadcasted_iota(jnp.int32, (BAND, 2 * BAND), 1)
    delta = row + BAND - col
    valid = (delta >= 0) & (delta <= steps) & ((col >= BAND) | (n > 0))
    lane = lax.broadcasted_iota(jnp.int32, (BAND, LANES), 1)
    first_head = lane < HEAD_DIM_A
    even_head = (lane & (HEAD_DIM_A // 2)) == 0
    lse_tile = jnp.zeros((BAND, LANES), F32)
    for p in range(HEADS_A // 2):
        sl = slice(p * LANES, (p + 1) * LANES)
        qp = q_ref[0, :, sl]
        kcat = jnp.concatenate([kp_ref[0, :, sl], kc_ref[0, :, sl]], axis=0)
        vcat = jnp.concatenate([vp_ref[0, :, sl], vc_ref[0, :, sl]], axis=0)
        outs = []
        for hh in range(2):
            keep = even_head if hh == 0 else jnp.logical_not(even_head)
            qm = jnp.where(keep, qp, jnp.zeros_like(qp))
            s = lax.dot_general(qm, kcat, (((1,), (1,)), ((), ())), preferred_element_type=F32)
            s = jnp.where(valid, s, NEG_INF)
            m = jnp.max(s, axis=-1, keepdims=True)
            pexp = jnp.exp(s - m)
            den = jnp.sum(pexp, axis=-1, keepdims=True)
            o = jnp.dot(pexp.astype(BF16), vcat, preferred_element_type=F32) / den
            outs.append(o)
            lse_tile = jnp.where(lane == (2 * p + hh), m + jnp.log(den), lse_tile)
        o_ref[0, :, sl] = jnp.where(first_head, outs[0], outs[1]).astype(BF16)
    lse_ref[0] = lse_tile


def _dilated_attention(q, k, v, g):
    window, dil = DILATIONS[g]
    batch, _, l, d = q.shape
    cur = pl.BlockSpec((1, 1, BAND, d), lambda b, r, i: (b, r, i, 0))
    prev = pl.BlockSpec((1, 1, BAND, d), lambda b, r, i: (b, r, jnp.maximum(i - 1, 0), 0))
    return pl.pallas_call(
        functools.partial(_attn_kernel, steps=window // dil),
        grid=(batch, dil, l // BAND),
        in_specs=[cur, prev, cur, prev, cur],
        out_specs=[cur, pl.BlockSpec((1, 1, BAND, LANES), lambda b, r, i: (b, r, i, 0))],
        out_shape=[
            jax.ShapeDtypeStruct((batch, dil, l, d), BF16),
            jax.ShapeDtypeStruct((batch, dil, l, LANES), F32),
        ],
        compiler_params=_params("parallel", "parallel", "arbitrary"),
        name=f"dilated_attn_g{g}",
    )(q, k, k, v, v)


def _split3(v):
    hi = v.astype(BF16)
    rem = v - hi.astype(F32)
    mid = rem.astype(BF16)
    return hi, mid, (rem - mid.astype(F32)).astype(BF16)


def _dot3(mat_bf, v):
    return sum(jnp.dot(mat_bf, part, preferred_element_type=F32) for part in _split3(v))


def _merge_proj_kernel(o0_ref, o1_ref, o2_ref, l0_ref, l1_ref, l2_ref, p1_ref, p2_ref, x_ref, mod_ref, e_ref,
                       w_ref, out_ref):
    tm, d = x_ref.shape
    perms = (None, p1_ref, p2_ref)
    lses, outs = [], []
    for perm, l_ref, o_ref in zip(perms, (l0_ref, l1_ref, l2_ref), (o0_ref, o1_ref, o2_ref)):
        lse = l_ref[0].reshape(tm, LANES)
        o = o_ref[0].reshape(tm, d)
        if perm is None:
            lses.append(lse)
            outs.append(o.astype(F32))
        else:
            lses.append(_dot3(perm[...], lse))
            outs.append(jnp.dot(perm[...], o, preferred_element_type=F32))
    m = jnp.maximum(jnp.maximum(lses[0], lses[1]), lses[2])
    exps = [jnp.exp(l - m) for l in lses]
    inv = 1.0 / (exps[0] + exps[1] + exps[2])
    acc = None
    for e, o in zip(exps, outs):
        w = e * inv
        w_hi = w.astype(BF16)
        w_lo = (w - w_hi.astype(F32)).astype(BF16)
        wide = (jnp.dot(w_hi, e_ref[...], preferred_element_type=F32)
                + jnp.dot(w_lo, e_ref[...], preferred_element_type=F32))
        acc = wide * o if acc is None else acc + wide * o
    y = jnp.dot(acc.astype(BF16), w_ref[...], preferred_element_type=F32)
    out_ref[...] = x_ref[...] + mod_ref[0, 2:3, :] * y


def _merge_proj(outs, lses, x2, mod, w_out_bf, seq):
    n, d = x2.shape
    tm = TM_PROJ
    per_b = seq // tm
    expand = (jnp.arange(LANES)[:, None] == jnp.arange(d)[None, :] // HEAD_DIM_A).astype(BF16)
    tok = jnp.arange(tm)
    perms = [(((tok % dil) * (tm // dil) + tok // dil)[:, None] == tok[None, :]).astype(BF16)
             for _, dil in DILATIONS[1:]]
    slab = lambda dil, width: pl.BlockSpec((1, dil, tm // dil, width), lambda i: (i // per_b, 0, i % per_b, 0))
    const = lambda shape: pl.BlockSpec(shape, lambda i: (0, 0))
    return pl.pallas_call(
        _merge_proj_kernel,
        grid=(n // tm,),
        in_specs=[slab(dil, d) for _, dil in DILATIONS] + [slab(dil, LANES) for _, dil in DILATIONS] + [
            const((tm, tm)),
            const((tm, tm)),
            pl.BlockSpec((tm, d), lambda i: (i, 0)),
            pl.BlockSpec((1, 6, d), lambda i: (i // per_b, 0, 0)),
            const((LANES, d)),
            const((d, d)),
        ],
        out_specs=pl.BlockSpec((tm, d), lambda i: (i, 0)),
        out_shape=jax.ShapeDtypeStruct((n, d), F32),
        compiler_params=_params("parallel"),
        name="merge_out_proj",
    )(*outs, *lses, *perms, x2, mod, expand, w_out_bf)


def _gla_in_kernel(x_ref, mod_ref, g_ref, w_ref, o_ref):
    h = _norm_mod(x_ref[...], g_ref[...], mod_ref[0, 0:1, :], mod_ref[0, 1:2, :])
    o_ref[...] = jnp.dot(h.astype(BF16), w_ref[...], preferred_element_type=F32).astype(BF16)


def _gla_in_proj(x2, mod, g1, w_bf, seq):
    n, d = x2.shape
    ncol = w_bf.shape[1]
    tm = TM_PROJ
    per_b = seq // tm
    return pl.pallas_call(
        _gla_in_kernel,
        grid=(n // tm,),
        in_specs=[
            pl.BlockSpec((tm, d), lambda i: (i, 0)),
            pl.BlockSpec((1, 6, d), lambda i: (i // per_b, 0, 0)),
            pl.BlockSpec((1, d), lambda i: (0, 0)),
            pl.BlockSpec((d, ncol), lambda i: (0, 0)),
        ],
        out_specs=pl.BlockSpec((tm, ncol), lambda i: (i, 0)),
        out_shape=jax.ShapeDtypeStruct((n, ncol), BF16),
        compiler_params=_params("parallel"),
        name="gla_in_proj",
    )(x2, mod, g1, w_bf)


def _gla_kernel(q_ref, k_ref, v_ref, r_ref, a_ref, wg_ref, gb_ref, og_ref, tri_ref, x_ref, mod_ref, wo_ref,
                out_ref, st_ref, o_scr, la_scr):
    t = pl.program_id(1)
    c = GLA_CHUNK

    @pl.when(t == 0)
    def _():
        st_ref[...] = jnp.zeros_like(st_ref)

    g = jnp.dot(a_ref[0], wg_ref[...], preferred_element_type=F32) + gb_ref[...]
    la_scr[...] = (jnp.minimum(g, 0.0) - jnp.log(1.0 + jnp.exp(-jnp.abs(g)))) * (1.0 / GATE_TAU)

    rr = lax.broadcasted_iota(jnp.int32, (c, c), 0)
    cc = lax.broadcasted_iota(jnp.int32, (c, c), 1)
    causal = cc <= rr

    def chunk(ci, carry):
        rows = pl.ds(pl.multiple_of(ci * c, c), c)
        la = la_scr[rows, :]
        b = _dot3(tri_ref[...], la)
        b_last = b[c - 1:c, :]
        q = q_ref[0, rows, :].astype(F32) * (KEY_DIM_B ** -0.5)
        k = k_ref[0, rows, :].astype(F32)
        q_dec = (q * jnp.exp(b)).astype(BF16)
        k_inv = (k * jnp.exp(-b)).astype(BF16)
        k_dec = (k * jnp.exp(b_last - b)).astype(BF16)
        decay = jnp.exp(b_last)
        v = v_ref[0, rows, :]
        for h in range(HEADS_B):
            ks = slice(h * KEY_DIM_B, (h + 1) * KEY_DIM_B)
            vs = slice(h * VAL_DIM_B, (h + 1) * VAL_DIM_B)
            qd, ki, kd, vh = q_dec[:, ks], k_inv[:, ks], k_dec[:, ks], v[:, vs]
            att = lax.dot_general(qd, ki, (((1,), (1,)), ((), ())), preferred_element_type=F32)
            att = jnp.where(causal, att, 0.0)
            st = st_ref[h]
            o = (jnp.dot(att.astype(BF16), vh, preferred_element_type=F32)
                 + lax.dot_general(qd, st.astype(BF16), (((1,), (1,)), ((), ())), preferred_element_type=F32))
            o_scr[rows, vs] = o
            st_ref[h] = st * decay[:, ks] + lax.dot_general(vh, kd, (((0,), (0,)), ((), ())),
                                                             preferred_element_type=F32)
        return carry

    lax.fori_loop(0, q_ref.shape[1] // c, chunk, 0)

    r = r_ref[0].astype(F32)
    parts = []
    for h in range(HEADS_B):
        vs = slice(h * VAL_DIM_B, (h + 1) * VAL_DIM_B)
        oh = o_scr[:, vs]
        ms = jnp.mean(oh * oh, axis=-1, keepdims=True)
        rh = r[:, vs]
        parts.append((oh * lax.rsqrt(ms + RMS_EPS) * og_ref[...] * (rh * jax.nn.sigmoid(rh))).astype(BF16))
    y = jnp.dot(jnp.concatenate(parts, axis=1), wo_ref[...], preferred_element_type=F32)
    out_ref[0] = x_ref[0] + mod_ref[0, 2:3, :] * y


def _gla(proj, x3, mod, wg_bf, gate_bias, out_gain, w_out_bf):
    batch, seq, d = x3.shape
    ncol = proj.shape[1]
    t = T_GLA
    hk = HEADS_B * KEY_DIM_B
    hv = HEADS_B * VAL_DIM_B
    p3 = proj.reshape(batch, seq, ncol)
    tri = (jnp.arange(GLA_CHUNK)[None, :] <= jnp.arange(GLA_CHUNK)[:, None]).astype(BF16)
    a_blk = (2 * hk + 2 * hv) // LANES
    const = lambda shape: pl.BlockSpec(shape, lambda b, i: (0,) * len(shape))
    return pl.pallas_call(
        _gla_kernel,
        grid=(batch, seq // t),
        in_specs=[
            pl.BlockSpec((1, t, hk), lambda b, i: (b, i, 0)),
            pl.BlockSpec((1, t, hk), lambda b, i: (b, i, 1)),
            pl.BlockSpec((1, t, hv), lambda b, i: (b, i, (2 * hk) // hv)),
            pl.BlockSpec((1, t, hv), lambda b, i: (b, i, (2 * hk + hv) // hv)),
            pl.BlockSpec((1, t, LANES), lambda b, i: (b, i, a_blk)),
            const((LANES, hk)),
            const((1, hk)),
            const((1, VAL_DIM_B)),
            const((GLA_CHUNK, GLA_CHUNK)),
            pl.BlockSpec((1, t, d), lambda b, i: (b, i, 0)),
            pl.BlockSpec((1, 6, d), lambda b, i: (b, 0, 0)),
            const((hv, d)),
        ],
        out_specs=pl.BlockSpec((1, t, d), lambda b, i: (b, i, 0)),
        out_shape=jax.ShapeDtypeStruct((batch, seq, d), F32),
        scratch_shapes=[
            pltpu.VMEM((HEADS_B, VAL_DIM_B, KEY_DIM_B), F32),
            pltpu.VMEM((t, hv), F32),
            pltpu.VMEM((t, hk), F32),
        ],
        compiler_params=_params("parallel", "arbitrary"),
        name="gla",
    )(p3, p3, p3, p3, p3, wg_bf, gate_bias, out_gain, tri, x3, mod, w_out_bf)


def _router_kernel(x_ref, mod_ref, g_ref, rw_ref, rb_ref, tri_ref, h_ref, info_ref, cnt_ref, carry_ref):
    i = pl.program_id(0)

    @pl.when(i == 0)
    def _():
        carry_ref[...] = jnp.zeros_like(carry_ref)

    h = _norm_mod(x_ref[...], g_ref[...], mod_ref[0, 3:4, :], mod_ref[0, 4:5, :])
    h_ref[...] = h
    tm = h.shape[0]
    lane = lax.broadcasted_iota(jnp.int32, (tm, LANES), 1)
    lane_f = lane.astype(F32)
    logits = jnp.dot(h, rw_ref[...], precision=HIGHEST, preferred_element_type=F32) + rb_ref[...]
    work = jnp.where(lane < N_EXPERTS, logits, -jnp.inf)
    picks, vals, idxs = [], [], []
    for _ in range(TOP_K):
        m = jnp.max(work, axis=-1, keepdims=True)
        idx = jnp.min(jnp.where(work == m, lane_f, float(LANES)), axis=-1, keepdims=True)
        pick = lane_f == idx
        work = jnp.where(pick, -jnp.inf, work)
        picks.append(pick)
        vals.append(m)
        idxs.append(idx)
    exps = [jnp.exp(v - vals[0]) for v in vals]
    inv = 1.0 / (exps[0] + exps[1] + exps[2] + exps[3])
    chosen = jnp.zeros((tm, LANES), F32)
    for pick in picks:
        chosen = jnp.where(pick, 1.0, chosen)
    before = jnp.dot(tri_ref[...], chosen.astype(BF16), preferred_element_type=F32) + carry_ref[0:1, :]
    info = jnp.zeros((tm, LANES), F32)
    for kk in range(TOP_K):
        rank = jnp.sum(jnp.where(picks[kk], before, 0.0), axis=-1, keepdims=True)
        info = jnp.where(lane == kk, idxs[kk], info)
        info = jnp.where(lane == TOP_K + kk, rank, info)
        info = jnp.where(lane == 2 * TOP_K + kk, exps[kk] * inv, info)
    info_ref[...] = info
    total = carry_ref[0:1, :] + jnp.sum(chosen, axis=0, keepdims=True)
    carry_ref[...] = jnp.broadcast_to(total, carry_ref.shape)
    cnt_ref[...] = jnp.broadcast_to(total, cnt_ref.shape)


def _router(x2, mod, g2, router_w, router_b, seq):
    n, d = x2.shape
    tm = TM_ROUTE
    per_b = seq // tm
    rw = jnp.zeros((d, LANES), F32).at[:, :N_EXPERTS].set(router_w)
    rb = jnp.zeros((1, LANES), F32).at[0, :N_EXPERTS].set(router_b)
    tri = (jnp.arange(tm)[None, :] < jnp.arange(tm)[:, None]).astype(BF16)
    return pl.pallas_call(
        _router_kernel,
        grid=(n // tm,),
        in_specs=[
            pl.BlockSpec((tm, d), lambda i: (i, 0)),
            pl.BlockSpec((1, 6, d), lambda i: (i // per_b, 0, 0)),
            pl.BlockSpec((1, d), lambda i: (0, 0)),
            pl.BlockSpec((d, LANES), lambda i: (0, 0)),
            pl.BlockSpec((1, LANES), lambda i: (0, 0)),
            pl.BlockSpec((tm, tm), lambda i: (0, 0)),
        ],
        out_specs=[
            pl.BlockSpec((tm, d), lambda i: (i, 0)),
            pl.BlockSpec((tm, LANES), lambda i: (i, 0)),
            pl.BlockSpec((8, LANES), lambda i: (0, 0)),
        ],
        out_shape=[
            jax.ShapeDtypeStruct((n, d), F32),
            jax.ShapeDtypeStruct((n, LANES), F32),
            jax.ShapeDtypeStruct((8, LANES), F32),
        ],
        scratch_shapes=[pltpu.VMEM((8, LANES), F32)],
        compiler_params=_params("arbitrary"),
        name="moe_router",
    )(x2, mod, g2, rw, rb, tri)


def _row_copy(src, src_row, dst, dst_row, sem):
    return pltpu.make_async_copy(src.at[pl.ds(src_row, 1)], dst.at[pl.ds(dst_row, 1)], sem)


def _dispatch_kernel(seg_end_ref, padded_ref, dest_ref, h_ref, xs_hbm, zero_ref, sem, zero_sem):
    tm = h_ref.shape[0]

    @pl.when(pl.program_id(0) == 0)
    def _():
        zero_ref[...] = jnp.zeros_like(zero_ref)

        def tail(e):
            return pltpu.make_async_copy(
                zero_ref, xs_hbm.at[pl.ds(pl.multiple_of(seg_end_ref[e] - TM_EXPERT, TM_EXPERT), TM_EXPERT)], zero_sem)

        def start(e, carry):
            pl.when(padded_ref[e] > 0)(lambda: tail(e).start())
            return carry

        def wait(e, carry):
            pl.when(padded_ref[e] > 0)(lambda: tail(e).wait())
            return carry

        lax.fori_loop(0, N_EXPERTS, start, 0)
        lax.fori_loop(0, N_EXPERTS, wait, 0)

        def spare(blk):
            return pltpu.make_async_copy(
                zero_ref, xs_hbm.at[pl.ds(pl.multiple_of(blk * TM_EXPERT, TM_EXPERT), TM_EXPERT)], zero_sem)

        used = seg_end_ref[N_EXPERTS - 1] // TM_EXPERT
        total = xs_hbm.shape[0] // TM_EXPERT
        lax.fori_loop(used, total, lambda blk, carry: (spare(blk).start(), carry)[1], 0)
        lax.fori_loop(used, total, lambda blk, carry: (spare(blk).wait(), carry)[1], 0)

    def start(t, carry):
        for kk in range(TOP_K):
            _row_copy(h_ref, t, xs_hbm, dest_ref[0, 0, TOP_K * t + kk], sem).start(priority=kk % 2)
        return carry

    lax.fori_loop(0, tm, start, 0)

    def wait(t, carry):
        for kk in range(TOP_K):
            _row_copy(h_ref, 0, xs_hbm, 0, sem).wait()
        return carry

    lax.fori_loop(0, tm, wait, 0)


def _dispatch(h, dest, seg_end, padded, cap):
    n, d = h.shape
    tm = TM_MOVE
    dest3 = dest.reshape(n // tm, 1, tm * TOP_K)
    return pl.pallas_call(
        _dispatch_kernel,
        grid_spec=pltpu.PrefetchScalarGridSpec(
            num_scalar_prefetch=2,
            grid=(n // tm,),
            in_specs=[
                pl.BlockSpec((1, 1, tm * TOP_K), lambda i, se, pd: (i, 0, 0), memory_space=pltpu.SMEM),
                pl.BlockSpec((tm, d), lambda i, se, pd: (i, 0)),
            ],
            out_specs=pl.BlockSpec(memory_space=pl.ANY),
            scratch_shapes=[pltpu.VMEM((TM_EXPERT, d), F32), pltpu.SemaphoreType.DMA(()),
                            pltpu.SemaphoreType.DMA(())],
        ),
        out_shape=jax.ShapeDtypeStruct((cap, d), F32),
        compiler_params=_params("arbitrary"),
        name="moe_dispatch",
    )(seg_end, padded, dest3, h)


def _expert_kernel(be_ref, nxt_ref, na_ref, x_ref, wgu_hbm, bgu_ref, wd_hbm, bd_ref, o_ref,
                   gu_stage, d_stage, gu_bf, d_bf, sems, *, layer):
    i = pl.program_id(0)
    live = i < na_ref[0]
    e = be_ref[i]

    def fetch(expert):
        return (pltpu.make_async_copy(wgu_hbm.at[layer, expert], gu_stage, sems.at[0]),
                pltpu.make_async_copy(wd_hbm.at[layer, expert], d_stage, sems.at[1]))

    @pl.when(i == 0)
    def _():
        for c in fetch(e):
            c.start()

    @pl.when(live & ((i == 0) | (e != be_ref[jnp.maximum(i - 1, 0)])))
    def _():
        for c in fetch(e):
            c.wait()
        gu_bf[...] = gu_stage[...].astype(BF16)
        d_bf[...] = d_stage[...].astype(BF16)

        @pl.when(nxt_ref[i] >= 0)
        def _():
            for c in fetch(nxt_ref[i]):
                c.start()

    @pl.when(jnp.logical_not(live))
    def _():
        o_ref[...] = jnp.zeros_like(o_ref)

    @pl.when(live)
    def _():
        f = d_bf.shape[0]
        gu = jnp.dot(x_ref[...].astype(BF16), gu_bf[...], preferred_element_type=F32) + bgu_ref[0]
        gate = jnp.minimum(gu[:, :f], SWIGLU_LIMIT)
        up = jnp.clip(gu[:, f:], -SWIGLU_LIMIT, SWIGLU_LIMIT)
        act = (up + 1.0) * (gate * jax.nn.sigmoid(SWIGLU_ALPHA * gate))
        o_ref[...] = jnp.dot(act.astype(BF16), d_bf[...], preferred_element_type=F32) + bd_ref[0]


def _experts(xs, block_e, next_e, n_active, w_gu, b_gu, w_down, b_down, layer):
    cap, d = xs.shape
    _, e, _, f2 = w_gu.shape
    f = f2 // 2
    tm = TM_EXPERT
    nb = cap // tm
    return pl.pallas_call(
        functools.partial(_expert_kernel, layer=layer),
        grid_spec=pltpu.PrefetchScalarGridSpec(
            num_scalar_prefetch=3,
            grid=(nb,),
            in_specs=[
                pl.BlockSpec((tm, d), lambda i, be, nx, na: (jnp.minimum(i, na[0] - 1), 0)),
                pl.BlockSpec(memory_space=pl.ANY),
                pl.BlockSpec((1, 1, f2), lambda i, be, nx, na: (be[i], 0, 0)),
                pl.BlockSpec(memory_space=pl.ANY),
                pl.BlockSpec((1, 1, d), lambda i, be, nx, na: (be[i], 0, 0)),
            ],
            out_specs=pl.BlockSpec((tm, d), lambda i, be, nx, na: (i, 0)),
            scratch_shapes=[pltpu.VMEM((d, f2), F32), pltpu.VMEM((f, d), F32), pltpu.VMEM((d, f2), BF16),
                            pltpu.VMEM((f, d), BF16), pltpu.SemaphoreType.DMA((2,))],
        ),
        out_shape=jax.ShapeDtypeStruct((cap, d), F32),
        compiler_params=_params("arbitrary"),
        name="moe_experts",
    )(block_e, next_e, n_active, xs, w_gu, b_gu.reshape(e, 1, f2), w_down, b_down.reshape(e, 1, d))


def _combine_kernel(dest_ref, info_ref, x_ref, mod_ref, ys_hbm, out_ref, buf, sem):
    tm = x_ref.shape[0]

    def start(t, carry):
        for kk in range(TOP_K):
            _row_copy(ys_hbm, dest_ref[0, 0, TOP_K * t + kk], buf.at[kk], t, sem).start(priority=kk % 2)
        return carry

    lax.fori_loop(0, tm, start, 0)

    def wait(t, carry):
        for kk in range(TOP_K):
            _row_copy(ys_hbm, 0, buf.at[kk], 0, sem).wait()
        return carry

    lax.fori_loop(0, tm, wait, 0)

    info = info_ref[...]
    y = None
    for kk in range(TOP_K):
        term = info[:, 2 * TOP_K + kk:2 * TOP_K + kk + 1] * buf[kk]
        y = term if y is None else y + term
    out_ref[...] = x_ref[...] + mod_ref[0, 5:6, :] * y


def _combine(ys, dest, info, x2, mod, seq):
    n, d = x2.shape
    tm = TM_MOVE
    per_b = seq // tm
    dest3 = dest.reshape(n // tm, 1, tm * TOP_K)
    return pl.pallas_call(
        _combine_kernel,
        grid=(n // tm,),
        in_specs=[
            pl.BlockSpec((1, 1, tm * TOP_K), lambda i: (i, 0, 0), memory_space=pltpu.SMEM),
            pl.BlockSpec((tm, LANES), lambda i: (i, 0)),
            pl.BlockSpec((tm, d), lambda i: (i, 0)),
            pl.BlockSpec((1, 6, d), lambda i: (i // per_b, 0, 0)),
            pl.BlockSpec(memory_space=pl.ANY),
        ],
        out_specs=pl.BlockSpec((tm, d), lambda i: (i, 0)),
        out_shape=jax.ShapeDtypeStruct((n, d), F32),
        scratch_shapes=[pltpu.VMEM((TOP_K, tm, d), F32), pltpu.SemaphoreType.DMA(())],
        compiler_params=_params("arbitrary"),
        name="moe_combine",
    )(dest3, info, x2, mod, ys)


def _moe(x2, mod, g2, router_w, router_b, w_gu, b_gu, w_down, b_down, layer, seq):
    n, d = x2.shape
    h, info, cnt = _router(x2, mod, g2, router_w, router_b, seq)
    counts = cnt[0, :N_EXPERTS].astype(jnp.int32)
    padded = (counts + TM_EXPERT - 1) // TM_EXPERT * TM_EXPERT
    seg_end = jnp.cumsum(padded)
    seg_start = seg_end - padded
    cap = n * TOP_K + N_EXPERTS * TM_EXPERT
    nb = cap // TM_EXPERT
    block_start = jnp.arange(nb, dtype=jnp.int32) * TM_EXPERT
    block_e = jnp.minimum(jnp.sum(seg_end[None, :] <= block_start[:, None], axis=1), N_EXPERTS - 1).astype(jnp.int32)
    n_active = (seg_end[-1:] // TM_EXPERT).astype(jnp.int32)
    expert = info[:, :TOP_K].astype(jnp.int32)
    rank = info[:, TOP_K:2 * TOP_K].astype(jnp.int32)
    dest = (seg_start[expert] + rank).reshape(-1)
    ids = jnp.arange(N_EXPERTS, dtype=jnp.int32)
    later = (ids[None, :] > ids[:, None]) & (padded[None, :] > 0)
    next_of = jnp.min(jnp.where(later, ids[None, :], N_EXPERTS), axis=1)
    next_e = jnp.where(next_of < N_EXPERTS, next_of, -1).astype(jnp.int32)[block_e]
    xs = _dispatch(h, dest, seg_end.astype(jnp.int32), padded.astype(jnp.int32), cap)
    ys = _experts(xs, block_e, next_e, n_active, w_gu, b_gu, w_down, b_down, layer)
    return _combine(ys, dest, info, x2, mod, seq)


def kernel(x, c, positions, ada_w, ada_b, norm1_g, norm2_g, a_w_in, a_q_gain, a_k_gain, a_w_out, b_w_in,
           b_w_gate_up, b_gate_bias, b_out_gain, b_w_out, router_w, router_b, moe_w_gu, moe_b_gu, moe_w_down,
           moe_b_down):
    batch, seq, d = x.shape
    depth = ada_w.shape[0]
    n = batch * seq
    mods = _adaln(c, ada_w, ada_b).reshape(depth, batch, 6, d)
    x2 = x.reshape(n, d)
    for layer in range(depth):
        mod = mods[layer]
        j = layer // 2
        g1 = norm1_g[layer].reshape(1, d)
        g2 = norm2_g[layer].reshape(1, d)
        if layer % 2 == 0:
            qkv = _qkv_proj(x2, mod, g1, a_w_in[j], a_q_gain[j], a_k_gain[j], positions, batch, seq)
            outs, lses = zip(*[_dilated_attention(*qkv[g], g) for g in range(N_GROUPS)])
            x2 = _merge_proj(outs, lses, x2, mod, a_w_out[j].astype(BF16), seq)
        else:
            hk = HEADS_B * KEY_DIM_B
            hv = HEADS_B * VAL_DIM_B
            width = 2 * hk + 2 * hv
            w_in = jnp.zeros((d, width + LANES), BF16).at[:, :width + GATE_RANK].set(b_w_in[j].astype(BF16))
            wg = jnp.zeros((LANES, hk), BF16).at[:GATE_RANK].set(b_w_gate_up[j].astype(BF16))
            proj = _gla_in_proj(x2, mod, g1, w_in, seq)
            x3 = _gla(proj, x2.reshape(batch, seq, d), mod, wg, b_gate_bias[j].reshape(1, hk),
                      b_out_gain[j].reshape(1, VAL_DIM_B), b_w_out[j].astype(BF16))
            x2 = x3.reshape(n, d)
        x2 = _moe(x2, mod, g2, router_w[layer], router_b[layer], moe_w_gu, moe_b_gu[layer],
                  moe_w_down, moe_b_down[layer], layer, seq)
    return x2.reshape(batch, seq, d)
```

```python
import functools

import jax
import jax.numpy as jnp
from jax import lax
from jax.experimental import pallas as pl
from jax.experimental.pallas import tpu as pltpu

F32 = jnp.float32
BF16 = jnp.bfloat16
HIGHEST = lax.Precision.HIGHEST

RMS_EPS = 1e-6
DILATIONS = ((128, 1), (512, 4), (2048, 16))
N_GROUPS = 3
HEADS_A = 16
HEAD_DIM_A = 64
BAND = 128
ROPE_THETA = 10000.0
NEG_INF = -1e30
HEADS_B = 4
KEY_DIM_B = 128
VAL_DIM_B = 256
GATE_RANK = 16
GATE_TAU = 16.0
GLA_CHUNK = 64
N_EXPERTS = 32
TOP_K = 4
SWIGLU_LIMIT = 7.0
SWIGLU_ALPHA = 1.702

LANES = 128
VMEM_LIMIT = 56 * 1024 * 1024

TM_PROJ = 512
TM_QKV = 256
TM_ROUTE = 512
TM_EXPERT = 512
TM_MOVE = 512
T_GLA = 512


def _params(*sem):
    return pltpu.CompilerParams(dimension_semantics=sem, vmem_limit_bytes=VMEM_LIMIT)


def _norm_mod(x, gain, shift, scale):
    ms = jnp.mean(x * x, axis=-1, keepdims=True)
    y = x * lax.rsqrt(ms + RMS_EPS) * gain
    return y * (1.0 + scale) + shift


def _pack_pairs(v):
    k = v.shape[1] // 2
    bits = lax.bitcast_convert_type(v.astype(BF16).astype(F32), jnp.uint32)
    return (bits[:, :k] >> 16) | bits[:, k:]


def _unpack_pairs(w):
    lo = lax.bitcast_convert_type(w << 16, F32)
    hi = lax.bitcast_convert_type(w & jnp.uint32(0xFFFF0000), F32)
    return jnp.concatenate([lo, hi], axis=1)


def _adaln_kernel(c_ref, w_ref, b_ref, o_ref):
    c = c_ref[...]
    cond = c * jax.nn.sigmoid(c)
    o_ref[0] = jnp.dot(cond, w_ref[0], precision=HIGHEST, preferred_element_type=F32) + b_ref[0]


def _adaln(c, ada_w, ada_b):
    depth, d, n6 = ada_w.shape
    b = c.shape[0]
    tn = 1536
    return pl.pallas_call(
        _adaln_kernel,
        grid=(depth, n6 // tn),
        in_specs=[
            pl.BlockSpec((b, d), lambda l, j: (0, 0)),
            pl.BlockSpec((1, d, tn), lambda l, j: (l, 0, j)),
            pl.BlockSpec((1, 1, tn), lambda l, j: (l, 0, j)),
        ],
        out_specs=pl.BlockSpec((1, b, tn), lambda l, j: (l, 0, j)),
        out_shape=jax.ShapeDtypeStruct((depth, b, n6), F32),
        compiler_params=_params("parallel", "parallel"),
        name="adaln",
    )(c, ada_w, ada_b.reshape(depth, 1, n6))


def _rope_kernel(pos_ref, cos_ref, sin_ref):
    pos = pos_ref[...].astype(F32)
    lane = lax.broadcasted_iota(jnp.int32, (1, LANES), 1)
    half = HEAD_DIM_A // 2
    fidx = (lane & (half - 1)).astype(F32)
    inv_freq = jnp.power(jnp.full((1, LANES), ROPE_THETA, F32), -(fidx / half))
    ang = pos * inv_freq
    cos_ref[...] = jnp.cos(ang)
    s = jnp.sin(ang)
    sin_ref[...] = jnp.where(lane < LANES // 2, -s, s)


def _rope_tables(positions):
    n = positions.size
    tm = 1024
    return pl.pallas_call(
        _rope_kernel,
        grid=(n // tm,),
        in_specs=[pl.BlockSpec((tm, 1), lambda i: (i, 0))],
        out_specs=[pl.BlockSpec((tm, LANES), lambda i: (i, 0))] * 2,
        out_shape=[jax.ShapeDtypeStruct((n, LANES), F32)] * 2,
        compiler_params=_params("parallel"),
        name="rope_tables",
    )(positions.reshape(n, 1))


def _stream_order(tm, dil):
    s = jnp.arange(tm)
    return (s % (tm // dil)) * dil + s // (tm // dil)


def _qkv_kernel(x_ref, mod_ref, g_ref, w_hbm, gain_ref, cos_ref, sin_ref, bd_ref, p1_ref, p2_ref, *rest):
    o_refs, (w_ref, w_sem) = rest[:3 * N_GROUPS], rest[3 * N_GROUPS:]
    tm, d = x_ref.shape

    @pl.when(pl.program_id(0) == 0)
    def _():
        copy = pltpu.make_async_copy(w_hbm, w_ref, w_sem)
        copy.start()
        copy.wait()

    h = _norm_mod(x_ref[...], g_ref[...], mod_ref[0, 0:1, :], mod_ref[0, 1:2, :]).astype(BF16)
    hs = [h] + [jnp.dot(p[...], h, preferred_element_type=F32).astype(BF16) for p in (p1_ref, p2_ref)]
    for jj in range(3 * N_GROUPS):
        g, comp = divmod(jj, 3)
        dil = DILATIONS[g][1]
        rows = tm // dil
        out = o_refs[jj]
        acc = jnp.dot(hs[g], w_ref[:, jj * d:(jj + 1) * d], preferred_element_type=F32)
        if comp == 2:
            val = acc.astype(BF16)
            for r in range(dil):
                out[0, r] = val[r * rows:(r + 1) * rows]
            continue
        gain = gain_ref[jj]
        cos = cos_ref[g]
        sin = sin_ref[g]
        for cc in range(d // 256):
            a = acc[:, cc * 256:(cc + 1) * 256]
            ss = jnp.dot((a * a).astype(BF16), bd_ref[...], preferred_element_type=F32)
            qn = a * lax.rsqrt(ss * (1.0 / HEAD_DIM_A) + RMS_EPS) * gain[:, cc * 256:(cc + 1) * 256]
            for hh in range(2):
                xx = qn[:, hh * LANES:(hh + 1) * LANES]
                val = (xx * cos + pltpu.roll(xx, 64, 1) * sin).astype(BF16)
                c1 = cc * 256 + hh * LANES
                for r in range(dil):
                    out[0, r, :, c1:c1 + LANES] = val[r * rows:(r + 1) * rows]


def _split_half_columns(w):
    lead = w.shape[:-1]
    w = w.reshape(*lead, HEADS_A // 2, 2, 2, HEAD_DIM_A // 2)
    return jnp.swapaxes(w, -3, -2).reshape(*lead, HEADS_A * HEAD_DIM_A)


def _qkv_proj(x2, mod, g1, w_in, q_gain, k_gain, positions, batch, seq):
    n, d = x2.shape
    tm = TM_QKV
    per_b = seq // tm
    reps = d // HEAD_DIM_A
    w = w_in.reshape(d, N_GROUPS, 3, d)
    w = jnp.concatenate([_split_half_columns(w[:, :, :2]), w[:, :, 2:]], axis=2).reshape(d, 3 * N_GROUPS * d)
    gains = []
    for g in range(N_GROUPS):
        gains += [_split_half_columns(jnp.tile(q_gain[g], reps)) * (HEAD_DIM_A ** -0.5),
                  _split_half_columns(jnp.tile(k_gain[g], reps)), jnp.ones((d,), F32)]
    gains = jnp.stack(gains).reshape(3 * N_GROUPS, 1, d)
    orders = [_stream_order(tm, dil) for _, dil in DILATIONS]
    pos_tiles = positions.reshape(n // tm, tm)
    cos_t, sin_t = _rope_tables(jnp.stack([pos_tiles[:, o] for o in orders]))
    cos_t, sin_t = cos_t.reshape(N_GROUPS, n, LANES), sin_t.reshape(N_GROUPS, n, LANES)
    perms = [(o[:, None] == jnp.arange(tm)[None, :]).astype(BF16) for o in orders[1:]]
    lane = jnp.arange(256)
    head_of = (lane // LANES) * 2 + (lane // 32) % 2
    bd = (head_of[:, None] == head_of[None, :]).astype(BF16)
    dils = [DILATIONS[jj // 3][1] for jj in range(3 * N_GROUPS)]
    const = lambda shape: pl.BlockSpec(shape, lambda i: (0,) * len(shape))
    flat = pl.pallas_call(
        _qkv_kernel,
        grid=(n // tm,),
        in_specs=[
            pl.BlockSpec((tm, d), lambda i: (i, 0)),
            pl.BlockSpec((1, 6, d), lambda i: (i // per_b, 0, 0)),
            const((1, d)),
            pl.BlockSpec(memory_space=pl.ANY),
            const((3 * N_GROUPS, 1, d)),
            pl.BlockSpec((N_GROUPS, tm, LANES), lambda i: (0, i, 0)),
            pl.BlockSpec((N_GROUPS, tm, LANES), lambda i: (0, i, 0)),
            const((256, 256)),
            const((tm, tm)),
            const((tm, tm)),
        ],
        out_specs=[pl.BlockSpec((1, dil, tm // dil, d), lambda i: (i // per_b, 0, i % per_b, 0)) for dil in dils],
        out_shape=[jax.ShapeDtypeStruct((batch, dil, seq // dil, d), BF16) for dil in dils],
        scratch_shapes=[pltpu.VMEM((d, 3 * N_GROUPS * d), BF16), pltpu.SemaphoreType.DMA(())],
        compiler_params=_params("arbitrary"),
        name="qkv_proj",
    )(x2, mod, g1, w.astype(BF16), gains, cos_t, sin_t, bd, *perms)
    return [flat[3 * g:3 * g + 3] for g in range(N_GROUPS)]


def _attn_kernel(q_ref, kp_ref, kc_ref, vp_ref, vc_ref, o_ref, lse_ref, *, steps):
    q_ref, kp_ref, kc_ref, vp_ref, vc_ref, o_ref, lse_ref = (
        r.at[0] for r in (q_ref, kp_ref, kc_ref, vp_ref, vc_ref, o_ref, lse_ref))
    n = pl.program_id(2)
    row = lax.broadcasted_iota(jnp.int32, (BAND, 2 * BAND), 0)
    col = lax.broadcasted_iota(jnp.int32, (BAND, 2 * BAND), 1)
    delta = row + BAND - col
    valid = (delta >= 0) & (delta <= steps) & ((col >= BAND) | (n > 0))
    lane = lax.broadcasted_iota(jnp.int32, (BAND, LANES), 1)
    first_head = lane < HEAD_DIM_A
    even_head = (lane & (HEAD_DIM_A // 2)) == 0
    lse_tile = jnp.zeros((BAND, LANES), F32)
    for p in range(HEADS_A // 2):
        sl = slice(p * LANES, (p + 1) * LANES)
        qp = q_ref[0, :, sl]
        kcat = jnp.concatenate([kp_ref[0, :, sl], kc_ref[0, :, sl]], axis=0)
        vcat = jnp.concatenate([vp_ref[0, :, sl], vc_ref[0, :, sl]], axis=0)
        outs = []
        for hh in range(2):
            keep = even_head if hh == 0 else jnp.logical_not(even_head)
            qm = jnp.where(keep, qp, jnp.zeros_like(qp))
            s = lax.dot_general(qm, kcat, (((1,), (1,)), ((), ())), preferred_element_type=F32)
            s = jnp.where(valid, s, NEG_INF)
            m = jnp.max(s, axis=-1, keepdims=True)
            pexp = jnp.exp(s - m)
            den = jnp.sum(pexp, axis=-1, keepdims=True)
            o = jnp.dot(pexp.astype(BF16), vcat, preferred_element_type=F32) / den
            outs.append(o)
            lse_tile = jnp.where(lane == (2 * p + hh), m + jnp.log(den), lse_tile)
        o_ref[0, :, sl] = jnp.where(first_head, outs[0], outs[1]).astype(BF16)
    lse_ref[0] = lse_tile


def _dilated_attention(q, k, v, g):
    window, dil = DILATIONS[g]
    batch, _, l, d = q.shape
    cur = pl.BlockSpec((1, 1, BAND, d), lambda b, r, i: (b, r, i, 0))
    prev = pl.BlockSpec((1, 1, BAND, d), lambda b, r, i: (b, r, jnp.maximum(i - 1, 0), 0))
    return pl.pallas_call(
        functools.partial(_attn_kernel, steps=window // dil),
        grid=(batch, dil, l // BAND),
        in_specs=[cur, prev, cur, prev, cur],
        out_specs=[cur, pl.BlockSpec((1, 1, BAND, LANES), lambda b, r, i: (b, r, i, 0))],
        out_shape=[
            jax.ShapeDtypeStruct((batch, dil, l, d), BF16),
            jax.ShapeDtypeStruct((batch, dil, l, LANES), F32),
        ],
        compiler_params=_params("parallel", "parallel", "arbitrary"),
        name=f"dilated_attn_g{g}",
    )(q, k, k, v, v)


def _split3(v):
    hi = v.astype(BF16)
    rem = v - hi.astype(F32)
    mid = rem.astype(BF16)
    return hi, mid, (rem - mid.astype(F32)).astype(BF16)


def _dot3(mat_bf, v):
    return sum(jnp.dot(mat_bf, part, preferred_element_type=F32) for part in _split3(v))


def _merge_proj_kernel(o0_ref, o1_ref, o2_ref, l0_ref, l1_ref, l2_ref, p1_ref, p2_ref, x_ref, mod_ref, e_ref,
                       w_ref, out_ref):
    tm, d = x_ref.shape
    perms = (None, p1_ref, p2_ref)
    lses, outs = [], []
    for perm, l_ref, o_ref in zip(perms, (l0_ref, l1_ref, l2_ref), (o0_ref, o1_ref, o2_ref)):
        lse = l_ref[0].reshape(tm, LANES)
        o = o_ref[0].reshape(tm, d)
        if perm is None:
            lses.append(lse)
            outs.append(o.astype(F32))
        else:
            lses.append(_dot3(perm[...], lse))
            outs.append(jnp.dot(perm[...], o, preferred_element_type=F32))
    m = jnp.maximum(jnp.maximum(lses[0], lses[1]), lses[2])
    exps = [jnp.exp(l - m) for l in lses]
    inv = 1.0 / (exps[0] + exps[1] + exps[2])
    acc = None
    for e, o in zip(exps, outs):
        w = e * inv
        w_hi = w.astype(BF16)
        w_lo = (w - w_hi.astype(F32)).astype(BF16)
        wide = (jnp.dot(w_hi, e_ref[...], preferred_element_type=F32)
                + jnp.dot(w_lo, e_ref[...], preferred_element_type=F32))
        acc = wide * o if acc is None else acc + wide * o
    y = jnp.dot(acc.astype(BF16), w_ref[...], preferred_element_type=F32)
    out_ref[...] = x_ref[...] + mod_ref[0, 2:3, :] * y


def _merge_proj(outs, lses, x2, mod, w_out_bf, seq):
    n, d = x2.shape
    tm = TM_PROJ
    per_b = seq // tm
    expand = (jnp.arange(LANES)[:, None] == jnp.arange(d)[None, :] // HEAD_DIM_A).astype(BF16)
    tok = jnp.arange(tm)
    perms = [(((tok % dil) * (tm // dil) + tok // dil)[:, None] == tok[None, :]).astype(BF16)
             for _, dil in DILATIONS[1:]]
    slab = lambda dil, width: pl.BlockSpec((1, dil, tm // dil, width), lambda i: (i // per_b, 0, i % per_b, 0))
    const = lambda shape: pl.BlockSpec(shape, lambda i: (0, 0))
    return pl.pallas_call(
        _merge_proj_kernel,
        grid=(n // tm,),
        in_specs=[slab(dil, d) for _, dil in DILATIONS] + [slab(dil, LANES) for _, dil in DILATIONS] + [
            const((tm, tm)),
            const((tm, tm)),
            pl.BlockSpec((tm, d), lambda i: (i, 0)),
            pl.BlockSpec((1, 6, d), lambda i: (i // per_b, 0, 0)),
            const((LANES, d)),
            const((d, d)),
        ],
        out_specs=pl.BlockSpec((tm, d), lambda i: (i, 0)),
        out_shape=jax.ShapeDtypeStruct((n, d), F32),
        compiler_params=_params("parallel"),
        name="merge_out_proj",
    )(*outs, *lses, *perms, x2, mod, expand, w_out_bf)


def _gla_in_kernel(x_ref, mod_ref, g_ref, w_ref, o_ref):
    h = _norm_mod(x_ref[...], g_ref[...], mod_ref[0, 0:1, :], mod_ref[0, 1:2, :])
    o_ref[...] = jnp.dot(h.astype(BF16), w_ref[...], preferred_element_type=F32).astype(BF16)


def _gla_in_proj(x2, mod, g1, w_bf, seq):
    n, d = x2.shape
    ncol = w_bf.shape[1]
    tm = TM_PROJ
    per_b = seq // tm
    return pl.pallas_call(
        _gla_in_kernel,
        grid=(n // tm,),
        in_specs=[
            pl.BlockSpec((tm, d), lambda i: (i, 0)),
            pl.BlockSpec((1, 6, d), lambda i: (i // per_b, 0, 0)),
            pl.BlockSpec((1, d), lambda i: (0, 0)),
            pl.BlockSpec((d, ncol), lambda i: (0, 0)),
        ],
        out_specs=pl.BlockSpec((tm, ncol), lambda i: (i, 0)),
        out_shape=jax.ShapeDtypeStruct((n, ncol), BF16),
        compiler_params=_params("parallel"),
        name="gla_in_proj",
    )(x2, mod, g1, w_bf)


def _gla_kernel(q_ref, k_ref, v_ref, r_ref, a_ref, wg_ref, gb_ref, og_ref, tri_ref, x_ref, mod_ref, wo_ref,
                out_ref, st_ref, o_scr, la_scr):
    t = pl.program_id(1)
    c = GLA_CHUNK

    @pl.when(t == 0)
    def _():
        st_ref[...] = jnp.zeros_like(st_ref)

    g = jnp.dot(a_ref[0], wg_ref[...], preferred_element_type=F32) + gb_ref[...]
    la_scr[...] = (jnp.minimum(g, 0.0) - jnp.log(1.0 + jnp.exp(-jnp.abs(g)))) * (1.0 / GATE_TAU)

    rr = lax.broadcasted_iota(jnp.int32, (c, c), 0)
    cc = lax.broadcasted_iota(jnp.int32, (c, c), 1)
    causal = cc <= rr

    def chunk(ci, carry):
        rows = pl.ds(pl.multiple_of(ci * c, c), c)
        la = la_scr[rows, :]
        b = _dot3(tri_ref[...], la)
        b_last = b[c - 1:c, :]
        q = q_ref[0, rows, :].astype(F32) * (KEY_DIM_B ** -0.5)
        k = k_ref[0, rows, :].astype(F32)
        q_dec = (q * jnp.exp(b)).astype(BF16)
        k_inv = (k * jnp.exp(-b)).astype(BF16)
        k_dec = (k * jnp.exp(b_last - b)).astype(BF16)
        decay = jnp.exp(b_last)
        v = v_ref[0, rows, :]
        for h in range(HEADS_B):
            ks = slice(h * KEY_DIM_B, (h + 1) * KEY_DIM_B)
            vs = slice(h * VAL_DIM_B, (h + 1) * VAL_DIM_B)
            qd, ki, kd, vh = q_dec[:, ks], k_inv[:, ks], k_dec[:, ks], v[:, vs]
            att = lax.dot_general(qd, ki, (((1,), (1,)), ((), ())), preferred_element_type=F32)
            att = jnp.where(causal, att, 0.0)
            st = st_ref[h]
            o = (jnp.dot(att.astype(BF16), vh, preferred_element_type=F32)
                 + lax.dot_general(qd, st.astype(BF16), (((1,), (1,)), ((), ())), preferred_element_type=F32))
            o_scr[rows, vs] = o
            st_ref[h] = st * decay[:, ks] + lax.dot_general(vh, kd, (((0,), (0,)), ((), ())),
                                                             preferred_element_type=F32)
        return carry

    lax.fori_loop(0, q_ref.shape[1] // c, chunk, 0)

    r = r_ref[0].astype(F32)
    parts = []
    for h in range(HEADS_B):
        vs = slice(h * VAL_DIM_B, (h + 1) * VAL_DIM_B)
        oh = o_scr[:, vs]
        ms = jnp.mean(oh * oh, axis=-1, keepdims=True)
        rh = r[:, vs]
        parts.append((oh * lax.rsqrt(ms + RMS_EPS) * og_ref[...] * (rh * jax.nn.sigmoid(rh))).astype(BF16))
    y = jnp.dot(jnp.concatenate(parts, axis=1), wo_ref[...], preferred_element_type=F32)
    out_ref[0] = x_ref[0] + mod_ref[0, 2:3, :] * y


def _gla(proj, x3, mod, wg_bf, gate_bias, out_gain, w_out_bf):
    batch, seq, d = x3.shape
    ncol = proj.shape[1]
    t = T_GLA
    hk = HEADS_B * KEY_DIM_B
    hv = HEADS_B * VAL_DIM_B
    p3 = proj.reshape(batch, seq, ncol)
    tri = (jnp.arange(GLA_CHUNK)[None, :] <= jnp.arange(GLA_CHUNK)[:, None]).astype(BF16)
    a_blk = (2 * hk + 2 * hv) // LANES
    const = lambda shape: pl.BlockSpec(shape, lambda b, i: (0,) * len(shape))
    return pl.pallas_call(
        _gla_kernel,
        grid=(batch, seq // t),
        in_specs=[
            pl.BlockSpec((1, t, hk), lambda b, i: (b, i, 0)),
            pl.BlockSpec((1, t, hk), lambda b, i: (b, i, 1)),
            pl.BlockSpec((1, t, hv), lambda b, i: (b, i, (2 * hk) // hv)),
            pl.BlockSpec((1, t, hv), lambda b, i: (b, i, (2 * hk + hv) // hv)),
            pl.BlockSpec((1, t, LANES), lambda b, i: (b, i, a_blk)),
            const((LANES, hk)),
            const((1, hk)),
            const((1, VAL_DIM_B)),
            const((GLA_CHUNK, GLA_CHUNK)),
            pl.BlockSpec((1, t, d), lambda b, i: (b, i, 0)),
            pl.BlockSpec((1, 6, d), lambda b, i: (b, 0, 0)),
            const((hv, d)),
        ],
        out_specs=pl.BlockSpec((1, t, d), lambda b, i: (b, i, 0)),
        out_shape=jax.ShapeDtypeStruct((batch, seq, d), F32),
        scratch_shapes=[
            pltpu.VMEM((HEADS_B, VAL_DIM_B, KEY_DIM_B), F32),
            pltpu.VMEM((t, hv), F32),
            pltpu.VMEM((t, hk), F32),
        ],
        compiler_params=_params("parallel", "arbitrary"),
        name="gla",
    )(p3, p3, p3, p3, p3, wg_bf, gate_bias, out_gain, tri, x3, mod, w_out_bf)


def _router_kernel(x_ref, mod_ref, g_ref, rw_ref, rb_ref, tri_ref, h_ref, info_ref, cnt_ref, carry_ref):
    i = pl.program_id(0)

    @pl.when(i == 0)
    def _():
        carry_ref[...] = jnp.zeros_like(carry_ref)

    h = _norm_mod(x_ref[...], g_ref[...], mod_ref[0, 3:4, :], mod_ref[0, 4:5, :])
    h_ref[...] = _pack_pairs(h)
    tm = h.shape[0]
    lane = lax.broadcasted_iota(jnp.int32, (tm, LANES), 1)
    lane_f = lane.astype(F32)
    h_hi = h.astype(BF16)
    h_lo = (h - h_hi.astype(F32)).astype(BF16)
    logits = (jnp.dot(h_hi, rw_ref[0], preferred_element_type=F32) + jnp.dot(h_lo, rw_ref[0], preferred_element_type=F32)
              + jnp.dot(h_hi, rw_ref[1], preferred_element_type=F32) + rb_ref[...])
    work = jnp.where(lane < N_EXPERTS, logits, -jnp.inf)
    picks, vals, idxs = [], [], []
    for _ in range(TOP_K):
        m = jnp.max(work, axis=-1, keepdims=True)
        idx = jnp.min(jnp.where(work == m, lane_f, float(LANES)), axis=-1, keepdims=True)
        pick = lane_f == idx
        work = jnp.where(pick, -jnp.inf, work)
        picks.append(pick)
        vals.append(m)
        idxs.append(idx)
    exps = [jnp.exp(v - vals[0]) for v in vals]
    inv = 1.0 / (exps[0] + exps[1] + exps[2] + exps[3])
    chosen = jnp.zeros((tm, LANES), F32)
    for pick in picks:
        chosen = jnp.where(pick, 1.0, chosen)
    before = jnp.dot(tri_ref[...], chosen.astype(BF16), preferred_element_type=F32) + carry_ref[0:1, :]
    info = jnp.zeros((tm, LANES), F32)
    for kk in range(TOP_K):
        rank = jnp.sum(jnp.where(picks[kk], before, 0.0), axis=-1, keepdims=True)
        info = jnp.where(lane == kk, idxs[kk], info)
        info = jnp.where(lane == TOP_K + kk, rank, info)
        info = jnp.where(lane == 2 * TOP_K + kk, exps[kk] * inv, info)
    info_ref[...] = info
    total = carry_ref[0:1, :] + jnp.sum(chosen, axis=0, keepdims=True)
    carry_ref[...] = jnp.broadcast_to(total, carry_ref.shape)
    cnt_ref[...] = jnp.broadcast_to(total, cnt_ref.shape)


def _router(x2, mod, g2, router_w, router_b, seq):
    n, d = x2.shape
    tm = TM_ROUTE
    per_b = seq // tm
    rw = jnp.zeros((d, LANES), F32).at[:, :N_EXPERTS].set(router_w)
    rw_hi = rw.astype(BF16)
    rw = jnp.stack([rw_hi, (rw - rw_hi.astype(F32)).astype(BF16)])
    rb = jnp.zeros((1, LANES), F32).at[0, :N_EXPERTS].set(router_b)
    tri = (jnp.arange(tm)[None, :] < jnp.arange(tm)[:, None]).astype(BF16)
    return pl.pallas_call(
        _router_kernel,
        grid=(n // tm,),
        in_specs=[
            pl.BlockSpec((tm, d), lambda i: (i, 0)),
            pl.BlockSpec((1, 6, d), lambda i: (i // per_b, 0, 0)),
            pl.BlockSpec((1, d), lambda i: (0, 0)),
            pl.BlockSpec((2, d, LANES), lambda i: (0, 0, 0)),
            pl.BlockSpec((1, LANES), lambda i: (0, 0)),
            pl.BlockSpec((tm, tm), lambda i: (0, 0)),
        ],
        out_specs=[
            pl.BlockSpec((tm, d // 2), lambda i: (i, 0)),
            pl.BlockSpec((tm, LANES), lambda i: (i, 0)),
            pl.BlockSpec((8, LANES), lambda i: (0, 0)),
        ],
        out_shape=[
            jax.ShapeDtypeStruct((n, d // 2), jnp.uint32),
            jax.ShapeDtypeStruct((n, LANES), F32),
            jax.ShapeDtypeStruct((8, LANES), F32),
        ],
        scratch_shapes=[pltpu.VMEM((8, LANES), F32)],
        compiler_params=_params("arbitrary"),
        name="moe_router",
    )(x2, mod, g2, rw, rb, tri)


def _row_copy(src, src_row, dst, dst_row, sem):
    return pltpu.make_async_copy(src.at[pl.ds(src_row, 1)], dst.at[pl.ds(dst_row, 1)], sem)


def _dispatch_kernel(seg_end_ref, padded_ref, dest_ref, h_ref, xs_hbm, zero_ref, sem, zero_sem):
    tm = h_ref.shape[0]

    @pl.when(pl.program_id(0) == 0)
    def _():
        zero_ref[...] = jnp.zeros_like(zero_ref)

        def tail(e):
            return pltpu.make_async_copy(
                zero_ref, xs_hbm.at[pl.ds(pl.multiple_of(seg_end_ref[e] - TM_EXPERT, TM_EXPERT), TM_EXPERT)], zero_sem)

        def start(e, carry):
            pl.when(padded_ref[e] > 0)(lambda: tail(e).start())
            return carry

        def wait(e, carry):
            pl.when(padded_ref[e] > 0)(lambda: tail(e).wait())
            return carry

        lax.fori_loop(0, N_EXPERTS, start, 0)
        lax.fori_loop(0, N_EXPERTS, wait, 0)

        def spare(blk):
            return pltpu.make_async_copy(
                zero_ref, xs_hbm.at[pl.ds(pl.multiple_of(blk * TM_EXPERT, TM_EXPERT), TM_EXPERT)], zero_sem)

        used = seg_end_ref[N_EXPERTS - 1] // TM_EXPERT
        total = xs_hbm.shape[0] // TM_EXPERT
        lax.fori_loop(used, total, lambda blk, carry: (spare(blk).start(), carry)[1], 0)
        lax.fori_loop(used, total, lambda blk, carry: (spare(blk).wait(), carry)[1], 0)

    def start(t, carry):
        for kk in range(TOP_K):
            _row_copy(h_ref, t, xs_hbm, dest_ref[0, 0, TOP_K * t + kk], sem).start(priority=kk % 2)
        return carry

    lax.fori_loop(0, tm, start, 0)

    def wait(t, carry):
        for kk in range(TOP_K):
            _row_copy(h_ref, 0, xs_hbm, 0, sem).wait()
        return carry

    lax.fori_loop(0, tm, wait, 0)


def _dispatch(h, dest, seg_end, padded, cap):
    n, w = h.shape
    tm = TM_MOVE
    dest3 = dest.reshape(n // tm, 1, tm * TOP_K)
    return pl.pallas_call(
        _dispatch_kernel,
        grid_spec=pltpu.PrefetchScalarGridSpec(
            num_scalar_prefetch=2,
            grid=(n // tm,),
            in_specs=[
                pl.BlockSpec((1, 1, tm * TOP_K), lambda i, se, pd: (i, 0, 0), memory_space=pltpu.SMEM),
                pl.BlockSpec((tm, w), lambda i, se, pd: (i, 0)),
            ],
            out_specs=pl.BlockSpec(memory_space=pl.ANY),
            scratch_shapes=[pltpu.VMEM((TM_EXPERT, w), h.dtype), pltpu.SemaphoreType.DMA(()),
                            pltpu.SemaphoreType.DMA(())],
        ),
        out_shape=jax.ShapeDtypeStruct((cap, w), h.dtype),
        compiler_params=_params("arbitrary"),
        name="moe_dispatch",
    )(seg_end, padded, dest3, h)


def _expert_kernel(be_ref, nxt_ref, na_ref, x_ref, wgu_hbm, bgu_ref, wd_hbm, bd_ref, o_ref,
                   gu_stage, d_stage, gu_bf, d_bf, sems, *, layer):
    i = pl.program_id(0)
    live = i < na_ref[0]
    e = be_ref[i]

    def fetch(expert):
        return (pltpu.make_async_copy(wgu_hbm.at[layer, expert], gu_stage, sems.at[0]),
                pltpu.make_async_copy(wd_hbm.at[layer, expert], d_stage, sems.at[1]))

    @pl.when(i == 0)
    def _():
        for c in fetch(e):
            c.start()

    @pl.when(live & ((i == 0) | (e != be_ref[jnp.maximum(i - 1, 0)])))
    def _():
        for c in fetch(e):
            c.wait()
        gu_bf[...] = gu_stage[...].astype(BF16)
        d_bf[...] = d_stage[...].astype(BF16)

        @pl.when(nxt_ref[i] >= 0)
        def _():
            for c in fetch(nxt_ref[i]):
                c.start()

    @pl.when(jnp.logical_not(live))
    def _():
        o_ref[...] = jnp.zeros_like(o_ref)

    @pl.when(live)
    def _():
        f = d_bf.shape[0]
        xb = _unpack_pairs(x_ref[...]).astype(BF16)
        gu = jnp.dot(xb, gu_bf[...], preferred_element_type=F32) + bgu_ref[0]
        gate = jnp.minimum(gu[:, :f], SWIGLU_LIMIT)
        up = jnp.clip(gu[:, f:], -SWIGLU_LIMIT, SWIGLU_LIMIT)
        act = (up + 1.0) * (gate * jax.nn.sigmoid(SWIGLU_ALPHA * gate))
        o_ref[...] = _pack_pairs(jnp.dot(act.astype(BF16), d_bf[...], preferred_element_type=F32) + bd_ref[0])


def _experts(xs, block_e, next_e, n_active, w_gu, b_gu, w_down, b_down, layer):
    cap, w = xs.shape
    _, e, d, f2 = w_gu.shape
    f = f2 // 2
    tm = TM_EXPERT
    nb = cap // tm
    return pl.pallas_call(
        functools.partial(_expert_kernel, layer=layer),
        grid_spec=pltpu.PrefetchScalarGridSpec(
            num_scalar_prefetch=3,
            grid=(nb,),
            in_specs=[
                pl.BlockSpec((tm, w), lambda i, be, nx, na: (jnp.minimum(i, na[0] - 1), 0)),
                pl.BlockSpec(memory_space=pl.ANY),
                pl.BlockSpec((1, 1, f2), lambda i, be, nx, na: (be[i], 0, 0)),
                pl.BlockSpec(memory_space=pl.ANY),
                pl.BlockSpec((1, 1, d), lambda i, be, nx, na: (be[i], 0, 0)),
            ],
            out_specs=pl.BlockSpec((tm, w), lambda i, be, nx, na: (i, 0)),
            scratch_shapes=[pltpu.VMEM((d, f2), F32), pltpu.VMEM((f, d), F32), pltpu.VMEM((d, f2), BF16),
                            pltpu.VMEM((f, d), BF16), pltpu.SemaphoreType.DMA((2,))],
        ),
        out_shape=jax.ShapeDtypeStruct((cap, w), xs.dtype),
        compiler_params=_params("arbitrary"),
        name="moe_experts",
    )(block_e, next_e, n_active, xs, w_gu, b_gu.reshape(e, 1, f2), w_down, b_down.reshape(e, 1, d))


def _combine_kernel(dest_ref, info_ref, x_ref, mod_ref, ys_hbm, out_ref, buf, sem):
    tm = x_ref.shape[0]

    def start(t, carry):
        for kk in range(TOP_K):
            _row_copy(ys_hbm, dest_ref[0, 0, TOP_K * t + kk], buf.at[kk], t, sem).start(priority=kk % 2)
        return carry

    lax.fori_loop(0, tm, start, 0)

    def wait(t, carry):
        for kk in range(TOP_K):
            _row_copy(ys_hbm, 0, buf.at[kk], 0, sem).wait()
        return carry

    lax.fori_loop(0, tm, wait, 0)

    info = info_ref[...]
    y = None
    for kk in range(TOP_K):
        term = info[:, 2 * TOP_K + kk:2 * TOP_K + kk + 1] * _unpack_pairs(buf[kk])
        y = term if y is None else y + term
    out_ref[...] = x_ref[...] + mod_ref[0, 5:6, :] * y


def _combine(ys, dest, info, x2, mod, seq):
    n, d = x2.shape
    tm = TM_MOVE
    per_b = seq // tm
    dest3 = dest.reshape(n // tm, 1, tm * TOP_K)
    return pl.pallas_call(
        _combine_kernel,
        grid=(n // tm,),
        in_specs=[
            pl.BlockSpec((1, 1, tm * TOP_K), lambda i: (i, 0, 0), memory_space=pltpu.SMEM),
            pl.BlockSpec((tm, LANES), lambda i: (i, 0)),
            pl.BlockSpec((tm, d), lambda i: (i, 0)),
            pl.BlockSpec((1, 6, d), lambda i: (i // per_b, 0, 0)),
            pl.BlockSpec(memory_space=pl.ANY),
        ],
        out_specs=pl.BlockSpec((tm, d), lambda i: (i, 0)),
        out_shape=jax.ShapeDtypeStruct((n, d), F32),
        scratch_shapes=[pltpu.VMEM((TOP_K, tm, ys.shape[1]), ys.dtype), pltpu.SemaphoreType.DMA(())],
        compiler_params=_params("arbitrary"),
        name="moe_combine",
    )(dest3, info, x2, mod, ys)


def _moe(x2, mod, g2, router_w, router_b, w_gu, b_gu, w_down, b_down, layer, seq):
    n, d = x2.shape
    h, info, cnt = _router(x2, mod, g2, router_w, router_b, seq)
    counts = cnt[0, :N_EXPERTS].astype(jnp.int32)
    padded = (counts + TM_EXPERT - 1) // TM_EXPERT * TM_EXPERT
    seg_end = jnp.cumsum(padded)
    seg_start = seg_end - padded
    cap = n * TOP_K + N_EXPERTS * TM_EXPERT
    nb = cap // TM_EXPERT
    block_start = jnp.arange(nb, dtype=jnp.int32) * TM_EXPERT
    block_e = jnp.minimum(jnp.sum(seg_end[None, :] <= block_start[:, None], axis=1), N_EXPERTS - 1).astype(jnp.int32)
    n_active = (seg_end[-1:] // TM_EXPERT).astype(jnp.int32)
    expert = info[:, :TOP_K].astype(jnp.int32)
    rank = info[:, TOP_K:2 * TOP_K].astype(jnp.int32)
    dest = (seg_start[expert] + rank).reshape(-1)
    ids = jnp.arange(N_EXPERTS, dtype=jnp.int32)
    later = (ids[None, :] > ids[:, None]) & (padded[None, :] > 0)
    next_of = jnp.min(jnp.where(later, ids[None, :], N_EXPERTS), axis=1)
    next_e = jnp.where(next_of < N_EXPERTS, next_of, -1).astype(jnp.int32)[block_e]
    xs = _dispatch(h, dest, seg_end.astype(jnp.int32), padded.astype(jnp.int32), cap)
    ys = _experts(xs, block_e, next_e, n_active, w_gu, b_gu, w_down, b_down, layer)
    return _combine(ys, dest, info, x2, mod, seq)


def kernel(x, c, positions, ada_w, ada_b, norm1_g, norm2_g, a_w_in, a_q_gain, a_k_gain, a_w_out, b_w_in,
           b_w_gate_up, b_gate_bias, b_out_gain, b_w_out, router_w, router_b, moe_w_gu, moe_b_gu, moe_w_down,
           moe_b_down):
    batch, seq, d = x.shape
    depth = ada_w.shape[0]
    n = batch * seq
    mods = _adaln(c, ada_w, ada_b).reshape(depth, batch, 6, d)
    x2 = x.reshape(n, d)
    for layer in range(depth):
        mod = mods[layer]
        j = layer // 2
        g1 = norm1_g[layer].reshape(1, d)
        g2 = norm2_g[layer].reshape(1, d)
        if layer % 2 == 0:
            qkv = _qkv_proj(x2, mod, g1, a_w_in[j], a_q_gain[j], a_k_gain[j], positions, batch, seq)
            outs, lses = zip(*[_dilated_attention(*qkv[g], g) for g in range(N_GROUPS)])
            x2 = _merge_proj(outs, lses, x2, mod, a_w_out[j].astype(BF16), seq)
        else:
            hk = HEADS_B * KEY_DIM_B
            hv = HEADS_B * VAL_DIM_B
            width = 2 * hk + 2 * hv
            w_in = jnp.zeros((d, width + LANES), BF16).at[:, :width + GATE_RANK].set(b_w_in[j].astype(BF16))
            wg = jnp.zeros((LANES, hk), BF16).at[:GATE_RANK].set(b_w_gate_up[j].astype(BF16))
            proj = _gla_in_proj(x2, mod, g1, w_in, seq)
            x3 = _gla(proj, x2.reshape(batch, seq, d), mod, wg, b_gate_bias[j].reshape(1, hk),
                      b_out_gain[j].reshape(1, VAL_DIM_B), b_w_out[j].astype(BF16))
            x2 = x3.reshape(n, d)
        x2 = _moe(x2, mod, g2, router_w[layer], router_b[layer], moe_w_gu, moe_b_gu[layer],
                  moe_w_down, moe_b_down[layer], layer, seq)
    return x2.reshape(batch, seq, d)
```

```python
import functools

import jax
import jax.numpy as jnp
from jax import lax
from jax.experimental import pallas as pl
from jax.experimental.pallas import tpu as pltpu

F32 = jnp.float32
BF16 = jnp.bfloat16
HIGHEST = lax.Precision.HIGHEST

RMS_EPS = 1e-6
DILATIONS = ((128, 1), (512, 4), (2048, 16))
N_GROUPS = 3
HEADS_A = 16
HEAD_DIM_A = 64
BAND = 128
ROPE_THETA = 10000.0
NEG_INF = -1e30
HEADS_B = 4
KEY_DIM_B = 128
VAL_DIM_B = 256
GATE_RANK = 16
GATE_TAU = 16.0
GLA_CHUNK = 64
N_EXPERTS = 32
TOP_K = 4
SWIGLU_LIMIT = 7.0
SWIGLU_ALPHA = 1.702

LANES = 128
VMEM_LIMIT = 56 * 1024 * 1024

TM_PROJ = 512
TM_QKV = 256
TM_ROUTE = 512
TM_EXPERT = 512
TM_MOVE = 512
ATTN_ROWS = 256
RUN_ALIGN = 8
RUN_ROWS = TOP_K * TM_MOVE + N_EXPERTS * RUN_ALIGN
T_GLA = 512


def _params(*sem):
    return pltpu.CompilerParams(dimension_semantics=sem, vmem_limit_bytes=VMEM_LIMIT)


def _norm_mod(x, gain, shift, scale):
    ms = jnp.mean(x * x, axis=-1, keepdims=True)
    y = x * lax.rsqrt(ms + RMS_EPS) * gain
    return y * (1.0 + scale) + shift


def _pack_pairs(v):
    k = v.shape[1] // 2
    bits = lax.bitcast_convert_type(v.astype(BF16).astype(F32), jnp.uint32)
    return (bits[:, :k] >> 16) | bits[:, k:]


def _unpack_pairs(w):
    lo = lax.bitcast_convert_type(w << 16, F32)
    hi = lax.bitcast_convert_type(w & jnp.uint32(0xFFFF0000), F32)
    return jnp.concatenate([lo, hi], axis=1)


def _adaln_kernel(c_ref, w_ref, b_ref, o_ref):
    c = c_ref[...]
    cond = c * jax.nn.sigmoid(c)
    o_ref[0] = jnp.dot(cond, w_ref[0], precision=HIGHEST, preferred_element_type=F32) + b_ref[0]


def _adaln(c, ada_w, ada_b):
    depth, d, n6 = ada_w.shape
    b = c.shape[0]
    tn = 1536
    return pl.pallas_call(
        _adaln_kernel,
        grid=(depth, n6 // tn),
        in_specs=[
            pl.BlockSpec((b, d), lambda l, j: (0, 0)),
            pl.BlockSpec((1, d, tn), lambda l, j: (l, 0, j)),
            pl.BlockSpec((1, 1, tn), lambda l, j: (l, 0, j)),
        ],
        out_specs=pl.BlockSpec((1, b, tn), lambda l, j: (l, 0, j)),
        out_shape=jax.ShapeDtypeStruct((depth, b, n6), F32),
        compiler_params=_params("parallel", "parallel"),
        name="adaln",
    )(c, ada_w, ada_b.reshape(depth, 1, n6))


def _rope_kernel(pos_ref, cos_ref, sin_ref):
    pos = pos_ref[...].astype(F32)
    lane = lax.broadcasted_iota(jnp.int32, (1, LANES), 1)
    half = HEAD_DIM_A // 2
    fidx = (lane & (half - 1)).astype(F32)
    inv_freq = jnp.power(jnp.full((1, LANES), ROPE_THETA, F32), -(fidx / half))
    ang = pos * inv_freq
    cos_ref[...] = jnp.cos(ang)
    s = jnp.sin(ang)
    sin_ref[...] = jnp.where(lane < LANES // 2, -s, s)


def _rope_tables(positions):
    n = positions.size
    tm = 1024
    return pl.pallas_call(
        _rope_kernel,
        grid=(n // tm,),
        in_specs=[pl.BlockSpec((tm, 1), lambda i: (i, 0))],
        out_specs=[pl.BlockSpec((tm, LANES), lambda i: (i, 0))] * 2,
        out_shape=[jax.ShapeDtypeStruct((n, LANES), F32)] * 2,
        compiler_params=_params("parallel"),
        name="rope_tables",
    )(positions.reshape(n, 1))


def _stream_order(tm, dil):
    s = jnp.arange(tm)
    return (s % (tm // dil)) * dil + s // (tm // dil)


def _qkv_kernel(x_ref, mod_ref, g_ref, w_hbm, gain_ref, cos_ref, sin_ref, bd_ref, p1_ref, p2_ref, *rest):
    o_refs, (w_ref, w_sem) = rest[:3 * N_GROUPS], rest[3 * N_GROUPS:]
    tm, d = x_ref.shape

    @pl.when(pl.program_id(0) == 0)
    def _():
        copy = pltpu.make_async_copy(w_hbm, w_ref, w_sem)
        copy.start()
        copy.wait()

    h = _norm_mod(x_ref[...], g_ref[...], mod_ref[0, 0:1, :], mod_ref[0, 1:2, :]).astype(BF16)
    hs = [h] + [jnp.dot(p[...], h, preferred_element_type=F32).astype(BF16) for p in (p1_ref, p2_ref)]
    for jj in range(3 * N_GROUPS):
        g, comp = divmod(jj, 3)
        dil = DILATIONS[g][1]
        rows = tm // dil
        out = o_refs[jj]
        acc = jnp.dot(hs[g], w_ref[:, jj * d:(jj + 1) * d], preferred_element_type=F32)
        if comp == 2:
            val = acc.astype(BF16)
            for r in range(dil):
                out[0, r] = val[r * rows:(r + 1) * rows]
            continue
        gain = gain_ref[jj]
        cos = cos_ref[g]
        sin = sin_ref[g]
        for cc in range(d // 256):
            a = acc[:, cc * 256:(cc + 1) * 256]
            ss = jnp.dot((a * a).astype(BF16), bd_ref[...], preferred_element_type=F32)
            qn = a * lax.rsqrt(ss * (1.0 / HEAD_DIM_A) + RMS_EPS) * gain[:, cc * 256:(cc + 1) * 256]
            for hh in range(2):
                xx = qn[:, hh * LANES:(hh + 1) * LANES]
                val = (xx * cos + pltpu.roll(xx, 64, 1) * sin).astype(BF16)
                c1 = cc * 256 + hh * LANES
                for r in range(dil):
                    out[0, r, :, c1:c1 + LANES] = val[r * rows:(r + 1) * rows]


def _split_half_columns(w):
    lead = w.shape[:-1]
    w = w.reshape(*lead, HEADS_A // 2, 2, 2, HEAD_DIM_A // 2)
    return jnp.swapaxes(w, -3, -2).reshape(*lead, HEADS_A * HEAD_DIM_A)


def _qkv_proj(x2, mod, g1, w_in, q_gain, k_gain, positions, batch, seq):
    n, d = x2.shape
    tm = TM_QKV
    per_b = seq // tm
    reps = d // HEAD_DIM_A
    w = w_in.reshape(d, N_GROUPS, 3, d)
    w = jnp.concatenate([_split_half_columns(w[:, :, :2]), w[:, :, 2:]], axis=2).reshape(d, 3 * N_GROUPS * d)
    gains = []
    for g in range(N_GROUPS):
        gains += [_split_half_columns(jnp.tile(q_gain[g], reps)) * (HEAD_DIM_A ** -0.5),
                  _split_half_columns(jnp.tile(k_gain[g], reps)), jnp.ones((d,), F32)]
    gains = jnp.stack(gains).reshape(3 * N_GROUPS, 1, d)
    orders = [_stream_order(tm, dil) for _, dil in DILATIONS]
    pos_tiles = positions.reshape(n // tm, tm)
    cos_t, sin_t = _rope_tables(jnp.stack([pos_tiles[:, o] for o in orders]))
    cos_t, sin_t = cos_t.reshape(N_GROUPS, n, LANES), sin_t.reshape(N_GROUPS, n, LANES)
    perms = [(o[:, None] == jnp.arange(tm)[None, :]).astype(BF16) for o in orders[1:]]
    lane = jnp.arange(256)
    head_of = (lane // LANES) * 2 + (lane // 32) % 2
    bd = (head_of[:, None] == head_of[None, :]).astype(BF16)
    dils = [DILATIONS[jj // 3][1] for jj in range(3 * N_GROUPS)]
    const = lambda shape: pl.BlockSpec(shape, lambda i: (0,) * len(shape))
    flat = pl.pallas_call(
        _qkv_kernel,
        grid=(n // tm,),
        in_specs=[
            pl.BlockSpec((tm, d), lambda i: (i, 0)),
            pl.BlockSpec((1, 6, d), lambda i: (i // per_b, 0, 0)),
            const((1, d)),
            pl.BlockSpec(memory_space=pl.ANY),
            const((3 * N_GROUPS, 1, d)),
            pl.BlockSpec((N_GROUPS, tm, LANES), lambda i: (0, i, 0)),
            pl.BlockSpec((N_GROUPS, tm, LANES), lambda i: (0, i, 0)),
            const((256, 256)),
            const((tm, tm)),
            const((tm, tm)),
        ],
        out_specs=[pl.BlockSpec((1, dil, tm // dil, d), lambda i: (i // per_b, 0, i % per_b, 0)) for dil in dils],
        out_shape=[jax.ShapeDtypeStruct((batch, dil, seq // dil, d), BF16) for dil in dils],
        scratch_shapes=[pltpu.VMEM((d, 3 * N_GROUPS * d), BF16), pltpu.SemaphoreType.DMA(())],
        compiler_params=_params("arbitrary"),
        name="qkv_proj",
    )(x2, mod, g1, w.astype(BF16), gains, cos_t, sin_t, bd, *perms)
    return [flat[3 * g:3 * g + 3] for g in range(N_GROUPS)]


def _attn_kernel(q_ref, kp_ref, kc_ref, vp_ref, vc_ref, o_ref, lse_ref, *, steps):
    q_ref, kp_ref, kc_ref, vp_ref, vc_ref, o_ref, lse_ref = (
        r.at[0, 0] for r in (q_ref, kp_ref, kc_ref, vp_ref, vc_ref, o_ref, lse_ref))
    n = pl.program_id(2)
    row = lax.broadcasted_iota(jnp.int32, (BAND, 2 * BAND), 0)
    col = lax.broadcasted_iota(jnp.int32, (BAND, 2 * BAND), 1)
    delta = row + BAND - col
    in_band = (delta >= 0) & (delta <= steps)
    lane = lax.broadcasted_iota(jnp.int32, (BAND, LANES), 1)
    first_head = lane < HEAD_DIM_A
    even_head = (lane & (HEAD_DIM_A // 2)) == 0
    for sub in range(q_ref.shape[0] // BAND):
        rows = slice(sub * BAND, (sub + 1) * BAND)
        if sub == 0:
            k_prev, v_prev = kp_ref, vp_ref
            valid = in_band & ((col >= BAND) | (n > 0))
        else:
            k_prev, v_prev = kc_ref.at[(sub - 1) * BAND:sub * BAND], vc_ref.at[(sub - 1) * BAND:sub * BAND]
            valid = in_band
        lse_tile = jnp.zeros((BAND, LANES), F32)
        for p in range(HEADS_A // 2):
            sl = slice(p * LANES, (p + 1) * LANES)
            qp = q_ref[rows, sl]
            kcat = jnp.concatenate([k_prev[:, sl], kc_ref[rows, sl]], axis=0)
            vcat = jnp.concatenate([v_prev[:, sl], vc_ref[rows, sl]], axis=0)
            outs = []
            for hh in range(2):
                keep = even_head if hh == 0 else jnp.logical_not(even_head)
                qm = jnp.where(keep, qp, jnp.zeros_like(qp))
                s = lax.dot_general(qm, kcat, (((1,), (1,)), ((), ())), preferred_element_type=F32)
                s = jnp.where(valid, s, NEG_INF)
                m = jnp.max(s, axis=-1, keepdims=True)
                pexp = jnp.exp(s - m)
                den = jnp.sum(pexp, axis=-1, keepdims=True)
                o = jnp.dot(pexp.astype(BF16), vcat, preferred_element_type=F32) / den
                outs.append(o)
                lse_tile = jnp.where(lane == (2 * p + hh), m + jnp.log(den), lse_tile)
            o_ref[rows, sl] = jnp.where(first_head, outs[0], outs[1]).astype(BF16)
        lse_ref[rows, :] = lse_tile


def _dilated_attention(q, k, v, g):
    window, dil = DILATIONS[g]
    batch, _, l, d = q.shape
    qb = ATTN_ROWS
    cur = pl.BlockSpec((1, 1, qb, d), lambda b, r, i: (b, r, i, 0))
    prev = pl.BlockSpec((1, 1, BAND, d), lambda b, r, i: (b, r, jnp.maximum(i * (qb // BAND) - 1, 0), 0))
    return pl.pallas_call(
        functools.partial(_attn_kernel, steps=window // dil),
        grid=(batch, dil, l // qb),
        in_specs=[cur, prev, cur, prev, cur],
        out_specs=[cur, pl.BlockSpec((1, 1, qb, LANES), lambda b, r, i: (b, r, i, 0))],
        out_shape=[
            jax.ShapeDtypeStruct((batch, dil, l, d), BF16),
            jax.ShapeDtypeStruct((batch, dil, l, LANES), F32),
        ],
        compiler_params=_params("parallel", "parallel", "arbitrary"),
        name=f"dilated_attn_g{g}",
    )(q, k, k, v, v)


def _split3(v):
    hi = v.astype(BF16)
    rem = v - hi.astype(F32)
    mid = rem.astype(BF16)
    return hi, mid, (rem - mid.astype(F32)).astype(BF16)


def _dot3(mat_bf, v):
    return sum(jnp.dot(mat_bf, part, preferred_element_type=F32) for part in _split3(v))


def _merge_proj_kernel(o0_ref, o1_ref, o2_ref, l0_ref, l1_ref, l2_ref, p1_ref, p2_ref, x_ref, mod_ref, e_ref,
                       w_ref, out_ref):
    tm, d = x_ref.shape
    perms = (None, p1_ref, p2_ref)
    lses, outs = [], []
    for perm, l_ref, o_ref in zip(perms, (l0_ref, l1_ref, l2_ref), (o0_ref, o1_ref, o2_ref)):
        lse = l_ref[0].reshape(tm, LANES)
        o = o_ref[0].reshape(tm, d)
        if perm is None:
            lses.append(lse)
            outs.append(o.astype(F32))
        else:
            lses.append(_dot3(perm[...], lse))
            outs.append(jnp.dot(perm[...], o, preferred_element_type=F32))
    m = jnp.maximum(jnp.maximum(lses[0], lses[1]), lses[2])
    exps = [jnp.exp(l - m) for l in lses]
    inv = 1.0 / (exps[0] + exps[1] + exps[2])
    acc = None
    for e, o in zip(exps, outs):
        w = e * inv
        w_hi = w.astype(BF16)
        w_lo = (w - w_hi.astype(F32)).astype(BF16)
        wide = (jnp.dot(w_hi, e_ref[...], preferred_element_type=F32)
                + jnp.dot(w_lo, e_ref[...], preferred_element_type=F32))
        acc = wide * o if acc is None else acc + wide * o
    y = jnp.dot(acc.astype(BF16), w_ref[...], preferred_element_type=F32)
    out_ref[...] = x_ref[...] + mod_ref[0, 2:3, :] * y


def _merge_proj(outs, lses, x2, mod, w_out_bf, seq):
    n, d = x2.shape
    tm = TM_PROJ
    per_b = seq // tm
    expand = (jnp.arange(LANES)[:, None] == jnp.arange(d)[None, :] // HEAD_DIM_A).astype(BF16)
    tok = jnp.arange(tm)
    perms = [(((tok % dil) * (tm // dil) + tok // dil)[:, None] == tok[None, :]).astype(BF16)
             for _, dil in DILATIONS[1:]]
    slab = lambda dil, width: pl.BlockSpec((1, dil, tm // dil, width), lambda i: (i // per_b, 0, i % per_b, 0))
    const = lambda shape: pl.BlockSpec(shape, lambda i: (0, 0))
    return pl.pallas_call(
        _merge_proj_kernel,
        grid=(n // tm,),
        in_specs=[slab(dil, d) for _, dil in DILATIONS] + [slab(dil, LANES) for _, dil in DILATIONS] + [
            const((tm, tm)),
            const((tm, tm)),
            pl.BlockSpec((tm, d), lambda i: (i, 0)),
            pl.BlockSpec((1, 6, d), lambda i: (i // per_b, 0, 0)),
            const((LANES, d)),
            const((d, d)),
        ],
        out_specs=pl.BlockSpec((tm, d), lambda i: (i, 0)),
        out_shape=jax.ShapeDtypeStruct((n, d), F32),
        compiler_params=_params("parallel"),
        name="merge_out_proj",
    )(*outs, *lses, *perms, x2, mod, expand, w_out_bf)


def _gla_in_kernel(x_ref, mod_ref, g_ref, w_ref, o_ref):
    h = _norm_mod(x_ref[...], g_ref[...], mod_ref[0, 0:1, :], mod_ref[0, 1:2, :])
    o_ref[...] = jnp.dot(h.astype(BF16), w_ref[...], preferred_element_type=F32).astype(BF16)


def _gla_in_proj(x2, mod, g1, w_bf, seq):
    n, d = x2.shape
    ncol = w_bf.shape[1]
    tm = TM_PROJ
    per_b = seq // tm
    return pl.pallas_call(
        _gla_in_kernel,
        grid=(n // tm,),
        in_specs=[
            pl.BlockSpec((tm, d), lambda i: (i, 0)),
            pl.BlockSpec((1, 6, d), lambda i: (i // per_b, 0, 0)),
            pl.BlockSpec((1, d), lambda i: (0, 0)),
            pl.BlockSpec((d, ncol), lambda i: (0, 0)),
        ],
        out_specs=pl.BlockSpec((tm, ncol), lambda i: (i, 0)),
        out_shape=jax.ShapeDtypeStruct((n, ncol), BF16),
        compiler_params=_params("parallel"),
        name="gla_in_proj",
    )(x2, mod, g1, w_bf)


def _gla_kernel(q_ref, k_ref, v_ref, r_ref, a_ref, wg_ref, gb_ref, og_ref, tri_ref, x_ref, mod_ref, wo_ref,
                out_ref, st_ref, o_scr, la_scr):
    t = pl.program_id(1)
    c = GLA_CHUNK

    @pl.when(t == 0)
    def _():
        st_ref[...] = jnp.zeros_like(st_ref)

    g = jnp.dot(a_ref[0], wg_ref[...], preferred_element_type=F32) + gb_ref[...]
    la_scr[...] = (jnp.minimum(g, 0.0) - jnp.log(1.0 + jnp.exp(-jnp.abs(g)))) * (1.0 / GATE_TAU)

    rr = lax.broadcasted_iota(jnp.int32, (c, c), 0)
    cc = lax.broadcasted_iota(jnp.int32, (c, c), 1)
    causal = cc <= rr

    def chunk(ci, carry):
        rows = pl.ds(pl.multiple_of(ci * c, c), c)
        la = la_scr[rows, :]
        b = _dot3(tri_ref[...], la)
        b_last = b[c - 1:c, :]
        q = q_ref[0, rows, :].astype(F32) * (KEY_DIM_B ** -0.5)
        k = k_ref[0, rows, :].astype(F32)
        q_dec = (q * jnp.exp(b)).astype(BF16)
        k_inv = (k * jnp.exp(-b)).astype(BF16)
        k_dec = (k * jnp.exp(b_last - b)).astype(BF16)
        decay = jnp.exp(b_last)
        v = v_ref[0, rows, :]
        for h in range(HEADS_B):
            ks = slice(h * KEY_DIM_B, (h + 1) * KEY_DIM_B)
            vs = slice(h * VAL_DIM_B, (h + 1) * VAL_DIM_B)
            qd, ki, kd, vh = q_dec[:, ks], k_inv[:, ks], k_dec[:, ks], v[:, vs]
            att = lax.dot_general(qd, ki, (((1,), (1,)), ((), ())), preferred_element_type=F32)
            att = jnp.where(causal, att, 0.0)
            st = st_ref[h]
            o = (jnp.dot(att.astype(BF16), vh, preferred_element_type=F32)
                 + lax.dot_general(qd, st.astype(BF16), (((1,), (1,)), ((), ())), preferred_element_type=F32))
            o_scr[rows, vs] = o
            st_ref[h] = st * decay[:, ks] + lax.dot_general(vh, kd, (((0,), (0,)), ((), ())),
                                                             preferred_element_type=F32)
        return carry

    lax.fori_loop(0, q_ref.shape[1] // c, chunk, 0, unroll=2)

    r = r_ref[0].astype(F32)
    parts = []
    for h in range(HEADS_B):
        vs = slice(h * VAL_DIM_B, (h + 1) * VAL_DIM_B)
        oh = o_scr[:, vs]
        ms = jnp.mean(oh * oh, axis=-1, keepdims=True)
        rh = r[:, vs]
        parts.append((oh * lax.rsqrt(ms + RMS_EPS) * og_ref[...] * (rh * jax.nn.sigmoid(rh))).astype(BF16))
    y = jnp.dot(jnp.concatenate(parts, axis=1), wo_ref[...], preferred_element_type=F32)
    out_ref[0] = x_ref[0] + mod_ref[0, 2:3, :] * y


def _gla(proj, x3, mod, wg_bf, gate_bias, out_gain, w_out_bf):
    batch, seq, d = x3.shape
    ncol = proj.shape[1]
    t = T_GLA
    hk = HEADS_B * KEY_DIM_B
    hv = HEADS_B * VAL_DIM_B
    p3 = proj.reshape(batch, seq, ncol)
    tri = (jnp.arange(GLA_CHUNK)[None, :] <= jnp.arange(GLA_CHUNK)[:, None]).astype(BF16)
    a_blk = (2 * hk + 2 * hv) // LANES
    const = lambda shape: pl.BlockSpec(shape, lambda b, i: (0,) * len(shape))
    return pl.pallas_call(
        _gla_kernel,
        grid=(batch, seq // t),
        in_specs=[
            pl.BlockSpec((1, t, hk), lambda b, i: (b, i, 0)),
            pl.BlockSpec((1, t, hk), lambda b, i: (b, i, 1)),
            pl.BlockSpec((1, t, hv), lambda b, i: (b, i, (2 * hk) // hv)),
            pl.BlockSpec((1, t, hv), lambda b, i: (b, i, (2 * hk + hv) // hv)),
            pl.BlockSpec((1, t, LANES), lambda b, i: (b, i, a_blk)),
            const((LANES, hk)),
            const((1, hk)),
            const((1, VAL_DIM_B)),
            const((GLA_CHUNK, GLA_CHUNK)),
            pl.BlockSpec((1, t, d), lambda b, i: (b, i, 0)),
            pl.BlockSpec((1, 6, d), lambda b, i: (b, 0, 0)),
            const((hv, d)),
        ],
        out_specs=pl.BlockSpec((1, t, d), lambda b, i: (b, i, 0)),
        out_shape=jax.ShapeDtypeStruct((batch, seq, d), F32),
        scratch_shapes=[
            pltpu.VMEM((HEADS_B, VAL_DIM_B, KEY_DIM_B), F32),
            pltpu.VMEM((t, hv), F32),
            pltpu.VMEM((t, hk), F32),
        ],
        compiler_params=_params("parallel", "arbitrary"),
        name="gla",
    )(p3, p3, p3, p3, p3, wg_bf, gate_bias, out_gain, tri, x3, mod, w_out_bf)


def _router_kernel(x_ref, mod_ref, g_ref, rw_ref, rb_ref, tri_ref, h_ref, info_ref, cnt_ref, tcnt_ref, tcar_ref,
                   carry_ref):
    i = pl.program_id(0)

    @pl.when(i == 0)
    def _():
        carry_ref[...] = jnp.zeros_like(carry_ref)

    h = _norm_mod(x_ref[...], g_ref[...], mod_ref[0, 3:4, :], mod_ref[0, 4:5, :])
    h_ref[...] = h.astype(BF16)
    tm = h.shape[0]
    lane = lax.broadcasted_iota(jnp.int32, (tm, LANES), 1)
    lane_f = lane.astype(F32)
    h_hi = h.astype(BF16)
    h_lo = (h - h_hi.astype(F32)).astype(BF16)
    logits = (jnp.dot(h_hi, rw_ref[0], preferred_element_type=F32) + jnp.dot(h_lo, rw_ref[0], preferred_element_type=F32)
              + jnp.dot(h_hi, rw_ref[1], preferred_element_type=F32) + rb_ref[...])
    work = jnp.where(lane < N_EXPERTS, logits, -jnp.inf)
    picks, vals, idxs = [], [], []
    for _ in range(TOP_K):
        m = jnp.max(work, axis=-1, keepdims=True)
        idx = jnp.min(jnp.where(work == m, lane_f, float(LANES)), axis=-1, keepdims=True)
        pick = lane_f == idx
        work = jnp.where(pick, -jnp.inf, work)
        picks.append(pick)
        vals.append(m)
        idxs.append(idx)
    exps = [jnp.exp(v - vals[0]) for v in vals]
    inv = 1.0 / (exps[0] + exps[1] + exps[2] + exps[3])
    chosen = jnp.zeros((tm, LANES), F32)
    for pick in picks:
        chosen = jnp.where(pick, 1.0, chosen)
    before = jnp.dot(tri_ref[...], chosen.astype(BF16), preferred_element_type=F32) + carry_ref[0:1, :]
    info = jnp.zeros((tm, LANES), F32)
    for kk in range(TOP_K):
        rank = jnp.sum(jnp.where(picks[kk], before, 0.0), axis=-1, keepdims=True)
        info = jnp.where(lane == kk, idxs[kk], info)
        info = jnp.where(lane == TOP_K + kk, rank, info)
        info = jnp.where(lane == 2 * TOP_K + kk, exps[kk] * inv, info)
    info_ref[...] = info
    here = jnp.floor((jnp.sum(chosen, axis=0, keepdims=True) + (RUN_ALIGN - 1)) * (1.0 / RUN_ALIGN)) * RUN_ALIGN
    tcnt_ref[0] = jnp.broadcast_to(here, tcnt_ref.shape[1:])
    tcar_ref[0] = carry_ref[...]
    total = carry_ref[0:1, :] + here
    carry_ref[...] = jnp.broadcast_to(total, carry_ref.shape)
    cnt_ref[...] = jnp.broadcast_to(total, cnt_ref.shape)


def _router(x2, mod, g2, router_w, router_b, seq):
    n, d = x2.shape
    tm = TM_ROUTE
    per_b = seq // tm
    rw = jnp.zeros((d, LANES), F32).at[:, :N_EXPERTS].set(router_w)
    rw_hi = rw.astype(BF16)
    rw = jnp.stack([rw_hi, (rw - rw_hi.astype(F32)).astype(BF16)])
    rb = jnp.zeros((1, LANES), F32).at[0, :N_EXPERTS].set(router_b)
    tri = (jnp.arange(tm)[None, :] < jnp.arange(tm)[:, None]).astype(BF16)
    return pl.pallas_call(
        _router_kernel,
        grid=(n // tm,),
        in_specs=[
            pl.BlockSpec((tm, d), lambda i: (i, 0)),
            pl.BlockSpec((1, 6, d), lambda i: (i // per_b, 0, 0)),
            pl.BlockSpec((1, d), lambda i: (0, 0)),
            pl.BlockSpec((2, d, LANES), lambda i: (0, 0, 0)),
            pl.BlockSpec((1, LANES), lambda i: (0, 0)),
            pl.BlockSpec((tm, tm), lambda i: (0, 0)),
        ],
        out_specs=[
            pl.BlockSpec((tm, d), lambda i: (i, 0)),
            pl.BlockSpec((tm, LANES), lambda i: (i, 0)),
            pl.BlockSpec((8, LANES), lambda i: (0, 0)),
            pl.BlockSpec((1, 8, LANES), lambda i: (i, 0, 0)),
            pl.BlockSpec((1, 8, LANES), lambda i: (i, 0, 0)),
        ],
        out_shape=[
            jax.ShapeDtypeStruct((n, d), BF16),
            jax.ShapeDtypeStruct((n, LANES), F32),
            jax.ShapeDtypeStruct((8, LANES), F32),
            jax.ShapeDtypeStruct((n // tm, 8, LANES), F32),
            jax.ShapeDtypeStruct((n // tm, 8, LANES), F32),
        ],
        scratch_shapes=[pltpu.VMEM((8, LANES), F32)],
        compiler_params=_params("arbitrary"),
        name="moe_router",
    )(x2, mod, g2, rw, rb, tri)


def _for_each_run(tile, cnt_ref, lo_ref, base_ref, act):
    bits = [1 << k for k in range(TM_MOVE.bit_length() - 1, RUN_ALIGN.bit_length() - 2, -1)]

    def one(e, carry):
        j = tile * N_EXPERTS + e
        cnt, lo, base = cnt_ref[j], lo_ref[j], base_ref[j]
        off = jnp.int32(0)
        for bit in bits:
            take = (cnt & bit) != 0
            pl.when(take)(functools.partial(act, pl.multiple_of(lo + off, RUN_ALIGN),
                                            pl.multiple_of(base + off, RUN_ALIGN), bit))
            off = off + jnp.where(take, bit, 0)
        return carry

    lax.fori_loop(0, N_EXPERTS, one, 0)


def _dispatch_kernel(seg_end_ref, padded_ref, cnt_ref, lo_ref, base_ref, pos_ref, h_ref, xs_hbm,
                     loc_ref, zero_ref, sem, zero_sem):
    i = pl.program_id(0)
    tm, d = h_ref.shape

    @pl.when(pl.program_id(0) == 0)
    def _():
        zero_ref[...] = jnp.zeros_like(zero_ref)

        def tail(e):
            return pltpu.make_async_copy(
                zero_ref, xs_hbm.at[pl.ds(pl.multiple_of(seg_end_ref[e] - TM_EXPERT, TM_EXPERT), TM_EXPERT)], zero_sem)

        def start(e, carry):
            pl.when(padded_ref[e] > 0)(lambda: tail(e).start())
            return carry

        def wait(e, carry):
            pl.when(padded_ref[e] > 0)(lambda: tail(e).wait())
            return carry

        lax.fori_loop(0, N_EXPERTS, start, 0)
        lax.fori_loop(0, N_EXPERTS, wait, 0)

        def spare(blk):
            return pltpu.make_async_copy(
                zero_ref, xs_hbm.at[pl.ds(pl.multiple_of(blk * TM_EXPERT, TM_EXPERT), TM_EXPERT)], zero_sem)

        used = seg_end_ref[N_EXPERTS - 1] // TM_EXPERT
        total = xs_hbm.shape[0] // TM_EXPERT
        lax.fori_loop(used, total, lambda blk, carry: (spare(blk).start(), carry)[1], 0)
        lax.fori_loop(used, total, lambda blk, carry: (spare(blk).wait(), carry)[1], 0)

    slot = lax.broadcasted_iota(jnp.int32, (RUN_ROWS, tm), 0).astype(F32)
    hit = slot == pos_ref[0:1, :]
    for kk in range(1, TOP_K):
        hit = hit | (slot == pos_ref[kk:kk + 1, :])
    perm = jnp.where(hit, 1.0, 0.0).astype(BF16)
    loc_ref[...] = _pack_pairs(jnp.dot(perm, h_ref[...], preferred_element_type=F32))

    def copy(local_row, global_row, size):
        return pltpu.make_async_copy(loc_ref.at[pl.ds(local_row, size)], xs_hbm.at[pl.ds(global_row, size)], sem)

    _for_each_run(i, cnt_ref, lo_ref, base_ref, lambda a, b, size: copy(a, b, size).start())
    _for_each_run(i, cnt_ref, lo_ref, base_ref, lambda a, b, size: copy(a, b, size).wait())


def _dispatch(h, pos_t, seg_end, padded, tile_cnt, tile_lo, tile_base, cap):
    n, d = h.shape
    tm = TM_MOVE
    smem = lambda i, *_: (0, i)
    return pl.pallas_call(
        _dispatch_kernel,
        grid_spec=pltpu.PrefetchScalarGridSpec(
            num_scalar_prefetch=5,
            grid=(n // tm,),
            in_specs=[
                pl.BlockSpec((8, tm), smem),
                pl.BlockSpec((tm, d), lambda i, *_: (i, 0)),
            ],
            out_specs=pl.BlockSpec(memory_space=pl.ANY),
            scratch_shapes=[pltpu.VMEM((RUN_ROWS, d // 2), jnp.uint32), pltpu.VMEM((TM_EXPERT, d // 2), jnp.uint32),
                            pltpu.SemaphoreType.DMA(()), pltpu.SemaphoreType.DMA(())],
        ),
        out_shape=jax.ShapeDtypeStruct((cap, d // 2), jnp.uint32),
        compiler_params=_params("arbitrary"),
        name="moe_dispatch",
    )(seg_end, padded, tile_cnt, tile_lo, tile_base, pos_t, h)


def _expert_kernel(be_ref, nxt_ref, na_ref, x_ref, wgu_hbm, bgu_ref, wd_hbm, bd_ref, o_ref,
                   gu_stage, d_stage, gu_bf, d_bf, sems, *, layer):
    i = pl.program_id(0)
    live = i < na_ref[0]
    e = be_ref[i]

    def fetch(expert):
        return (pltpu.make_async_copy(wgu_hbm.at[layer, expert], gu_stage, sems.at[0]),
                pltpu.make_async_copy(wd_hbm.at[layer, expert], d_stage, sems.at[1]))

    @pl.when(i == 0)
    def _():
        for c in fetch(e):
            c.start()

    @pl.when(live & ((i == 0) | (e != be_ref[jnp.maximum(i - 1, 0)])))
    def _():
        for c in fetch(e):
            c.wait()
        gu_bf[...] = gu_stage[...].astype(BF16)
        d_bf[...] = d_stage[...].astype(BF16)

        @pl.when(nxt_ref[i] >= 0)
        def _():
            for c in fetch(nxt_ref[i]):
                c.start()

    @pl.when(jnp.logical_not(live))
    def _():
        o_ref[...] = jnp.zeros_like(o_ref)

    @pl.when(live)
    def _():
        f = d_bf.shape[0]
        xb = _unpack_pairs(x_ref[...]).astype(BF16)
        gu = jnp.dot(xb, gu_bf[...], preferred_element_type=F32) + bgu_ref[0]
        gate = jnp.minimum(gu[:, :f], SWIGLU_LIMIT)
        up = jnp.clip(gu[:, f:], -SWIGLU_LIMIT, SWIGLU_LIMIT)
        act = (up + 1.0) * (gate * jax.nn.sigmoid(SWIGLU_ALPHA * gate))
        o_ref[...] = _pack_pairs(jnp.dot(act.astype(BF16), d_bf[...], preferred_element_type=F32) + bd_ref[0])


def _experts(xs, block_e, next_e, n_active, w_gu, b_gu, w_down, b_down, layer):
    cap, w = xs.shape
    _, e, d, f2 = w_gu.shape
    f = f2 // 2
    tm = TM_EXPERT
    nb = cap // tm
    return pl.pallas_call(
        functools.partial(_expert_kernel, layer=layer),
        grid_spec=pltpu.PrefetchScalarGridSpec(
            num_scalar_prefetch=3,
            grid=(nb,),
            in_specs=[
                pl.BlockSpec((tm, w), lambda i, be, nx, na: (jnp.minimum(i, na[0] - 1), 0)),
                pl.BlockSpec(memory_space=pl.ANY),
                pl.BlockSpec((1, 1, f2), lambda i, be, nx, na: (be[i], 0, 0)),
                pl.BlockSpec(memory_space=pl.ANY),
                pl.BlockSpec((1, 1, d), lambda i, be, nx, na: (be[i], 0, 0)),
            ],
            out_specs=pl.BlockSpec((tm, w), lambda i, be, nx, na: (i, 0)),
            scratch_shapes=[pltpu.VMEM((d, f2), F32), pltpu.VMEM((f, d), F32), pltpu.VMEM((d, f2), BF16),
                            pltpu.VMEM((f, d), BF16), pltpu.SemaphoreType.DMA((2,))],
        ),
        out_shape=jax.ShapeDtypeStruct((cap, w), xs.dtype),
        compiler_params=_params("arbitrary"),
        name="moe_experts",
    )(block_e, next_e, n_active, xs, w_gu, b_gu.reshape(e, 1, f2), w_down, b_down.reshape(e, 1, d))


def _combine_kernel(cnt_ref, lo_ref, base_ref, info_ref, x_ref, mod_ref, ys_hbm, out_ref, loc_ref, sem):
    i = pl.program_id(0)
    tm = x_ref.shape[0]

    @pl.when(i == 0)
    def _():
        loc_ref[...] = jnp.zeros_like(loc_ref)

    def copy(local_row, global_row, size):
        return pltpu.make_async_copy(ys_hbm.at[pl.ds(global_row, size)], loc_ref.at[pl.ds(local_row, size)], sem)

    _for_each_run(i, cnt_ref, lo_ref, base_ref, lambda a, b, size: copy(a, b, size).start())
    _for_each_run(i, cnt_ref, lo_ref, base_ref, lambda a, b, size: copy(a, b, size).wait())

    info = info_ref[...]
    slot = lax.broadcasted_iota(jnp.int32, (tm, RUN_ROWS), 1).astype(F32)
    mix = jnp.zeros((tm, RUN_ROWS), F32)
    for kk in range(TOP_K):
        mix = jnp.where(slot == info[:, TOP_K + kk:TOP_K + kk + 1], info[:, 2 * TOP_K + kk:2 * TOP_K + kk + 1], mix)
    rows = _unpack_pairs(loc_ref[...]).astype(BF16)
    y = jnp.dot(mix.astype(BF16), rows, preferred_element_type=F32)
    out_ref[...] = x_ref[...] + mod_ref[0, 5:6, :] * y


def _combine(ys, info, tile_cnt, tile_lo, tile_base, x2, mod, seq):
    n, d = x2.shape
    tm = TM_MOVE
    per_b = seq // tm
    return pl.pallas_call(
        _combine_kernel,
        grid_spec=pltpu.PrefetchScalarGridSpec(
            num_scalar_prefetch=3,
            grid=(n // tm,),
            in_specs=[
                pl.BlockSpec((tm, LANES), lambda i, *_: (i, 0)),
                pl.BlockSpec((tm, d), lambda i, *_: (i, 0)),
                pl.BlockSpec((1, 6, d), lambda i, *_: (i // per_b, 0, 0)),
                pl.BlockSpec(memory_space=pl.ANY),
            ],
            out_specs=pl.BlockSpec((tm, d), lambda i, *_: (i, 0)),
            scratch_shapes=[pltpu.VMEM((RUN_ROWS, ys.shape[1]), ys.dtype), pltpu.SemaphoreType.DMA(())],
        ),
        out_shape=jax.ShapeDtypeStruct((n, d), F32),
        compiler_params=_params("arbitrary"),
        name="moe_combine",
    )(tile_cnt, tile_lo, tile_base, info, x2, mod, ys)


def _moe(x2, mod, g2, router_w, router_b, w_gu, b_gu, w_down, b_down, layer, seq):
    n, d = x2.shape
    assert TM_MOVE == TM_ROUTE, "dispatch / combine tiles reuse the router's per-tile expert counts"
    h, info, cnt, tcnt, tcar = _router(x2, mod, g2, router_w, router_b, seq)
    counts = cnt[0, :N_EXPERTS].astype(jnp.int32)
    padded = (counts + TM_EXPERT - 1) // TM_EXPERT * TM_EXPERT
    seg_end = jnp.cumsum(padded)
    seg_start = seg_end - padded
    worst = n * TOP_K + N_EXPERTS * (n // TM_MOVE) * (RUN_ALIGN - 1)
    nb = -(-worst // TM_EXPERT) + N_EXPERTS
    cap = nb * TM_EXPERT
    block_start = jnp.arange(nb, dtype=jnp.int32) * TM_EXPERT
    block_e = jnp.minimum(jnp.sum(seg_end[None, :] <= block_start[:, None], axis=1), N_EXPERTS - 1).astype(jnp.int32)
    n_active = (seg_end[-1:] // TM_EXPERT).astype(jnp.int32)
    tile_cnt = tcnt[:, 0, :N_EXPERTS].astype(jnp.int32)
    tile_before = tcar[:, 0, :N_EXPERTS].astype(jnp.int32)
    tile_lo = jnp.cumsum(tile_cnt, axis=1) - tile_cnt
    tile_base = seg_start[None, :] + tile_before
    expert = info[:, :TOP_K].astype(jnp.int32)
    rank = info[:, TOP_K:2 * TOP_K].astype(jnp.int32)
    tile_of = jnp.arange(n, dtype=jnp.int32)[:, None] // TM_MOVE
    pos = ((tile_lo - tile_before)[tile_of, expert] + rank).astype(F32)
    pos_t = jnp.zeros((8, n), F32).at[:TOP_K].set(pos.T)
    info = jnp.concatenate([info[:, :TOP_K], pos, info[:, 2 * TOP_K:]], axis=1)
    tile_cnt, tile_lo, tile_base = (t.reshape(-1).astype(jnp.int32) for t in (tile_cnt, tile_lo, tile_base))
    ids = jnp.arange(N_EXPERTS, dtype=jnp.int32)
    later = (ids[None, :] > ids[:, None]) & (padded[None, :] > 0)
    next_of = jnp.min(jnp.where(later, ids[None, :], N_EXPERTS), axis=1)
    next_e = jnp.where(next_of < N_EXPERTS, next_of, -1).astype(jnp.int32)[block_e]
    xs = _dispatch(h, pos_t, seg_end.astype(jnp.int32), padded.astype(jnp.int32), tile_cnt, tile_lo, tile_base, cap)
    ys = _experts(xs, block_e, next_e, n_active, w_gu, b_gu, w_down, b_down, layer)
    return _combine(ys, info, tile_cnt, tile_lo, tile_base, x2, mod, seq)


def kernel(x, c, positions, ada_w, ada_b, norm1_g, norm2_g, a_w_in, a_q_gain, a_k_gain, a_w_out, b_w_in,
           b_w_gate_up, b_gate_bias, b_out_gain, b_w_out, router_w, router_b, moe_w_gu, moe_b_gu, moe_w_down,
           moe_b_down):
    batch, seq, d = x.shape
    depth = ada_w.shape[0]
    n = batch * seq
    mods = _adaln(c, ada_w, ada_b).reshape(depth, batch, 6, d)
    x2 = x.reshape(n, d)
    for layer in range(depth):
        mod = mods[layer]
        j = layer // 2
        g1 = norm1_g[layer].reshape(1, d)
        g2 = norm2_g[layer].reshape(1, d)
        if layer % 2 == 0:
            qkv = _qkv_proj(x2, mod, g1, a_w_in[j], a_q_gain[j], a_k_gain[j], positions, batch, seq)
            outs, lses = zip(*[_dilated_attention(*qkv[g], g) for g in range(N_GROUPS)])
            x2 = _merge_proj(outs, lses, x2, mod, a_w_out[j].astype(BF16), seq)
        else:
            hk = HEADS_B * KEY_DIM_B
            hv = HEADS_B * VAL_DIM_B
            width = 2 * hk + 2 * hv
            w_in = jnp.zeros((d, width + LANES), BF16).at[:, :width + GATE_RANK].set(b_w_in[j].astype(BF16))
            wg = jnp.zeros((LANES, hk), BF16).at[:GATE_RANK].set(b_w_gate_up[j].astype(BF16))
            proj = _gla_in_proj(x2, mod, g1, w_in, seq)
            x3 = _gla(proj, x2.reshape(batch, seq, d), mod, wg, b_gate_bias[j].reshape(1, hk),
                      b_out_gain[j].reshape(1, VAL_DIM_B), b_w_out[j].astype(BF16))
            x2 = x3.reshape(n, d)
        x2 = _moe(x2, mod, g2, router_w[layer], router_b[layer], moe_w_gu, moe_b_gu[layer],
                  moe_w_down, moe_b_down[layer], layer, seq)
    return x2.reshape(batch, seq, d)
```

```python
import functools

import jax
import jax.numpy as jnp
from jax import lax
from jax.experimental import pallas as pl
from jax.experimental.pallas import tpu as pltpu

F32 = jnp.float32
BF16 = jnp.bfloat16
HIGHEST = lax.Precision.HIGHEST

RMS_EPS = 1e-6
DILATIONS = ((128, 1), (512, 4), (2048, 16))
N_GROUPS = 3
HEADS_A = 16
HEAD_DIM_A = 64
BAND = 128
ROPE_THETA = 10000.0
NEG_INF = -1e30
HEADS_B = 4
KEY_DIM_B = 128
VAL_DIM_B = 256
GATE_RANK = 16
GATE_TAU = 16.0
GLA_CHUNK = 64
N_EXPERTS = 32
TOP_K = 4
SWIGLU_LIMIT = 7.0
SWIGLU_ALPHA = 1.702

LANES = 128
VMEM_LIMIT = 56 * 1024 * 1024

TM_PROJ = 512
TM_QKV = 256
TM_ROUTE = 512
TM_EXPERT = 512
TM_MOVE = 512
ATTN_ROWS = 256
RUN_ALIGN = 8
RUN_ROWS = TOP_K * TM_MOVE + N_EXPERTS * RUN_ALIGN
T_GLA = 512


def _params(*sem):
    return pltpu.CompilerParams(dimension_semantics=sem, vmem_limit_bytes=VMEM_LIMIT)


def _norm_mod(x, gain, shift, scale):
    ms = jnp.mean(x * x, axis=-1, keepdims=True)
    y = x * lax.rsqrt(ms + RMS_EPS) * gain
    return y * (1.0 + scale) + shift


def _pack_pairs(v):
    k = v.shape[1] // 2
    bits = lax.bitcast_convert_type(v.astype(BF16).astype(F32), jnp.uint32)
    return (bits[:, :k] >> 16) | bits[:, k:]


def _unpack_pairs(w):
    lo = lax.bitcast_convert_type(w << 16, F32)
    hi = lax.bitcast_convert_type(w & jnp.uint32(0xFFFF0000), F32)
    return jnp.concatenate([lo, hi], axis=1)


def _adaln_kernel(c_ref, w_ref, b_ref, o_ref):
    c = c_ref[...]
    cond = c * jax.nn.sigmoid(c)
    o_ref[0] = jnp.dot(cond, w_ref[0], precision=HIGHEST, preferred_element_type=F32) + b_ref[0]


def _adaln(c, ada_w, ada_b):
    depth, d, n6 = ada_w.shape
    b = c.shape[0]
    tn = 1536
    return pl.pallas_call(
        _adaln_kernel,
        grid=(depth, n6 // tn),
        in_specs=[
            pl.BlockSpec((b, d), lambda l, j: (0, 0)),
            pl.BlockSpec((1, d, tn), lambda l, j: (l, 0, j)),
            pl.BlockSpec((1, 1, tn), lambda l, j: (l, 0, j)),
        ],
        out_specs=pl.BlockSpec((1, b, tn), lambda l, j: (l, 0, j)),
        out_shape=jax.ShapeDtypeStruct((depth, b, n6), F32),
        compiler_params=_params("parallel", "parallel"),
        name="adaln",
    )(c, ada_w, ada_b.reshape(depth, 1, n6))


def _rope_kernel(pos_ref, cos_ref, sin_ref):
    pos = pos_ref[...].astype(F32)
    lane = lax.broadcasted_iota(jnp.int32, (1, LANES), 1)
    half = HEAD_DIM_A // 2
    fidx = (lane & (half - 1)).astype(F32)
    inv_freq = jnp.power(jnp.full((1, LANES), ROPE_THETA, F32), -(fidx / half))
    ang = pos * inv_freq
    cos_ref[...] = jnp.cos(ang)
    s = jnp.sin(ang)
    sin_ref[...] = jnp.where(lane < LANES // 2, -s, s)


def _rope_tables(positions):
    n = positions.size
    tm = 1024
    return pl.pallas_call(
        _rope_kernel,
        grid=(n // tm,),
        in_specs=[pl.BlockSpec((tm, 1), lambda i: (i, 0))],
        out_specs=[pl.BlockSpec((tm, LANES), lambda i: (i, 0))] * 2,
        out_shape=[jax.ShapeDtypeStruct((n, LANES), F32)] * 2,
        compiler_params=_params("parallel"),
        name="rope_tables",
    )(positions.reshape(n, 1))


def _stream_order(tm, dil):
    s = jnp.arange(tm)
    return (s % (tm // dil)) * dil + s // (tm // dil)


def _qkv_kernel(x_ref, mod_ref, g_ref, w_hbm, gain_ref, cos_ref, sin_ref, bd_ref, p1_ref, p2_ref, *rest):
    o_refs, (w_ref, w_sem) = rest[:3 * N_GROUPS], rest[3 * N_GROUPS:]
    tm, d = x_ref.shape

    @pl.when(pl.program_id(0) == 0)
    def _():
        copy = pltpu.make_async_copy(w_hbm, w_ref, w_sem)
        copy.start()
        copy.wait()

    h = _norm_mod(x_ref[...], g_ref[...], mod_ref[0, 0:1, :], mod_ref[0, 1:2, :]).astype(BF16)
    hs = [h] + [jnp.dot(p[...], h, preferred_element_type=F32).astype(BF16) for p in (p1_ref, p2_ref)]
    for jj in range(3 * N_GROUPS):
        g, comp = divmod(jj, 3)
        dil = DILATIONS[g][1]
        rows = tm // dil
        out = o_refs[jj]
        acc = jnp.dot(hs[g], w_ref[:, jj * d:(jj + 1) * d], preferred_element_type=F32)
        if comp == 2:
            val = acc.astype(BF16)
            for r in range(dil):
                out[0, r] = val[r * rows:(r + 1) * rows]
            continue
        gain = gain_ref[jj]
        cos = cos_ref[g]
        sin = sin_ref[g]
        for cc in range(d // 256):
            a = acc[:, cc * 256:(cc + 1) * 256]
            ss = jnp.dot((a * a).astype(BF16), bd_ref[...], preferred_element_type=F32)
            qn = a * lax.rsqrt(ss * (1.0 / HEAD_DIM_A) + RMS_EPS) * gain[:, cc * 256:(cc + 1) * 256]
            for hh in range(2):
                xx = qn[:, hh * LANES:(hh + 1) * LANES]
                val = (xx * cos + pltpu.roll(xx, 64, 1) * sin).astype(BF16)
                c1 = cc * 256 + hh * LANES
                for r in range(dil):
                    out[0, r, :, c1:c1 + LANES] = val[r * rows:(r + 1) * rows]


def _split_half_columns(w):
    lead = w.shape[:-1]
    w = w.reshape(*lead, HEADS_A // 2, 2, 2, HEAD_DIM_A // 2)
    return jnp.swapaxes(w, -3, -2).reshape(*lead, HEADS_A * HEAD_DIM_A)


def _qkv_proj(x2, mod, g1, w_in, q_gain, k_gain, positions, batch, seq):
    n, d = x2.shape
    tm = TM_QKV
    per_b = seq // tm
    reps = d // HEAD_DIM_A
    w = w_in.reshape(d, N_GROUPS, 3, d)
    w = jnp.concatenate([_split_half_columns(w[:, :, :2]), w[:, :, 2:]], axis=2).reshape(d, 3 * N_GROUPS * d)
    gains = []
    for g in range(N_GROUPS):
        gains += [_split_half_columns(jnp.tile(q_gain[g], reps)) * (HEAD_DIM_A ** -0.5),
                  _split_half_columns(jnp.tile(k_gain[g], reps)), jnp.ones((d,), F32)]
    gains = jnp.stack(gains).reshape(3 * N_GROUPS, 1, d)
    orders = [_stream_order(tm, dil) for _, dil in DILATIONS]
    pos_tiles = positions.reshape(n // tm, tm)
    cos_t, sin_t = _rope_tables(jnp.stack([pos_tiles[:, o] for o in orders]))
    cos_t, sin_t = cos_t.reshape(N_GROUPS, n, LANES), sin_t.reshape(N_GROUPS, n, LANES)
    perms = [(o[:, None] == jnp.arange(tm)[None, :]).astype(BF16) for o in orders[1:]]
    lane = jnp.arange(256)
    head_of = (lane // LANES) * 2 + (lane // 32) % 2
    bd = (head_of[:, None] == head_of[None, :]).astype(BF16)
    dils = [DILATIONS[jj // 3][1] for jj in range(3 * N_GROUPS)]
    const = lambda shape: pl.BlockSpec(shape, lambda i: (0,) * len(shape))
    flat = pl.pallas_call(
        _qkv_kernel,
        grid=(n // tm,),
        in_specs=[
            pl.BlockSpec((tm, d), lambda i: (i, 0)),
            pl.BlockSpec((1, 6, d), lambda i: (i // per_b, 0, 0)),
            const((1, d)),
            pl.BlockSpec(memory_space=pl.ANY),
            const((3 * N_GROUPS, 1, d)),
            pl.BlockSpec((N_GROUPS, tm, LANES), lambda i: (0, i, 0)),
            pl.BlockSpec((N_GROUPS, tm, LANES), lambda i: (0, i, 0)),
            const((256, 256)),
            const((tm, tm)),
            const((tm, tm)),
        ],
        out_specs=[pl.BlockSpec((1, dil, tm // dil, d), lambda i: (i // per_b, 0, i % per_b, 0)) for dil in dils],
        out_shape=[jax.ShapeDtypeStruct((batch, dil, seq // dil, d), BF16) for dil in dils],
        scratch_shapes=[pltpu.VMEM((d, 3 * N_GROUPS * d), BF16), pltpu.SemaphoreType.DMA(())],
        compiler_params=_params("arbitrary"),
        name="qkv_proj",
    )(x2, mod, g1, w.astype(BF16), gains, cos_t, sin_t, bd, *perms)
    return [flat[3 * g:3 * g + 3] for g in range(N_GROUPS)]


def _attn_kernel(q_ref, kp_ref, kc_ref, vp_ref, vc_ref, o_ref, lse_ref, *, steps):
    q_ref, kp_ref, kc_ref, vp_ref, vc_ref, o_ref, lse_ref = (
        r.at[0, 0] for r in (q_ref, kp_ref, kc_ref, vp_ref, vc_ref, o_ref, lse_ref))
    n = pl.program_id(2)
    row = lax.broadcasted_iota(jnp.int32, (BAND, 2 * BAND), 0)
    col = lax.broadcasted_iota(jnp.int32, (BAND, 2 * BAND), 1)
    delta = row + BAND - col
    in_band = (delta >= 0) & (delta <= steps)
    lane = lax.broadcasted_iota(jnp.int32, (BAND, LANES), 1)
    first_head = lane < HEAD_DIM_A
    even_head = (lane & (HEAD_DIM_A // 2)) == 0
    for sub in range(q_ref.shape[0] // BAND):
        rows = slice(sub * BAND, (sub + 1) * BAND)
        if sub == 0:
            k_prev, v_prev = kp_ref, vp_ref
            valid = in_band & ((col >= BAND) | (n > 0))
        else:
            k_prev, v_prev = kc_ref.at[(sub - 1) * BAND:sub * BAND], vc_ref.at[(sub - 1) * BAND:sub * BAND]
            valid = in_band
        lse_tile = jnp.zeros((BAND, LANES), F32)
        for p in range(HEADS_A // 2):
            sl = slice(p * LANES, (p + 1) * LANES)
            qp = q_ref[rows, sl]
            kcat = jnp.concatenate([k_prev[:, sl], kc_ref[rows, sl]], axis=0)
            vcat = jnp.concatenate([v_prev[:, sl], vc_ref[rows, sl]], axis=0)
            outs = []
            for hh in range(2):
                keep = even_head if hh == 0 else jnp.logical_not(even_head)
                qm = jnp.where(keep, qp, jnp.zeros_like(qp))
                s = lax.dot_general(qm, kcat, (((1,), (1,)), ((), ())), preferred_element_type=F32)
                s = jnp.where(valid, s, NEG_INF)
                m = jnp.max(s, axis=-1, keepdims=True)
                pexp = jnp.exp(s - m)
                den = jnp.sum(pexp, axis=-1, keepdims=True)
                o = jnp.dot(pexp.astype(BF16), vcat, preferred_element_type=F32) / den
                outs.append(o)
                lse_tile = jnp.where(lane == (2 * p + hh), m + jnp.log(den), lse_tile)
            o_ref[rows, sl] = jnp.where(first_head, outs[0], outs[1]).astype(BF16)
        lse_ref[rows, :] = lse_tile


def _dilated_attention(q, k, v, g):
    window, dil = DILATIONS[g]
    batch, _, l, d = q.shape
    qb = ATTN_ROWS
    cur = pl.BlockSpec((1, 1, qb, d), lambda b, r, i: (b, r, i, 0))
    prev = pl.BlockSpec((1, 1, BAND, d), lambda b, r, i: (b, r, jnp.maximum(i * (qb // BAND) - 1, 0), 0))
    return pl.pallas_call(
        functools.partial(_attn_kernel, steps=window // dil),
        grid=(batch, dil, l // qb),
        in_specs=[cur, prev, cur, prev, cur],
        out_specs=[cur, pl.BlockSpec((1, 1, qb, LANES), lambda b, r, i: (b, r, i, 0))],
        out_shape=[
            jax.ShapeDtypeStruct((batch, dil, l, d), BF16),
            jax.ShapeDtypeStruct((batch, dil, l, LANES), F32),
        ],
        compiler_params=_params("parallel", "parallel", "arbitrary"),
        name=f"dilated_attn_g{g}",
    )(q, k, k, v, v)


def _split3(v):
    hi = v.astype(BF16)
    rem = v - hi.astype(F32)
    mid = rem.astype(BF16)
    return hi, mid, (rem - mid.astype(F32)).astype(BF16)


def _dot3(mat_bf, v):
    return sum(jnp.dot(mat_bf, part, preferred_element_type=F32) for part in _split3(v))


def _merge_proj_kernel(o0_ref, o1_ref, o2_ref, l0_ref, l1_ref, l2_ref, p1_ref, p2_ref, x_ref, mod_ref, e_ref,
                       w_ref, out_ref):
    tm, d = x_ref.shape
    perms = (None, p1_ref, p2_ref)
    lses, outs = [], []
    for perm, l_ref, o_ref in zip(perms, (l0_ref, l1_ref, l2_ref), (o0_ref, o1_ref, o2_ref)):
        lse = l_ref[0].reshape(tm, LANES)
        o = o_ref[0].reshape(tm, d)
        if perm is None:
            lses.append(lse)
            outs.append(o.astype(F32))
        else:
            lses.append(_dot3(perm[...], lse))
            outs.append(jnp.dot(perm[...], o, preferred_element_type=F32))
    m = jnp.maximum(jnp.maximum(lses[0], lses[1]), lses[2])
    exps = [jnp.exp(l - m) for l in lses]
    inv = 1.0 / (exps[0] + exps[1] + exps[2])
    acc = None
    for e, o in zip(exps, outs):
        w = e * inv
        w_hi = w.astype(BF16)
        w_lo = (w - w_hi.astype(F32)).astype(BF16)
        wide = (jnp.dot(w_hi, e_ref[...], preferred_element_type=F32)
                + jnp.dot(w_lo, e_ref[...], preferred_element_type=F32))
        acc = wide * o if acc is None else acc + wide * o
    y = jnp.dot(acc.astype(BF16), w_ref[...], preferred_element_type=F32)
    out_ref[...] = x_ref[...] + mod_ref[0, 2:3, :] * y


def _merge_proj(outs, lses, x2, mod, w_out_bf, seq):
    n, d = x2.shape
    tm = TM_PROJ
    per_b = seq // tm
    expand = (jnp.arange(LANES)[:, None] == jnp.arange(d)[None, :] // HEAD_DIM_A).astype(BF16)
    tok = jnp.arange(tm)
    perms = [(((tok % dil) * (tm // dil) + tok // dil)[:, None] == tok[None, :]).astype(BF16)
             for _, dil in DILATIONS[1:]]
    slab = lambda dil, width: pl.BlockSpec((1, dil, tm // dil, width), lambda i: (i // per_b, 0, i % per_b, 0))
    const = lambda shape: pl.BlockSpec(shape, lambda i: (0, 0))
    return pl.pallas_call(
        _merge_proj_kernel,
        grid=(n // tm,),
        in_specs=[slab(dil, d) for _, dil in DILATIONS] + [slab(dil, LANES) for _, dil in DILATIONS] + [
            const((tm, tm)),
            const((tm, tm)),
            pl.BlockSpec((tm, d), lambda i: (i, 0)),
            pl.BlockSpec((1, 6, d), lambda i: (i // per_b, 0, 0)),
            const((LANES, d)),
            const((d, d)),
        ],
        out_specs=pl.BlockSpec((tm, d), lambda i: (i, 0)),
        out_shape=jax.ShapeDtypeStruct((n, d), F32),
        compiler_params=_params("parallel"),
        name="merge_out_proj",
    )(*outs, *lses, *perms, x2, mod, expand, w_out_bf)


def _gla_in_kernel(x_ref, mod_ref, g_ref, w_ref, o_ref):
    h = _norm_mod(x_ref[...], g_ref[...], mod_ref[0, 0:1, :], mod_ref[0, 1:2, :])
    o_ref[...] = jnp.dot(h.astype(BF16), w_ref[...], preferred_element_type=F32).astype(BF16)


def _gla_in_proj(x2, mod, g1, w_bf, seq):
    n, d = x2.shape
    ncol = w_bf.shape[1]
    tm = TM_PROJ
    per_b = seq // tm
    return pl.pallas_call(
        _gla_in_kernel,
        grid=(n // tm,),
        in_specs=[
            pl.BlockSpec((tm, d), lambda i: (i, 0)),
            pl.BlockSpec((1, 6, d), lambda i: (i // per_b, 0, 0)),
            pl.BlockSpec((1, d), lambda i: (0, 0)),
            pl.BlockSpec((d, ncol), lambda i: (0, 0)),
        ],
        out_specs=pl.BlockSpec((tm, ncol), lambda i: (i, 0)),
        out_shape=jax.ShapeDtypeStruct((n, ncol), BF16),
        compiler_params=_params("parallel"),
        name="gla_in_proj",
    )(x2, mod, g1, w_bf)


def _gla_kernel(q_ref, k_ref, v_ref, r_ref, a_ref, wg_ref, gb_ref, og_ref, tri_ref, x_ref, mod_ref, wo_ref,
                out_ref, st_ref, o_scr, la_scr):
    t = pl.program_id(1)
    c = GLA_CHUNK

    @pl.when(t == 0)
    def _():
        st_ref[...] = jnp.zeros_like(st_ref)

    g = jnp.dot(a_ref[0], wg_ref[...], preferred_element_type=F32) + gb_ref[...]
    la_scr[...] = (jnp.minimum(g, 0.0) - jnp.log(1.0 + jnp.exp(-jnp.abs(g)))) * (1.0 / GATE_TAU)

    rr = lax.broadcasted_iota(jnp.int32, (c, c), 0)
    cc = lax.broadcasted_iota(jnp.int32, (c, c), 1)
    causal = cc <= rr

    def chunk(ci, carry):
        rows = pl.ds(pl.multiple_of(ci * c, c), c)
        la = la_scr[rows, :]
        b = _dot3(tri_ref[...], la)
        b_last = b[c - 1:c, :]
        q = q_ref[0, rows, :].astype(F32) * (KEY_DIM_B ** -0.5)
        k = k_ref[0, rows, :].astype(F32)
        q_dec = (q * jnp.exp(b)).astype(BF16)
        k_inv = (k * jnp.exp(-b)).astype(BF16)
        k_dec = (k * jnp.exp(b_last - b)).astype(BF16)
        decay = jnp.exp(b_last)
        v = v_ref[0, rows, :]
        for h in range(HEADS_B):
            ks = slice(h * KEY_DIM_B, (h + 1) * KEY_DIM_B)
            vs = slice(h * VAL_DIM_B, (h + 1) * VAL_DIM_B)
            qd, ki, kd, vh = q_dec[:, ks], k_inv[:, ks], k_dec[:, ks], v[:, vs]
            att = lax.dot_general(qd, ki, (((1,), (1,)), ((), ())), preferred_element_type=F32)
            att = jnp.where(causal, att, 0.0)
            st = st_ref[h]
            o = (jnp.dot(att.astype(BF16), vh, preferred_element_type=F32)
                 + lax.dot_general(qd, st.astype(BF16), (((1,), (1,)), ((), ())), preferred_element_type=F32))
            o_scr[rows, vs] = o
            st_ref[h] = st * decay[:, ks] + lax.dot_general(vh, kd, (((0,), (0,)), ((), ())),
                                                             preferred_element_type=F32)
        return carry

    lax.fori_loop(0, q_ref.shape[1] // c, chunk, 0, unroll=2)

    r = r_ref[0].astype(F32)
    parts = []
    for h in range(HEADS_B):
        vs = slice(h * VAL_DIM_B, (h + 1) * VAL_DIM_B)
        oh = o_scr[:, vs]
        ms = jnp.mean(oh * oh, axis=-1, keepdims=True)
        rh = r[:, vs]
        parts.append((oh * lax.rsqrt(ms + RMS_EPS) * og_ref[...] * (rh * jax.nn.sigmoid(rh))).astype(BF16))
    y = jnp.dot(jnp.concatenate(parts, axis=1), wo_ref[...], preferred_element_type=F32)
    out_ref[0] = x_ref[0] + mod_ref[0, 2:3, :] * y


def _gla(proj, x3, mod, wg_bf, gate_bias, out_gain, w_out_bf):
    batch, seq, d = x3.shape
    ncol = proj.shape[1]
    t = T_GLA
    hk = HEADS_B * KEY_DIM_B
    hv = HEADS_B * VAL_DIM_B
    p3 = proj.reshape(batch, seq, ncol)
    tri = (jnp.arange(GLA_CHUNK)[None, :] <= jnp.arange(GLA_CHUNK)[:, None]).astype(BF16)
    a_blk = (2 * hk + 2 * hv) // LANES
    const = lambda shape: pl.BlockSpec(shape, lambda b, i: (0,) * len(shape))
    return pl.pallas_call(
        _gla_kernel,
        grid=(batch, seq // t),
        in_specs=[
            pl.BlockSpec((1, t, hk), lambda b, i: (b, i, 0)),
            pl.BlockSpec((1, t, hk), lambda b, i: (b, i, 1)),
            pl.BlockSpec((1, t, hv), lambda b, i: (b, i, (2 * hk) // hv)),
            pl.BlockSpec((1, t, hv), lambda b, i: (b, i, (2 * hk + hv) // hv)),
            pl.BlockSpec((1, t, LANES), lambda b, i: (b, i, a_blk)),
            const((LANES, hk)),
            const((1, hk)),
            const((1, VAL_DIM_B)),
            const((GLA_CHUNK, GLA_CHUNK)),
            pl.BlockSpec((1, t, d), lambda b, i: (b, i, 0)),
            pl.BlockSpec((1, 6, d), lambda b, i: (b, 0, 0)),
            const((hv, d)),
        ],
        out_specs=pl.BlockSpec((1, t, d), lambda b, i: (b, i, 0)),
        out_shape=jax.ShapeDtypeStruct((batch, seq, d), F32),
        scratch_shapes=[
            pltpu.VMEM((HEADS_B, VAL_DIM_B, KEY_DIM_B), F32),
            pltpu.VMEM((t, hv), F32),
            pltpu.VMEM((t, hk), F32),
        ],
        compiler_params=_params("parallel", "arbitrary"),
        name="gla",
    )(p3, p3, p3, p3, p3, wg_bf, gate_bias, out_gain, tri, x3, mod, w_out_bf)


def _router_kernel(x_ref, mod_ref, g_ref, rw_ref, rb_ref, tri_ref, upper_ref, h_ref, info_ref, cnt_ref, tcnt_ref,
                   tcar_ref, carry_ref):
    i = pl.program_id(0)

    @pl.when(i == 0)
    def _():
        carry_ref[...] = jnp.zeros_like(carry_ref)

    h = _norm_mod(x_ref[...], g_ref[...], mod_ref[0, 3:4, :], mod_ref[0, 4:5, :])
    h_ref[...] = h.astype(BF16)
    tm = h.shape[0]
    lane = lax.broadcasted_iota(jnp.int32, (tm, LANES), 1)
    lane_f = lane.astype(F32)
    h_hi = h.astype(BF16)
    h_lo = (h - h_hi.astype(F32)).astype(BF16)
    logits = (jnp.dot(h_hi, rw_ref[0], preferred_element_type=F32) + jnp.dot(h_lo, rw_ref[0], preferred_element_type=F32)
              + jnp.dot(h_hi, rw_ref[1], preferred_element_type=F32) + rb_ref[...])
    work = jnp.where(lane < N_EXPERTS, logits, -jnp.inf)
    picks, vals, idxs = [], [], []
    for _ in range(TOP_K):
        m = jnp.max(work, axis=-1, keepdims=True)
        idx = jnp.min(jnp.where(work == m, lane_f, float(LANES)), axis=-1, keepdims=True)
        pick = lane_f == idx
        work = jnp.where(pick, -jnp.inf, work)
        picks.append(pick)
        vals.append(m)
        idxs.append(idx)
    exps = [jnp.exp(v - vals[0]) for v in vals]
    inv = 1.0 / (exps[0] + exps[1] + exps[2] + exps[3])
    chosen = jnp.zeros((tm, LANES), F32)
    for pick in picks:
        chosen = jnp.where(pick, 1.0, chosen)
    here = jnp.floor((jnp.sum(chosen, axis=0, keepdims=True) + (RUN_ALIGN - 1)) * (1.0 / RUN_ALIGN)) * RUN_ALIGN
    run_start = jnp.dot(jnp.broadcast_to(here, (8, LANES)).astype(BF16), upper_ref[...],
                        preferred_element_type=F32)[0:1, :]
    sorted_row = jnp.dot(tri_ref[...], chosen.astype(BF16), preferred_element_type=F32) + run_start
    info = jnp.zeros((tm, LANES), F32)
    for kk in range(TOP_K):
        row = jnp.sum(jnp.where(picks[kk], sorted_row, 0.0), axis=-1, keepdims=True)
        info = jnp.where(lane == kk, idxs[kk], info)
        info = jnp.where(lane == TOP_K + kk, row, info)
        info = jnp.where(lane == 2 * TOP_K + kk, exps[kk] * inv, info)
    info_ref[...] = info
    tcnt_ref[0] = jnp.broadcast_to(here, tcnt_ref.shape[1:])
    tcar_ref[0] = carry_ref[...]
    total = carry_ref[0:1, :] + here
    carry_ref[...] = jnp.broadcast_to(total, carry_ref.shape)
    cnt_ref[...] = jnp.broadcast_to(total, cnt_ref.shape)


def _router(x2, mod, g2, router_w, router_b, seq):
    n, d = x2.shape
    tm = TM_ROUTE
    per_b = seq // tm
    rw = jnp.zeros((d, LANES), F32).at[:, :N_EXPERTS].set(router_w)
    rw_hi = rw.astype(BF16)
    rw = jnp.stack([rw_hi, (rw - rw_hi.astype(F32)).astype(BF16)])
    rb = jnp.zeros((1, LANES), F32).at[0, :N_EXPERTS].set(router_b)
    tri = (jnp.arange(tm)[None, :] < jnp.arange(tm)[:, None]).astype(BF16)
    upper = (jnp.arange(LANES)[:, None] < jnp.arange(LANES)[None, :]).astype(BF16)
    return pl.pallas_call(
        _router_kernel,
        grid=(n // tm,),
        in_specs=[
            pl.BlockSpec((tm, d), lambda i: (i, 0)),
            pl.BlockSpec((1, 6, d), lambda i: (i // per_b, 0, 0)),
            pl.BlockSpec((1, d), lambda i: (0, 0)),
            pl.BlockSpec((2, d, LANES), lambda i: (0, 0, 0)),
            pl.BlockSpec((1, LANES), lambda i: (0, 0)),
            pl.BlockSpec((tm, tm), lambda i: (0, 0)),
            pl.BlockSpec((LANES, LANES), lambda i: (0, 0)),
        ],
        out_specs=[
            pl.BlockSpec((tm, d), lambda i: (i, 0)),
            pl.BlockSpec((tm, LANES), lambda i: (i, 0)),
            pl.BlockSpec((8, LANES), lambda i: (0, 0)),
            pl.BlockSpec((1, 8, LANES), lambda i: (i, 0, 0)),
            pl.BlockSpec((1, 8, LANES), lambda i: (i, 0, 0)),
        ],
        out_shape=[
            jax.ShapeDtypeStruct((n, d), BF16),
            jax.ShapeDtypeStruct((n, LANES), F32),
            jax.ShapeDtypeStruct((8, LANES), F32),
            jax.ShapeDtypeStruct((n // tm, 8, LANES), F32),
            jax.ShapeDtypeStruct((n // tm, 8, LANES), F32),
        ],
        scratch_shapes=[pltpu.VMEM((8, LANES), F32)],
        compiler_params=_params("arbitrary"),
        name="moe_router",
    )(x2, mod, g2, rw, rb, tri, upper)


def _for_each_run(tile, cnt_ref, lo_ref, base_ref, act):
    bits = [1 << k for k in range(TM_MOVE.bit_length() - 1, RUN_ALIGN.bit_length() - 2, -1)]

    def one(e, carry):
        j = tile * N_EXPERTS + e
        cnt, lo, base = cnt_ref[j], lo_ref[j], base_ref[j]
        off = jnp.int32(0)
        for bit in bits:
            take = (cnt & bit) != 0
            pl.when(take)(functools.partial(act, pl.multiple_of(lo + off, RUN_ALIGN),
                                            pl.multiple_of(base + off, RUN_ALIGN), bit))
            off = off + jnp.where(take, bit, 0)
        return carry

    lax.fori_loop(0, N_EXPERTS, one, 0)


def _dispatch_kernel(seg_end_ref, padded_ref, cnt_ref, lo_ref, base_ref, pos_ref, h_ref, xs_hbm,
                     loc_ref, zero_ref, sem, zero_sem):
    i = pl.program_id(0)
    tm, d = h_ref.shape

    @pl.when(pl.program_id(0) == 0)
    def _():
        zero_ref[...] = jnp.zeros_like(zero_ref)

        def tail(e):
            return pltpu.make_async_copy(
                zero_ref, xs_hbm.at[pl.ds(pl.multiple_of(seg_end_ref[e] - TM_EXPERT, TM_EXPERT), TM_EXPERT)], zero_sem)

        def start(e, carry):
            pl.when(padded_ref[e] > 0)(lambda: tail(e).start())
            return carry

        def wait(e, carry):
            pl.when(padded_ref[e] > 0)(lambda: tail(e).wait())
            return carry

        lax.fori_loop(0, N_EXPERTS, start, 0)
        lax.fori_loop(0, N_EXPERTS, wait, 0)

        def spare(blk):
            return pltpu.make_async_copy(
                zero_ref, xs_hbm.at[pl.ds(pl.multiple_of(blk * TM_EXPERT, TM_EXPERT), TM_EXPERT)], zero_sem)

        used = seg_end_ref[N_EXPERTS - 1] // TM_EXPERT
        total = xs_hbm.shape[0] // TM_EXPERT
        lax.fori_loop(used, total, lambda blk, carry: (spare(blk).start(), carry)[1], 0)
        lax.fori_loop(used, total, lambda blk, carry: (spare(blk).wait(), carry)[1], 0)

    slot = lax.broadcasted_iota(jnp.int32, (RUN_ROWS, tm), 0).astype(F32)
    hit = slot == pos_ref[0:1, :]
    for kk in range(1, TOP_K):
        hit = hit | (slot == pos_ref[kk:kk + 1, :])
    perm = jnp.where(hit, 1.0, 0.0).astype(BF16)
    loc_ref[...] = _pack_pairs(jnp.dot(perm, h_ref[...], preferred_element_type=F32))

    def copy(local_row, global_row, size):
        return pltpu.make_async_copy(loc_ref.at[pl.ds(local_row, size)], xs_hbm.at[pl.ds(global_row, size)], sem)

    _for_each_run(i, cnt_ref, lo_ref, base_ref, lambda a, b, size: copy(a, b, size).start())
    _for_each_run(i, cnt_ref, lo_ref, base_ref, lambda a, b, size: copy(a, b, size).wait())


def _dispatch(h, pos_t, seg_end, padded, tile_cnt, tile_lo, tile_base, cap):
    n, d = h.shape
    tm = TM_MOVE
    smem = lambda i, *_: (0, i)
    return pl.pallas_call(
        _dispatch_kernel,
        grid_spec=pltpu.PrefetchScalarGridSpec(
            num_scalar_prefetch=5,
            grid=(n // tm,),
            in_specs=[
                pl.BlockSpec((8, tm), smem),
                pl.BlockSpec((tm, d), lambda i, *_: (i, 0)),
            ],
            out_specs=pl.BlockSpec(memory_space=pl.ANY),
            scratch_shapes=[pltpu.VMEM((RUN_ROWS, d // 2), jnp.uint32), pltpu.VMEM((TM_EXPERT, d // 2), jnp.uint32),
                            pltpu.SemaphoreType.DMA(()), pltpu.SemaphoreType.DMA(())],
        ),
        out_shape=jax.ShapeDtypeStruct((cap, d // 2), jnp.uint32),
        compiler_params=_params("arbitrary"),
        name="moe_dispatch",
    )(seg_end, padded, tile_cnt, tile_lo, tile_base, pos_t, h)


def _expert_kernel(be_ref, nxt_ref, na_ref, x_ref, wgu_hbm, bgu_ref, wd_hbm, bd_ref, o_ref,
                   gu_stage, d_stage, gu_bf, d_bf, sems, *, layer):
    i = pl.program_id(0)
    live = i < na_ref[0]
    e = be_ref[i]

    def fetch(expert):
        return (pltpu.make_async_copy(wgu_hbm.at[layer, expert], gu_stage, sems.at[0]),
                pltpu.make_async_copy(wd_hbm.at[layer, expert], d_stage, sems.at[1]))

    @pl.when(i == 0)
    def _():
        for c in fetch(e):
            c.start()

    @pl.when(live & ((i == 0) | (e != be_ref[jnp.maximum(i - 1, 0)])))
    def _():
        for c in fetch(e):
            c.wait()
        gu_bf[...] = gu_stage[...].astype(BF16)
        d_bf[...] = d_stage[...].astype(BF16)

        @pl.when(nxt_ref[i] >= 0)
        def _():
            for c in fetch(nxt_ref[i]):
                c.start()

    @pl.when(jnp.logical_not(live))
    def _():
        o_ref[...] = jnp.zeros_like(o_ref)

    @pl.when(live)
    def _():
        f = d_bf.shape[0]
        xb = _unpack_pairs(x_ref[...]).astype(BF16)
        gu = jnp.dot(xb, gu_bf[...], preferred_element_type=F32) + bgu_ref[0]
        gate = jnp.minimum(gu[:, :f], SWIGLU_LIMIT)
        up = jnp.clip(gu[:, f:], -SWIGLU_LIMIT, SWIGLU_LIMIT)
        act = (up + 1.0) * (gate * jax.nn.sigmoid(SWIGLU_ALPHA * gate))
        o_ref[...] = _pack_pairs(jnp.dot(act.astype(BF16), d_bf[...], preferred_element_type=F32) + bd_ref[0])


def _experts(xs, block_e, next_e, n_active, w_gu, b_gu, w_down, b_down, layer):
    cap, w = xs.shape
    _, e, d, f2 = w_gu.shape
    f = f2 // 2
    tm = TM_EXPERT
    nb = cap // tm
    return pl.pallas_call(
        functools.partial(_expert_kernel, layer=layer),
        grid_spec=pltpu.PrefetchScalarGridSpec(
            num_scalar_prefetch=3,
            grid=(nb,),
            in_specs=[
                pl.BlockSpec((tm, w), lambda i, be, nx, na: (jnp.minimum(i, na[0] - 1), 0)),
                pl.BlockSpec(memory_space=pl.ANY),
                pl.BlockSpec((1, 1, f2), lambda i, be, nx, na: (be[i], 0, 0)),
                pl.BlockSpec(memory_space=pl.ANY),
                pl.BlockSpec((1, 1, d), lambda i, be, nx, na: (be[i], 0, 0)),
            ],
            out_specs=pl.BlockSpec((tm, w), lambda i, be, nx, na: (i, 0)),
            scratch_shapes=[pltpu.VMEM((d, f2), F32), pltpu.VMEM((f, d), F32), pltpu.VMEM((d, f2), BF16),
                            pltpu.VMEM((f, d), BF16), pltpu.SemaphoreType.DMA((2,))],
        ),
        out_shape=jax.ShapeDtypeStruct((cap, w), xs.dtype),
        compiler_params=_params("arbitrary"),
        name="moe_experts",
    )(block_e, next_e, n_active, xs, w_gu, b_gu.reshape(e, 1, f2), w_down, b_down.reshape(e, 1, d))


def _combine_kernel(cnt_ref, lo_ref, base_ref, info_ref, x_ref, mod_ref, ys_hbm, out_ref, loc_ref, sem):
    i = pl.program_id(0)
    tm = x_ref.shape[0]

    @pl.when(i == 0)
    def _():
        loc_ref[...] = jnp.zeros_like(loc_ref)

    def copy(local_row, global_row, size):
        return pltpu.make_async_copy(ys_hbm.at[pl.ds(global_row, size)], loc_ref.at[pl.ds(local_row, size)], sem)

    _for_each_run(i, cnt_ref, lo_ref, base_ref, lambda a, b, size: copy(a, b, size).start())
    _for_each_run(i, cnt_ref, lo_ref, base_ref, lambda a, b, size: copy(a, b, size).wait())

    info = info_ref[...]
    slot = lax.broadcasted_iota(jnp.int32, (tm, RUN_ROWS), 1).astype(F32)
    mix = jnp.zeros((tm, RUN_ROWS), F32)
    for kk in range(TOP_K):
        mix = jnp.where(slot == info[:, TOP_K + kk:TOP_K + kk + 1], info[:, 2 * TOP_K + kk:2 * TOP_K + kk + 1], mix)
    rows = _unpack_pairs(loc_ref[...]).astype(BF16)
    y = jnp.dot(mix.astype(BF16), rows, preferred_element_type=F32)
    out_ref[...] = x_ref[...] + mod_ref[0, 5:6, :] * y


def _combine(ys, info, tile_cnt, tile_lo, tile_base, x2, mod, seq):
    n, d = x2.shape
    tm = TM_MOVE
    per_b = seq // tm
    return pl.pallas_call(
        _combine_kernel,
        grid_spec=pltpu.PrefetchScalarGridSpec(
            num_scalar_prefetch=3,
            grid=(n // tm,),
            in_specs=[
                pl.BlockSpec((tm, LANES), lambda i, *_: (i, 0)),
                pl.BlockSpec((tm, d), lambda i, *_: (i, 0)),
                pl.BlockSpec((1, 6, d), lambda i, *_: (i // per_b, 0, 0)),
                pl.BlockSpec(memory_space=pl.ANY),
            ],
            out_specs=pl.BlockSpec((tm, d), lambda i, *_: (i, 0)),
            scratch_shapes=[pltpu.VMEM((RUN_ROWS, ys.shape[1]), ys.dtype), pltpu.SemaphoreType.DMA(())],
        ),
        out_shape=jax.ShapeDtypeStruct((n, d), F32),
        compiler_params=_params("arbitrary"),
        name="moe_combine",
    )(tile_cnt, tile_lo, tile_base, info, x2, mod, ys)


def _moe(x2, mod, g2, router_w, router_b, w_gu, b_gu, w_down, b_down, layer, seq):
    n, d = x2.shape
    assert TM_MOVE == TM_ROUTE, "dispatch / combine tiles reuse the router's per-tile expert counts"
    h, info, cnt, tcnt, tcar = _router(x2, mod, g2, router_w, router_b, seq)
    counts = cnt[0, :N_EXPERTS].astype(jnp.int32)
    padded = (counts + TM_EXPERT - 1) // TM_EXPERT * TM_EXPERT
    seg_end = jnp.cumsum(padded)
    seg_start = seg_end - padded
    worst = n * TOP_K + N_EXPERTS * (n // TM_MOVE) * (RUN_ALIGN - 1)
    nb = -(-worst // TM_EXPERT) + N_EXPERTS
    cap = nb * TM_EXPERT
    block_start = jnp.arange(nb, dtype=jnp.int32) * TM_EXPERT
    block_e = jnp.minimum(jnp.sum(seg_end[None, :] <= block_start[:, None], axis=1), N_EXPERTS - 1).astype(jnp.int32)
    n_active = (seg_end[-1:] // TM_EXPERT).astype(jnp.int32)
    tile_cnt = tcnt[:, 0, :N_EXPERTS].astype(jnp.int32)
    tile_before = tcar[:, 0, :N_EXPERTS].astype(jnp.int32)
    tile_lo = jnp.cumsum(tile_cnt, axis=1) - tile_cnt
    tile_base = seg_start[None, :] + tile_before
    pos_t = info[:, TOP_K:TOP_K + 8].T
    tile_cnt, tile_lo, tile_base = (t.reshape(-1).astype(jnp.int32) for t in (tile_cnt, tile_lo, tile_base))
    ids = jnp.arange(N_EXPERTS, dtype=jnp.int32)
    later = (ids[None, :] > ids[:, None]) & (padded[None, :] > 0)
    next_of = jnp.min(jnp.where(later, ids[None, :], N_EXPERTS), axis=1)
    next_e = jnp.where(next_of < N_EXPERTS, next_of, -1).astype(jnp.int32)[block_e]
    xs = _dispatch(h, pos_t, seg_end.astype(jnp.int32), padded.astype(jnp.int32), tile_cnt, tile_lo, tile_base, cap)
    ys = _experts(xs, block_e, next_e, n_active, w_gu, b_gu, w_down, b_down, layer)
    return _combine(ys, info, tile_cnt, tile_lo, tile_base, x2, mod, seq)


def kernel(x, c, positions, ada_w, ada_b, norm1_g, norm2_g, a_w_in, a_q_gain, a_k_gain, a_w_out, b_w_in,
           b_w_gate_up, b_gate_bias, b_out_gain, b_w_out, router_w, router_b, moe_w_gu, moe_b_gu, moe_w_down,
           moe_b_down):
    batch, seq, d = x.shape
    depth = ada_w.shape[0]
    n = batch * seq
    mods = _adaln(c, ada_w, ada_b).reshape(depth, batch, 6, d)
    x2 = x.reshape(n, d)
    for layer in range(depth):
        mod = mods[layer]
        j = layer // 2
        g1 = norm1_g[layer].reshape(1, d)
        g2 = norm2_g[layer].reshape(1, d)
        if layer % 2 == 0:
            qkv = _qkv_proj(x2, mod, g1, a_w_in[j], a_q_gain[j], a_k_gain[j], positions, batch, seq)
            outs, lses = zip(*[_dilated_attention(*qkv[g], g) for g in range(N_GROUPS)])
            x2 = _merge_proj(outs, lses, x2, mod, a_w_out[j].astype(BF16), seq)
        else:
            hk = HEADS_B * KEY_DIM_B
            hv = HEADS_B * VAL_DIM_B
            width = 2 * hk + 2 * hv
            w_in = jnp.zeros((d, width + LANES), BF16).at[:, :width + GATE_RANK].set(b_w_in[j].astype(BF16))
            wg = jnp.zeros((LANES, hk), BF16).at[:GATE_RANK].set(b_w_gate_up[j].astype(BF16))
            proj = _gla_in_proj(x2, mod, g1, w_in, seq)
            x3 = _gla(proj, x2.reshape(batch, seq, d), mod, wg, b_gate_bias[j].reshape(1, hk),
                      b_out_gain[j].reshape(1, VAL_DIM_B), b_w_out[j].astype(BF16))
            x2 = x3.reshape(n, d)
        x2 = _moe(x2, mod, g2, router_w[layer], router_b[layer], moe_w_gu, moe_b_gu[layer],
                  moe_w_down, moe_b_down[layer], layer, seq)
    return x2.reshape(batch, seq, d)
```

```python
import functools

import jax
import jax.numpy as jnp
from jax import lax
from jax.experimental import pallas as pl
from jax.experimental.pallas import tpu as pltpu

F32 = jnp.float32
BF16 = jnp.bfloat16
HIGHEST = lax.Precision.HIGHEST

RMS_EPS = 1e-6
DILATIONS = ((128, 1), (512, 4), (2048, 16))
N_GROUPS = 3
HEADS_A = 16
HEAD_DIM_A = 64
BAND = 128
ROPE_THETA = 10000.0
NEG_INF = -1e30
HEADS_B = 4
KEY_DIM_B = 128
VAL_DIM_B = 256
GATE_RANK = 16
GATE_TAU = 16.0
GLA_CHUNK = 64
N_EXPERTS = 32
TOP_K = 4
SWIGLU_LIMIT = 7.0
SWIGLU_ALPHA = 1.702

LANES = 128
VMEM_LIMIT = 56 * 1024 * 1024

TM_PROJ = 512
TM_QKV = 256
TM_ROUTE = 512
TM_EXPERT = 512
TM_MOVE = 512
ATTN_ROWS = 256
RUN_ALIGN = 8
RUN_ROWS = TOP_K * TM_MOVE + N_EXPERTS * RUN_ALIGN
T_GLA = 512


def _params(*sem):
    return pltpu.CompilerParams(dimension_semantics=sem, vmem_limit_bytes=VMEM_LIMIT)


def _norm_mod(x, gain, shift, scale):
    ms = jnp.mean(x * x, axis=-1, keepdims=True)
    y = x * lax.rsqrt(ms + RMS_EPS) * gain
    return y * (1.0 + scale) + shift


def _pack_pairs(v):
    k = v.shape[1] // 2
    bits = lax.bitcast_convert_type(v.astype(BF16).astype(F32), jnp.uint32)
    return (bits[:, :k] >> 16) | bits[:, k:]


def _unpack_pairs(w):
    lo = lax.bitcast_convert_type(w << 16, F32)
    hi = lax.bitcast_convert_type(w & jnp.uint32(0xFFFF0000), F32)
    return jnp.concatenate([lo, hi], axis=1)


def _adaln_kernel(c_ref, w_ref, b_ref, o_ref):
    c = c_ref[...]
    cond = c * jax.nn.sigmoid(c)
    o_ref[0] = jnp.dot(cond, w_ref[0], precision=HIGHEST, preferred_element_type=F32) + b_ref[0]


def _adaln(c, ada_w, ada_b):
    depth, d, n6 = ada_w.shape
    b = c.shape[0]
    tn = 1536
    return pl.pallas_call(
        _adaln_kernel,
        grid=(depth, n6 // tn),
        in_specs=[
            pl.BlockSpec((b, d), lambda l, j: (0, 0)),
            pl.BlockSpec((1, d, tn), lambda l, j: (l, 0, j)),
            pl.BlockSpec((1, 1, tn), lambda l, j: (l, 0, j)),
        ],
        out_specs=pl.BlockSpec((1, b, tn), lambda l, j: (l, 0, j)),
        out_shape=jax.ShapeDtypeStruct((depth, b, n6), F32),
        compiler_params=_params("parallel", "parallel"),
        name="adaln",
    )(c, ada_w, ada_b.reshape(depth, 1, n6))


def _rope_kernel(pos_ref, cos_ref, sin_ref):
    pos = pos_ref[...].astype(F32)
    lane = lax.broadcasted_iota(jnp.int32, (1, LANES), 1)
    half = HEAD_DIM_A // 2
    fidx = (lane & (half - 1)).astype(F32)
    inv_freq = jnp.power(jnp.full((1, LANES), ROPE_THETA, F32), -(fidx / half))
    ang = pos * inv_freq
    cos_ref[...] = jnp.cos(ang)
    s = jnp.sin(ang)
    sin_ref[...] = jnp.where(lane < LANES // 2, -s, s)


def _rope_tables(positions):
    n = positions.size
    tm = 1024
    return pl.pallas_call(
        _rope_kernel,
        grid=(n // tm,),
        in_specs=[pl.BlockSpec((tm, 1), lambda i: (i, 0))],
        out_specs=[pl.BlockSpec((tm, LANES), lambda i: (i, 0))] * 2,
        out_shape=[jax.ShapeDtypeStruct((n, LANES), F32)] * 2,
        compiler_params=_params("parallel"),
        name="rope_tables",
    )(positions.reshape(n, 1))


def _stream_order(tm, dil):
    s = jnp.arange(tm)
    return (s % (tm // dil)) * dil + s // (tm // dil)


def _qkv_kernel(x_ref, mod_ref, g_ref, w_hbm, gain_ref, cos_ref, sin_ref, bd_ref, p1_ref, p2_ref, *rest):
    o_refs, (w_ref, w_sem) = rest[:3 * N_GROUPS], rest[3 * N_GROUPS:]
    tm, d = x_ref.shape

    @pl.when(pl.program_id(0) == 0)
    def _():
        copy = pltpu.make_async_copy(w_hbm, w_ref, w_sem)
        copy.start()
        copy.wait()

    h = _norm_mod(x_ref[...], g_ref[...], mod_ref[0, 0:1, :], mod_ref[0, 1:2, :]).astype(BF16)
    hs = [h] + [jnp.dot(p[...], h, preferred_element_type=F32).astype(BF16) for p in (p1_ref, p2_ref)]
    for jj in range(3 * N_GROUPS):
        g, comp = divmod(jj, 3)
        dil = DILATIONS[g][1]
        rows = tm // dil
        out = o_refs[jj]
        acc = jnp.dot(hs[g], w_ref[:, jj * d:(jj + 1) * d], preferred_element_type=F32)
        if comp == 2:
            val = acc.astype(BF16)
            for r in range(dil):
                out[0, r] = val[r * rows:(r + 1) * rows]
            continue
        gain = gain_ref[jj]
        cos = cos_ref[g]
        sin = sin_ref[g]
        for cc in range(d // 256):
            a = acc[:, cc * 256:(cc + 1) * 256]
            ss = jnp.dot((a * a).astype(BF16), bd_ref[...], preferred_element_type=F32)
            qn = a * lax.rsqrt(ss * (1.0 / HEAD_DIM_A) + RMS_EPS) * gain[:, cc * 256:(cc + 1) * 256]
            for hh in range(2):
                xx = qn[:, hh * LANES:(hh + 1) * LANES]
                val = (xx * cos + pltpu.roll(xx, 64, 1) * sin).astype(BF16)
                c1 = cc * 256 + hh * LANES
                for r in range(dil):
                    out[0, r, :, c1:c1 + LANES] = val[r * rows:(r + 1) * rows]


def _split_half_columns(w):
    lead = w.shape[:-1]
    w = w.reshape(*lead, HEADS_A // 2, 2, 2, HEAD_DIM_A // 2)
    return jnp.swapaxes(w, -3, -2).reshape(*lead, HEADS_A * HEAD_DIM_A)


def _qkv_proj(x2, mod, g1, w_in, q_gain, k_gain, positions, batch, seq):
    n, d = x2.shape
    tm = TM_QKV
    per_b = seq // tm
    reps = d // HEAD_DIM_A
    w = w_in.reshape(d, N_GROUPS, 3, d)
    w = jnp.concatenate([_split_half_columns(w[:, :, :2]), w[:, :, 2:]], axis=2).reshape(d, 3 * N_GROUPS * d)
    gains = []
    for g in range(N_GROUPS):
        gains += [_split_half_columns(jnp.tile(q_gain[g], reps)) * (HEAD_DIM_A ** -0.5),
                  _split_half_columns(jnp.tile(k_gain[g], reps)), jnp.ones((d,), F32)]
    gains = jnp.stack(gains).reshape(3 * N_GROUPS, 1, d)
    orders = [_stream_order(tm, dil) for _, dil in DILATIONS]
    pos_tiles = positions.reshape(n // tm, tm)
    cos_t, sin_t = _rope_tables(jnp.stack([pos_tiles[:, o] for o in orders]))
    cos_t, sin_t = cos_t.reshape(N_GROUPS, n, LANES), sin_t.reshape(N_GROUPS, n, LANES)
    perms = [(o[:, None] == jnp.arange(tm)[None, :]).astype(BF16) for o in orders[1:]]
    lane = jnp.arange(256)
    head_of = (lane // LANES) * 2 + (lane // 32) % 2
    bd = (head_of[:, None] == head_of[None, :]).astype(BF16)
    dils = [DILATIONS[jj // 3][1] for jj in range(3 * N_GROUPS)]
    const = lambda shape: pl.BlockSpec(shape, lambda i: (0,) * len(shape))
    flat = pl.pallas_call(
        _qkv_kernel,
        grid=(n // tm,),
        in_specs=[
            pl.BlockSpec((tm, d), lambda i: (i, 0)),
            pl.BlockSpec((1, 6, d), lambda i: (i // per_b, 0, 0)),
            const((1, d)),
            pl.BlockSpec(memory_space=pl.ANY),
            const((3 * N_GROUPS, 1, d)),
            pl.BlockSpec((N_GROUPS, tm, LANES), lambda i: (0, i, 0)),
            pl.BlockSpec((N_GROUPS, tm, LANES), lambda i: (0, i, 0)),
            const((256, 256)),
            const((tm, tm)),
            const((tm, tm)),
        ],
        out_specs=[pl.BlockSpec((1, dil, tm // dil, d), lambda i: (i // per_b, 0, i % per_b, 0)) for dil in dils],
        out_shape=[jax.ShapeDtypeStruct((batch, dil, seq // dil, d), BF16) for dil in dils],
        scratch_shapes=[pltpu.VMEM((d, 3 * N_GROUPS * d), BF16), pltpu.SemaphoreType.DMA(())],
        compiler_params=_params("arbitrary"),
        name="qkv_proj",
    )(x2, mod, g1, w.astype(BF16), gains, cos_t, sin_t, bd, *perms)
    return [flat[3 * g:3 * g + 3] for g in range(N_GROUPS)]


def _attn_kernel(q_ref, kp_ref, kc_ref, vp_ref, vc_ref, o_ref, lse_ref, *, steps):
    q_ref, kp_ref, kc_ref, vp_ref, vc_ref, o_ref, lse_ref = (
        r.at[0, 0] for r in (q_ref, kp_ref, kc_ref, vp_ref, vc_ref, o_ref, lse_ref))
    n = pl.program_id(2)
    row = lax.broadcasted_iota(jnp.int32, (BAND, 2 * BAND), 0)
    col = lax.broadcasted_iota(jnp.int32, (BAND, 2 * BAND), 1)
    delta = row + BAND - col
    in_band = (delta >= 0) & (delta <= steps)
    lane = lax.broadcasted_iota(jnp.int32, (BAND, LANES), 1)
    first_head = lane < HEAD_DIM_A
    even_head = (lane & (HEAD_DIM_A // 2)) == 0
    for sub in range(q_ref.shape[0] // BAND):
        rows = slice(sub * BAND, (sub + 1) * BAND)
        if sub == 0:
            k_prev, v_prev = kp_ref, vp_ref
            valid = in_band & ((col >= BAND) | (n > 0))
        else:
            k_prev, v_prev = kc_ref.at[(sub - 1) * BAND:sub * BAND], vc_ref.at[(sub - 1) * BAND:sub * BAND]
            valid = in_band
        lse_tile = jnp.zeros((BAND, LANES), F32)
        for p in range(HEADS_A // 2):
            sl = slice(p * LANES, (p + 1) * LANES)
            qp = q_ref[rows, sl]
            kcat = jnp.concatenate([k_prev[:, sl], kc_ref[rows, sl]], axis=0)
            vcat = jnp.concatenate([v_prev[:, sl], vc_ref[rows, sl]], axis=0)
            outs = []
            for hh in range(2):
                keep = even_head if hh == 0 else jnp.logical_not(even_head)
                qm = jnp.where(keep, qp, jnp.zeros_like(qp))
                s = lax.dot_general(qm, kcat, (((1,), (1,)), ((), ())), preferred_element_type=F32)
                s = jnp.where(valid, s, NEG_INF)
                m = jnp.max(s, axis=-1, keepdims=True)
                pexp = jnp.exp(s - m)
                den = jnp.sum(pexp, axis=-1, keepdims=True)
                o = jnp.dot(pexp.astype(BF16), vcat, preferred_element_type=F32) / den
                outs.append(o)
                lse_tile = jnp.where(lane == (2 * p + hh), m + jnp.log(den), lse_tile)
            o_ref[rows, sl] = jnp.where(first_head, outs[0], outs[1]).astype(BF16)
        lse_ref[rows, :] = lse_tile


def _dilated_attention(q, k, v, g):
    window, dil = DILATIONS[g]
    batch, _, l, d = q.shape
    qb = ATTN_ROWS
    cur = pl.BlockSpec((1, 1, qb, d), lambda b, r, i: (b, r, i, 0))
    prev = pl.BlockSpec((1, 1, BAND, d), lambda b, r, i: (b, r, jnp.maximum(i * (qb // BAND) - 1, 0), 0))
    return pl.pallas_call(
        functools.partial(_attn_kernel, steps=window // dil),
        grid=(batch, dil, l // qb),
        in_specs=[cur, prev, cur, prev, cur],
        out_specs=[cur, pl.BlockSpec((1, 1, qb, LANES), lambda b, r, i: (b, r, i, 0))],
        out_shape=[
            jax.ShapeDtypeStruct((batch, dil, l, d), BF16),
            jax.ShapeDtypeStruct((batch, dil, l, LANES), F32),
        ],
        compiler_params=_params("parallel", "parallel", "arbitrary"),
        name=f"dilated_attn_g{g}",
    )(q, k, k, v, v)


def _split3(v):
    hi = v.astype(BF16)
    rem = v - hi.astype(F32)
    mid = rem.astype(BF16)
    return hi, mid, (rem - mid.astype(F32)).astype(BF16)


def _dot3(mat_bf, v):
    return sum(jnp.dot(mat_bf, part, preferred_element_type=F32) for part in _split3(v))


def _merge_proj_kernel(o0_ref, o1_ref, o2_ref, l0_ref, l1_ref, l2_ref, p1_ref, p2_ref, x_ref, mod_ref, e_ref,
                       w_ref, out_ref):
    tm, d = x_ref.shape
    perms = (None, p1_ref, p2_ref)
    lses, outs = [], []
    for perm, l_ref, o_ref in zip(perms, (l0_ref, l1_ref, l2_ref), (o0_ref, o1_ref, o2_ref)):
        lse = l_ref[0].reshape(tm, LANES)
        o = o_ref[0].reshape(tm, d)
        if perm is None:
            lses.append(lse)
            outs.append(o.astype(F32))
        else:
            lses.append(_dot3(perm[...], lse))
            outs.append(jnp.dot(perm[...], o, preferred_element_type=F32))
    m = jnp.maximum(jnp.maximum(lses[0], lses[1]), lses[2])
    exps = [jnp.exp(l - m) for l in lses]
    inv = 1.0 / (exps[0] + exps[1] + exps[2])
    acc = None
    for e, o in zip(exps, outs):
        w = e * inv
        w_hi = w.astype(BF16)
        w_lo = (w - w_hi.astype(F32)).astype(BF16)
        wide = (jnp.dot(w_hi, e_ref[...], preferred_element_type=F32)
                + jnp.dot(w_lo, e_ref[...], preferred_element_type=F32))
        acc = wide * o if acc is None else acc + wide * o
    y = jnp.dot(acc.astype(BF16), w_ref[...], preferred_element_type=F32)
    out_ref[...] = x_ref[...] + mod_ref[0, 2:3, :] * y


def _merge_proj(outs, lses, x2, mod, w_out_bf, seq):
    n, d = x2.shape
    tm = TM_PROJ
    per_b = seq // tm
    expand = (jnp.arange(LANES)[:, None] == jnp.arange(d)[None, :] // HEAD_DIM_A).astype(BF16)
    tok = jnp.arange(tm)
    perms = [(((tok % dil) * (tm // dil) + tok // dil)[:, None] == tok[None, :]).astype(BF16)
             for _, dil in DILATIONS[1:]]
    slab = lambda dil, width: pl.BlockSpec((1, dil, tm // dil, width), lambda i: (i // per_b, 0, i % per_b, 0))
    const = lambda shape: pl.BlockSpec(shape, lambda i: (0, 0))
    return pl.pallas_call(
        _merge_proj_kernel,
        grid=(n // tm,),
        in_specs=[slab(dil, d) for _, dil in DILATIONS] + [slab(dil, LANES) for _, dil in DILATIONS] + [
            const((tm, tm)),
            const((tm, tm)),
            pl.BlockSpec((tm, d), lambda i: (i, 0)),
            pl.BlockSpec((1, 6, d), lambda i: (i // per_b, 0, 0)),
            const((LANES, d)),
            const((d, d)),
        ],
        out_specs=pl.BlockSpec((tm, d), lambda i: (i, 0)),
        out_shape=jax.ShapeDtypeStruct((n, d), F32),
        compiler_params=_params("parallel"),
        name="merge_out_proj",
    )(*outs, *lses, *perms, x2, mod, expand, w_out_bf)


def _gla_in_kernel(x_ref, mod_ref, g_ref, w_ref, o_ref):
    h = _norm_mod(x_ref[...], g_ref[...], mod_ref[0, 0:1, :], mod_ref[0, 1:2, :])
    o_ref[...] = jnp.dot(h.astype(BF16), w_ref[...], preferred_element_type=F32).astype(BF16)


def _gla_in_proj(x2, mod, g1, w_bf, seq):
    n, d = x2.shape
    ncol = w_bf.shape[1]
    tm = TM_PROJ
    per_b = seq // tm
    return pl.pallas_call(
        _gla_in_kernel,
        grid=(n // tm,),
        in_specs=[
            pl.BlockSpec((tm, d), lambda i: (i, 0)),
            pl.BlockSpec((1, 6, d), lambda i: (i // per_b, 0, 0)),
            pl.BlockSpec((1, d), lambda i: (0, 0)),
            pl.BlockSpec((d, ncol), lambda i: (0, 0)),
        ],
        out_specs=pl.BlockSpec((tm, ncol), lambda i: (i, 0)),
        out_shape=jax.ShapeDtypeStruct((n, ncol), BF16),
        compiler_params=_params("parallel"),
        name="gla_in_proj",
    )(x2, mod, g1, w_bf)


def _gla_kernel(q_ref, k_ref, v_ref, r_ref, a_ref, wg_ref, gb_ref, og_ref, tri_ref, x_ref, mod_ref, wo_ref,
                out_ref, st_ref, o_scr, la_scr):
    t = pl.program_id(1)
    c = GLA_CHUNK

    @pl.when(t == 0)
    def _():
        st_ref[...] = jnp.zeros_like(st_ref)

    g = jnp.dot(a_ref[0], wg_ref[...], preferred_element_type=F32) + gb_ref[...]
    la_scr[...] = (jnp.minimum(g, 0.0) - jnp.log(1.0 + jnp.exp(-jnp.abs(g)))) * (1.0 / GATE_TAU)

    rr = lax.broadcasted_iota(jnp.int32, (c, c), 0)
    cc = lax.broadcasted_iota(jnp.int32, (c, c), 1)
    causal = cc <= rr

    def chunk(ci, carry):
        rows = pl.ds(pl.multiple_of(ci * c, c), c)
        la = la_scr[rows, :]
        b = _dot3(tri_ref[...], la)
        b_last = b[c - 1:c, :]
        q = q_ref[0, rows, :].astype(F32) * (KEY_DIM_B ** -0.5)
        k = k_ref[0, rows, :].astype(F32)
        q_dec = (q * jnp.exp(b)).astype(BF16)
        k_inv = (k * jnp.exp(-b)).astype(BF16)
        k_dec = (k * jnp.exp(b_last - b)).astype(BF16)
        decay = jnp.exp(b_last)
        v = v_ref[0, rows, :]
        for h in range(HEADS_B):
            ks = slice(h * KEY_DIM_B, (h + 1) * KEY_DIM_B)
            vs = slice(h * VAL_DIM_B, (h + 1) * VAL_DIM_B)
            qd, ki, kd, vh = q_dec[:, ks], k_inv[:, ks], k_dec[:, ks], v[:, vs]
            att = lax.dot_general(qd, ki, (((1,), (1,)), ((), ())), preferred_element_type=F32)
            att = jnp.where(causal, att, 0.0)
            st = st_ref[h]
            o = (jnp.dot(att.astype(BF16), vh, preferred_element_type=F32)
                 + lax.dot_general(qd, st.astype(BF16), (((1,), (1,)), ((), ())), preferred_element_type=F32))
            o_scr[rows, vs] = o
            st_ref[h] = st * decay[:, ks] + lax.dot_general(vh, kd, (((0,), (0,)), ((), ())),
                                                             preferred_element_type=F32)
        return carry

    lax.fori_loop(0, q_ref.shape[1] // c, chunk, 0, unroll=2)

    r = r_ref[0].astype(F32)
    parts = []
    for h in range(HEADS_B):
        vs = slice(h * VAL_DIM_B, (h + 1) * VAL_DIM_B)
        oh = o_scr[:, vs]
        ms = jnp.mean(oh * oh, axis=-1, keepdims=True)
        rh = r[:, vs]
        parts.append((oh * lax.rsqrt(ms + RMS_EPS) * og_ref[...] * (rh * jax.nn.sigmoid(rh))).astype(BF16))
    y = jnp.dot(jnp.concatenate(parts, axis=1), wo_ref[...], preferred_element_type=F32)
    out_ref[0] = x_ref[0] + mod_ref[0, 2:3, :] * y


def _gla(proj, x3, mod, wg_bf, gate_bias, out_gain, w_out_bf):
    batch, seq, d = x3.shape
    ncol = proj.shape[1]
    t = T_GLA
    hk = HEADS_B * KEY_DIM_B
    hv = HEADS_B * VAL_DIM_B
    p3 = proj.reshape(batch, seq, ncol)
    tri = (jnp.arange(GLA_CHUNK)[None, :] <= jnp.arange(GLA_CHUNK)[:, None]).astype(BF16)
    a_blk = (2 * hk + 2 * hv) // LANES
    const = lambda shape: pl.BlockSpec(shape, lambda b, i: (0,) * len(shape))
    return pl.pallas_call(
        _gla_kernel,
        grid=(batch, seq // t),
        in_specs=[
            pl.BlockSpec((1, t, hk), lambda b, i: (b, i, 0)),
            pl.BlockSpec((1, t, hk), lambda b, i: (b, i, 1)),
            pl.BlockSpec((1, t, hv), lambda b, i: (b, i, (2 * hk) // hv)),
            pl.BlockSpec((1, t, hv), lambda b, i: (b, i, (2 * hk + hv) // hv)),
            pl.BlockSpec((1, t, LANES), lambda b, i: (b, i, a_blk)),
            const((LANES, hk)),
            const((1, hk)),
            const((1, VAL_DIM_B)),
            const((GLA_CHUNK, GLA_CHUNK)),
            pl.BlockSpec((1, t, d), lambda b, i: (b, i, 0)),
            pl.BlockSpec((1, 6, d), lambda b, i: (b, 0, 0)),
            const((hv, d)),
        ],
        out_specs=pl.BlockSpec((1, t, d), lambda b, i: (b, i, 0)),
        out_shape=jax.ShapeDtypeStruct((batch, seq, d), F32),
        scratch_shapes=[
            pltpu.VMEM((HEADS_B, VAL_DIM_B, KEY_DIM_B), F32),
            pltpu.VMEM((t, hv), F32),
            pltpu.VMEM((t, hk), F32),
        ],
        compiler_params=_params("parallel", "arbitrary"),
        name="gla",
    )(p3, p3, p3, p3, p3, wg_bf, gate_bias, out_gain, tri, x3, mod, w_out_bf)


def _router_kernel(x_ref, mod_ref, g_ref, rw_ref, rb_ref, tri_ref, upper_ref, h_ref, info_ref, cnt_ref, tcnt_ref,
                   tcar_ref, carry_ref):
    i = pl.program_id(0)

    @pl.when(i == 0)
    def _():
        carry_ref[...] = jnp.zeros_like(carry_ref)

    h = _norm_mod(x_ref[...], g_ref[...], mod_ref[0, 3:4, :], mod_ref[0, 4:5, :])
    h_ref[...] = h.astype(BF16)
    tm = h.shape[0]
    lane = lax.broadcasted_iota(jnp.int32, (tm, LANES), 1)
    lane_f = lane.astype(F32)
    h_hi = h.astype(BF16)
    h_lo = (h - h_hi.astype(F32)).astype(BF16)
    logits = (jnp.dot(h_hi, rw_ref[0], preferred_element_type=F32) + jnp.dot(h_lo, rw_ref[0], preferred_element_type=F32)
              + jnp.dot(h_hi, rw_ref[1], preferred_element_type=F32) + rb_ref[...])
    work = jnp.where(lane < N_EXPERTS, logits, -jnp.inf)
    picks, vals, idxs = [], [], []
    for _ in range(TOP_K):
        m = jnp.max(work, axis=-1, keepdims=True)
        idx = jnp.min(jnp.where(work == m, lane_f, float(LANES)), axis=-1, keepdims=True)
        pick = lane_f == idx
        work = jnp.where(pick, -jnp.inf, work)
        picks.append(pick)
        vals.append(m)
        idxs.append(idx)
    exps = [jnp.exp(v - vals[0]) for v in vals]
    inv = 1.0 / (exps[0] + exps[1] + exps[2] + exps[3])
    chosen = jnp.zeros((tm, LANES), F32)
    for pick in picks:
        chosen = jnp.where(pick, 1.0, chosen)
    here = jnp.floor((jnp.sum(chosen, axis=0, keepdims=True) + (RUN_ALIGN - 1)) * (1.0 / RUN_ALIGN)) * RUN_ALIGN
    run_start = jnp.dot(jnp.broadcast_to(here, (8, LANES)).astype(BF16), upper_ref[...],
                        preferred_element_type=F32)[0:1, :]
    sorted_row = jnp.dot(tri_ref[...], chosen.astype(BF16), preferred_element_type=F32) + run_start
    info = jnp.zeros((tm, LANES), F32)
    for kk in range(TOP_K):
        row = jnp.sum(jnp.where(picks[kk], sorted_row, 0.0), axis=-1, keepdims=True)
        info = jnp.where(lane == kk, idxs[kk], info)
        info = jnp.where(lane == TOP_K + kk, row, info)
        info = jnp.where(lane == 2 * TOP_K + kk, exps[kk] * inv, info)
    info_ref[...] = info
    tcnt_ref[0] = jnp.broadcast_to(here, tcnt_ref.shape[1:])
    tcar_ref[0] = carry_ref[...]
    total = carry_ref[0:1, :] + here
    carry_ref[...] = jnp.broadcast_to(total, carry_ref.shape)
    cnt_ref[...] = jnp.broadcast_to(total, cnt_ref.shape)


def _router(x2, mod, g2, router_w, router_b, seq):
    n, d = x2.shape
    tm = TM_ROUTE
    per_b = seq // tm
    rw = jnp.zeros((d, LANES), F32).at[:, :N_EXPERTS].set(router_w)
    rw_hi = rw.astype(BF16)
    rw = jnp.stack([rw_hi, (rw - rw_hi.astype(F32)).astype(BF16)])
    rb = jnp.zeros((1, LANES), F32).at[0, :N_EXPERTS].set(router_b)
    tri = (jnp.arange(tm)[None, :] < jnp.arange(tm)[:, None]).astype(BF16)
    upper = (jnp.arange(LANES)[:, None] < jnp.arange(LANES)[None, :]).astype(BF16)
    return pl.pallas_call(
        _router_kernel,
        grid=(n // tm,),
        in_specs=[
            pl.BlockSpec((tm, d), lambda i: (i, 0)),
            pl.BlockSpec((1, 6, d), lambda i: (i // per_b, 0, 0)),
            pl.BlockSpec((1, d), lambda i: (0, 0)),
            pl.BlockSpec((2, d, LANES), lambda i: (0, 0, 0)),
            pl.BlockSpec((1, LANES), lambda i: (0, 0)),
            pl.BlockSpec((tm, tm), lambda i: (0, 0)),
            pl.BlockSpec((LANES, LANES), lambda i: (0, 0)),
        ],
        out_specs=[
            pl.BlockSpec((tm, d), lambda i: (i, 0)),
            pl.BlockSpec((tm, LANES), lambda i: (i, 0)),
            pl.BlockSpec((8, LANES), lambda i: (0, 0)),
            pl.BlockSpec((1, 8, LANES), lambda i: (i, 0, 0)),
            pl.BlockSpec((1, 8, LANES), lambda i: (i, 0, 0)),
        ],
        out_shape=[
            jax.ShapeDtypeStruct((n, d), BF16),
            jax.ShapeDtypeStruct((n, LANES), F32),
            jax.ShapeDtypeStruct((8, LANES), F32),
            jax.ShapeDtypeStruct((n // tm, 8, LANES), F32),
            jax.ShapeDtypeStruct((n // tm, 8, LANES), F32),
        ],
        scratch_shapes=[pltpu.VMEM((8, LANES), F32)],
        compiler_params=_params("arbitrary"),
        name="moe_router",
    )(x2, mod, g2, rw, rb, tri, upper)


def _for_each_run(tile, cnt_ref, lo_ref, base_ref, act):
    bits = [1 << k for k in range(TM_MOVE.bit_length() - 1, RUN_ALIGN.bit_length() - 2, -1)]

    def one(e, carry):
        j = tile * N_EXPERTS + e
        cnt, lo, base = cnt_ref[j], lo_ref[j], base_ref[j]
        off = jnp.int32(0)
        for bit in bits:
            take = (cnt & bit) != 0
            pl.when(take)(functools.partial(act, pl.multiple_of(lo + off, RUN_ALIGN),
                                            pl.multiple_of(base + off, RUN_ALIGN), bit))
            off = off + jnp.where(take, bit, 0)
        return carry

    lax.fori_loop(0, N_EXPERTS, one, 0)


def _dispatch_kernel(seg_end_ref, padded_ref, cnt_ref, lo_ref, base_ref, pos_ref, h_ref, xs_hbm,
                     loc_ref, zero_ref, sems, zero_sem):
    i = pl.program_id(0)
    tm, d = h_ref.shape

    @pl.when(pl.program_id(0) == 0)
    def _():
        zero_ref[...] = jnp.zeros_like(zero_ref)

        def tail(e):
            return pltpu.make_async_copy(
                zero_ref, xs_hbm.at[pl.ds(pl.multiple_of(seg_end_ref[e] - TM_EXPERT, TM_EXPERT), TM_EXPERT)], zero_sem)

        def start(e, carry):
            pl.when(padded_ref[e] > 0)(lambda: tail(e).start())
            return carry

        def wait(e, carry):
            pl.when(padded_ref[e] > 0)(lambda: tail(e).wait())
            return carry

        lax.fori_loop(0, N_EXPERTS, start, 0)
        lax.fori_loop(0, N_EXPERTS, wait, 0)

        def spare(blk):
            return pltpu.make_async_copy(
                zero_ref, xs_hbm.at[pl.ds(pl.multiple_of(blk * TM_EXPERT, TM_EXPERT), TM_EXPERT)], zero_sem)

        used = seg_end_ref[N_EXPERTS - 1] // TM_EXPERT
        total = xs_hbm.shape[0] // TM_EXPERT
        lax.fori_loop(used, total, lambda blk, carry: (spare(blk).start(), carry)[1], 0)
        lax.fori_loop(used, total, lambda blk, carry: (spare(blk).wait(), carry)[1], 0)

    slot = lax.broadcasted_iota(jnp.int32, (RUN_ROWS, tm), 0).astype(F32)
    hit = slot == pos_ref[0:1, :]
    for kk in range(1, TOP_K):
        hit = hit | (slot == pos_ref[kk:kk + 1, :])
    perm = jnp.where(hit, 1.0, 0.0).astype(BF16)
    buf = i % 2
    loc_ref[buf] = _pack_pairs(jnp.dot(perm, h_ref[...], preferred_element_type=F32))

    def copy(which, local_row, global_row, size):
        return pltpu.make_async_copy(loc_ref.at[which, pl.ds(local_row, size)],
                                     xs_hbm.at[pl.ds(global_row, size)], sems.at[which])

    _for_each_run(i, cnt_ref, lo_ref, base_ref, lambda a, b, size: copy(buf, a, b, size).start())

    @pl.when(i > 0)
    def _():
        _for_each_run(i - 1, cnt_ref, lo_ref, base_ref, lambda a, b, size: copy(1 - buf, a, b, size).wait())

    @pl.when(i == pl.num_programs(0) - 1)
    def _():
        _for_each_run(i, cnt_ref, lo_ref, base_ref, lambda a, b, size: copy(buf, a, b, size).wait())


def _dispatch(h, pos_t, seg_end, padded, tile_cnt, tile_lo, tile_base, cap):
    n, d = h.shape
    tm = TM_MOVE
    smem = lambda i, *_: (0, i)
    return pl.pallas_call(
        _dispatch_kernel,
        grid_spec=pltpu.PrefetchScalarGridSpec(
            num_scalar_prefetch=5,
            grid=(n // tm,),
            in_specs=[
                pl.BlockSpec((8, tm), smem),
                pl.BlockSpec((tm, d), lambda i, *_: (i, 0)),
            ],
            out_specs=pl.BlockSpec(memory_space=pl.ANY),
            scratch_shapes=[pltpu.VMEM((2, RUN_ROWS, d // 2), jnp.uint32), pltpu.VMEM((TM_EXPERT, d // 2), jnp.uint32),
                            pltpu.SemaphoreType.DMA((2,)), pltpu.SemaphoreType.DMA(())],
        ),
        out_shape=jax.ShapeDtypeStruct((cap, d // 2), jnp.uint32),
        compiler_params=_params("arbitrary"),
        name="moe_dispatch",
    )(seg_end, padded, tile_cnt, tile_lo, tile_base, pos_t, h)


def _expert_kernel(be_ref, nxt_ref, na_ref, x_ref, wgu_hbm, bgu_ref, wd_hbm, bd_ref, o_ref,
                   gu_stage, d_stage, gu_bf, d_bf, sems, *, layer):
    i = pl.program_id(0)
    live = i < na_ref[0]
    e = be_ref[i]

    def fetch(expert):
        return (pltpu.make_async_copy(wgu_hbm.at[layer, expert], gu_stage, sems.at[0]),
                pltpu.make_async_copy(wd_hbm.at[layer, expert], d_stage, sems.at[1]))

    @pl.when(i == 0)
    def _():
        for c in fetch(e):
            c.start()

    @pl.when(live & ((i == 0) | (e != be_ref[jnp.maximum(i - 1, 0)])))
    def _():
        for c in fetch(e):
            c.wait()
        gu_bf[...] = gu_stage[...].astype(BF16)
        d_bf[...] = d_stage[...].astype(BF16)

        @pl.when(nxt_ref[i] >= 0)
        def _():
            for c in fetch(nxt_ref[i]):
                c.start()

    @pl.when(jnp.logical_not(live))
    def _():
        o_ref[...] = jnp.zeros_like(o_ref)

    @pl.when(live)
    def _():
        f = d_bf.shape[0]
        xb = _unpack_pairs(x_ref[...]).astype(BF16)
        gu = jnp.dot(xb, gu_bf[...], preferred_element_type=F32) + bgu_ref[0]
        gate = jnp.minimum(gu[:, :f], SWIGLU_LIMIT)
        up = jnp.clip(gu[:, f:], -SWIGLU_LIMIT, SWIGLU_LIMIT)
        act = (up + 1.0) * (gate * jax.nn.sigmoid(SWIGLU_ALPHA * gate))
        o_ref[...] = _pack_pairs(jnp.dot(act.astype(BF16), d_bf[...], preferred_element_type=F32) + bd_ref[0])


def _experts(xs, block_e, next_e, n_active, w_gu, b_gu, w_down, b_down, layer):
    cap, w = xs.shape
    _, e, d, f2 = w_gu.shape
    f = f2 // 2
    tm = TM_EXPERT
    nb = cap // tm
    return pl.pallas_call(
        functools.partial(_expert_kernel, layer=layer),
        grid_spec=pltpu.PrefetchScalarGridSpec(
            num_scalar_prefetch=3,
            grid=(nb,),
            in_specs=[
                pl.BlockSpec((tm, w), lambda i, be, nx, na: (jnp.minimum(i, na[0] - 1), 0)),
                pl.BlockSpec(memory_space=pl.ANY),
                pl.BlockSpec((1, 1, f2), lambda i, be, nx, na: (be[i], 0, 0)),
                pl.BlockSpec(memory_space=pl.ANY),
                pl.BlockSpec((1, 1, d), lambda i, be, nx, na: (be[i], 0, 0)),
            ],
            out_specs=pl.BlockSpec((tm, w), lambda i, be, nx, na: (i, 0)),
            scratch_shapes=[pltpu.VMEM((d, f2), F32), pltpu.VMEM((f, d), F32), pltpu.VMEM((d, f2), BF16),
                            pltpu.VMEM((f, d), BF16), pltpu.SemaphoreType.DMA((2,))],
        ),
        out_shape=jax.ShapeDtypeStruct((cap, w), xs.dtype),
        compiler_params=_params("arbitrary"),
        name="moe_experts",
    )(block_e, next_e, n_active, xs, w_gu, b_gu.reshape(e, 1, f2), w_down, b_down.reshape(e, 1, d))


def _combine_kernel(cnt_ref, lo_ref, base_ref, info_ref, x_ref, mod_ref, ys_hbm, out_ref, loc_ref, sems):
    i = pl.program_id(0)
    tm = x_ref.shape[0]
    buf = i % 2

    def copy(which, local_row, global_row, size):
        return pltpu.make_async_copy(ys_hbm.at[pl.ds(global_row, size)],
                                     loc_ref.at[which, pl.ds(local_row, size)], sems.at[which])

    def fetch(tile, which):
        _for_each_run(tile, cnt_ref, lo_ref, base_ref, lambda a, b, size: copy(which, a, b, size).start())

    @pl.when(i == 0)
    def _():
        loc_ref[...] = jnp.zeros_like(loc_ref)
        fetch(i, buf)

    @pl.when(i + 1 < pl.num_programs(0))
    def _():
        fetch(i + 1, 1 - buf)

    _for_each_run(i, cnt_ref, lo_ref, base_ref, lambda a, b, size: copy(buf, a, b, size).wait())

    info = info_ref[...]
    slot = lax.broadcasted_iota(jnp.int32, (tm, RUN_ROWS), 1).astype(F32)
    mix = jnp.zeros((tm, RUN_ROWS), F32)
    for kk in range(TOP_K):
        mix = jnp.where(slot == info[:, TOP_K + kk:TOP_K + kk + 1], info[:, 2 * TOP_K + kk:2 * TOP_K + kk + 1], mix)
    rows = _unpack_pairs(loc_ref[buf]).astype(BF16)
    y = jnp.dot(mix.astype(BF16), rows, preferred_element_type=F32)
    out_ref[...] = x_ref[...] + mod_ref[0, 5:6, :] * y


def _combine(ys, info, tile_cnt, tile_lo, tile_base, x2, mod, seq):
    n, d = x2.shape
    tm = TM_MOVE
    per_b = seq // tm
    return pl.pallas_call(
        _combine_kernel,
        grid_spec=pltpu.PrefetchScalarGridSpec(
            num_scalar_prefetch=3,
            grid=(n // tm,),
            in_specs=[
                pl.BlockSpec((tm, LANES), lambda i, *_: (i, 0)),
                pl.BlockSpec((tm, d), lambda i, *_: (i, 0)),
                pl.BlockSpec((1, 6, d), lambda i, *_: (i // per_b, 0, 0)),
                pl.BlockSpec(memory_space=pl.ANY),
            ],
            out_specs=pl.BlockSpec((tm, d), lambda i, *_: (i, 0)),
            scratch_shapes=[pltpu.VMEM((2, RUN_ROWS, ys.shape[1]), ys.dtype), pltpu.SemaphoreType.DMA((2,))],
        ),
        out_shape=jax.ShapeDtypeStruct((n, d), F32),
        compiler_params=_params("arbitrary"),
        name="moe_combine",
    )(tile_cnt, tile_lo, tile_base, info, x2, mod, ys)


def _moe(x2, mod, g2, router_w, router_b, w_gu, b_gu, w_down, b_down, layer, seq):
    n, d = x2.shape
    assert TM_MOVE == TM_ROUTE, "dispatch / combine tiles reuse the router's per-tile expert counts"
    h, info, cnt, tcnt, tcar = _router(x2, mod, g2, router_w, router_b, seq)
    counts = cnt[0, :N_EXPERTS].astype(jnp.int32)
    padded = (counts + TM_EXPERT - 1) // TM_EXPERT * TM_EXPERT
    seg_end = jnp.cumsum(padded)
    seg_start = seg_end - padded
    worst = n * TOP_K + N_EXPERTS * (n // TM_MOVE) * (RUN_ALIGN - 1)
    nb = -(-worst // TM_EXPERT) + N_EXPERTS
    cap = nb * TM_EXPERT
    block_start = jnp.arange(nb, dtype=jnp.int32) * TM_EXPERT
    block_e = jnp.minimum(jnp.sum(seg_end[None, :] <= block_start[:, None], axis=1), N_EXPERTS - 1).astype(jnp.int32)
    n_active = (seg_end[-1:] // TM_EXPERT).astype(jnp.int32)
    tile_cnt = tcnt[:, 0, :N_EXPERTS].astype(jnp.int32)
    tile_before = tcar[:, 0, :N_EXPERTS].astype(jnp.int32)
    tile_lo = jnp.cumsum(tile_cnt, axis=1) - tile_cnt
    tile_base = seg_start[None, :] + tile_before
    pos_t = info[:, TOP_K:TOP_K + 8].T
    tile_cnt, tile_lo, tile_base = (t.reshape(-1).astype(jnp.int32) for t in (tile_cnt, tile_lo, tile_base))
    ids = jnp.arange(N_EXPERTS, dtype=jnp.int32)
    later = (ids[None, :] > ids[:, None]) & (padded[None, :] > 0)
    next_of = jnp.min(jnp.where(later, ids[None, :], N_EXPERTS), axis=1)
    next_e = jnp.where(next_of < N_EXPERTS, next_of, -1).astype(jnp.int32)[block_e]
    xs = _dispatch(h, pos_t, seg_end.astype(jnp.int32), padded.astype(jnp.int32), tile_cnt, tile_lo, tile_base, cap)
    ys = _experts(xs, block_e, next_e, n_active, w_gu, b_gu, w_down, b_down, layer)
    return _combine(ys, info, tile_cnt, tile_lo, tile_base, x2, mod, seq)


def kernel(x, c, positions, ada_w, ada_b, norm1_g, norm2_g, a_w_in, a_q_gain, a_k_gain, a_w_out, b_w_in,
           b_w_gate_up, b_gate_bias, b_out_gain, b_w_out, router_w, router_b, moe_w_gu, moe_b_gu, moe_w_down,
           moe_b_down):
    batch, seq, d = x.shape
    depth = ada_w.shape[0]
    n = batch * seq
    mods = _adaln(c, ada_w, ada_b).reshape(depth, batch, 6, d)
    x2 = x.reshape(n, d)
    for layer in range(depth):
        mod = mods[layer]
        j = layer // 2
        g1 = norm1_g[layer].reshape(1, d)
        g2 = norm2_g[layer].reshape(1, d)
        if layer % 2 == 0:
            qkv = _qkv_proj(x2, mod, g1, a_w_in[j], a_q_gain[j], a_k_gain[j], positions, batch, seq)
            outs, lses = zip(*[_dilated_attention(*qkv[g], g) for g in range(N_GROUPS)])
            x2 = _merge_proj(outs, lses, x2, mod, a_w_out[j].astype(BF16), seq)
        else:
            hk = HEADS_B * KEY_DIM_B
            hv = HEADS_B * VAL_DIM_B
            width = 2 * hk + 2 * hv
            w_in = jnp.zeros((d, width + LANES), BF16).at[:, :width + GATE_RANK].set(b_w_in[j].astype(BF16))
            wg = jnp.zeros((LANES, hk), BF16).at[:GATE_RANK].set(b_w_gate_up[j].astype(BF16))
            proj = _gla_in_proj(x2, mod, g1, w_in, seq)
            x3 = _gla(proj, x2.reshape(batch, seq, d), mod, wg, b_gate_bias[j].reshape(1, hk),
                      b_out_gain[j].reshape(1, VAL_DIM_B), b_w_out[j].astype(BF16))
            x2 = x3.reshape(n, d)
        x2 = _moe(x2, mod, g2, router_w[layer], router_b[layer], moe_w_gu, moe_b_gu[layer],
                  moe_w_down, moe_b_down[layer], layer, seq)
    return x2.reshape(batch, seq, d)
```

```python
import functools

import jax
import jax.numpy as jnp
from jax import lax
from jax.experimental import pallas as pl
from jax.experimental.pallas import tpu as pltpu

F32 = jnp.float32
BF16 = jnp.bfloat16
HIGHEST = lax.Precision.HIGHEST

RMS_EPS = 1e-6
DILATIONS = ((128, 1), (512, 4), (2048, 16))
N_GROUPS = 3
HEADS_A = 16
HEAD_DIM_A = 64
BAND = 128
ROPE_THETA = 10000.0
NEG_INF = -1e30
HEADS_B = 4
KEY_DIM_B = 128
VAL_DIM_B = 256
GATE_RANK = 16
GATE_TAU = 16.0
GLA_CHUNK = 64
N_EXPERTS = 32
TOP_K = 4
SWIGLU_LIMIT = 7.0
SWIGLU_ALPHA = 1.702

LANES = 128
VMEM_LIMIT = 56 * 1024 * 1024

TM_PROJ = 512
TM_QKV = 256
TM_ROUTE = 512
TM_EXPERT = 512
TM_MOVE = 512
ATTN_ROWS = 512
RUN_ALIGN = 8
RUN_ROWS = TOP_K * TM_MOVE + N_EXPERTS * RUN_ALIGN
T_GLA = 512
GLA_SEQS = 1


def _params(*sem):
    return pltpu.CompilerParams(dimension_semantics=sem, vmem_limit_bytes=VMEM_LIMIT)


def _norm_mod(x, gain, shift, scale):
    ms = jnp.mean(x * x, axis=-1, keepdims=True)
    y = x * lax.rsqrt(ms + RMS_EPS) * gain
    return y * (1.0 + scale) + shift


def _pack_pairs(v):
    k = v.shape[1] // 2
    bits = lax.bitcast_convert_type(v.astype(BF16).astype(F32), jnp.uint32)
    return (bits[:, :k] >> 16) | bits[:, k:]


def _unpack_pairs(w):
    lo = lax.bitcast_convert_type(w << 16, F32)
    hi = lax.bitcast_convert_type(w & jnp.uint32(0xFFFF0000), F32)
    return jnp.concatenate([lo, hi], axis=1)


def _adaln_kernel(c_ref, w_ref, b_ref, o_ref):
    c = c_ref[...]
    cond = c * jax.nn.sigmoid(c)
    o_ref[0] = jnp.dot(cond, w_ref[0], precision=HIGHEST, preferred_element_type=F32) + b_ref[0]


def _adaln(c, ada_w, ada_b):
    depth, d, n6 = ada_w.shape
    b = c.shape[0]
    tn = 1536
    return pl.pallas_call(
        _adaln_kernel,
        grid=(depth, n6 // tn),
        in_specs=[
            pl.BlockSpec((b, d), lambda l, j: (0, 0)),
            pl.BlockSpec((1, d, tn), lambda l, j: (l, 0, j)),
            pl.BlockSpec((1, 1, tn), lambda l, j: (l, 0, j)),
        ],
        out_specs=pl.BlockSpec((1, b, tn), lambda l, j: (l, 0, j)),
        out_shape=jax.ShapeDtypeStruct((depth, b, n6), F32),
        compiler_params=_params("parallel", "parallel"),
        name="adaln",
    )(c, ada_w, ada_b.reshape(depth, 1, n6))


def _rope_kernel(pos_ref, cos_ref, sin_ref):
    pos = pos_ref[...].astype(F32)
    lane = lax.broadcasted_iota(jnp.int32, (1, LANES), 1)
    half = HEAD_DIM_A // 2
    fidx = (lane & (half - 1)).astype(F32)
    inv_freq = jnp.power(jnp.full((1, LANES), ROPE_THETA, F32), -(fidx / half))
    ang = pos * inv_freq
    cos_ref[...] = jnp.cos(ang)
    s = jnp.sin(ang)
    sin_ref[...] = jnp.where(lane < LANES // 2, -s, s)


def _rope_tables(positions):
    n = positions.size
    tm = 1024
    return pl.pallas_call(
        _rope_kernel,
        grid=(n // tm,),
        in_specs=[pl.BlockSpec((tm, 1), lambda i: (i, 0))],
        out_specs=[pl.BlockSpec((tm, LANES), lambda i: (i, 0))] * 2,
        out_shape=[jax.ShapeDtypeStruct((n, LANES), F32)] * 2,
        compiler_params=_params("parallel"),
        name="rope_tables",
    )(positions.reshape(n, 1))


def _stream_order(tm, dil):
    s = jnp.arange(tm)
    return (s % (tm // dil)) * dil + s // (tm // dil)


def _qkv_kernel(x_ref, mod_ref, g_ref, w_hbm, gain_ref, cos_ref, sin_ref, bd_ref, p1_ref, p2_ref, *rest):
    o_refs, (w_ref, w_sem) = rest[:3 * N_GROUPS], rest[3 * N_GROUPS:]
    tm, d = x_ref.shape

    @pl.when(pl.program_id(0) == 0)
    def _():
        copy = pltpu.make_async_copy(w_hbm, w_ref, w_sem)
        copy.start()
        copy.wait()

    h = _norm_mod(x_ref[...], g_ref[...], mod_ref[0, 0:1, :], mod_ref[0, 1:2, :]).astype(BF16)
    hs = [h] + [jnp.dot(p[...], h, preferred_element_type=F32).astype(BF16) for p in (p1_ref, p2_ref)]
    for jj in range(3 * N_GROUPS):
        g, comp = divmod(jj, 3)
        dil = DILATIONS[g][1]
        rows = tm // dil
        out = o_refs[jj]
        acc = jnp.dot(hs[g], w_ref[:, jj * d:(jj + 1) * d], preferred_element_type=F32)
        if comp == 2:
            val = acc.astype(BF16)
            for r in range(dil):
                out[0, r] = val[r * rows:(r + 1) * rows]
            continue
        gain = gain_ref[jj]
        cos = cos_ref[g]
        sin = sin_ref[g]
        for cc in range(d // 256):
            a = acc[:, cc * 256:(cc + 1) * 256]
            ss = jnp.dot((a * a).astype(BF16), bd_ref[...], preferred_element_type=F32)
            qn = a * lax.rsqrt(ss * (1.0 / HEAD_DIM_A) + RMS_EPS) * gain[:, cc * 256:(cc + 1) * 256]
            for hh in range(2):
                xx = qn[:, hh * LANES:(hh + 1) * LANES]
                val = (xx * cos + pltpu.roll(xx, 64, 1) * sin).astype(BF16)
                c1 = cc * 256 + hh * LANES
                for r in range(dil):
                    out[0, r, :, c1:c1 + LANES] = val[r * rows:(r + 1) * rows]


def _split_half_columns(w):
    lead = w.shape[:-1]
    w = w.reshape(*lead, HEADS_A // 2, 2, 2, HEAD_DIM_A // 2)
    return jnp.swapaxes(w, -3, -2).reshape(*lead, HEADS_A * HEAD_DIM_A)


def _qkv_proj(x2, mod, g1, w_in, q_gain, k_gain, positions, batch, seq):
    n, d = x2.shape
    tm = TM_QKV
    per_b = seq // tm
    reps = d // HEAD_DIM_A
    w = w_in.reshape(d, N_GROUPS, 3, d)
    w = jnp.concatenate([_split_half_columns(w[:, :, :2]), w[:, :, 2:]], axis=2).reshape(d, 3 * N_GROUPS * d)
    gains = []
    for g in range(N_GROUPS):
        gains += [_split_half_columns(jnp.tile(q_gain[g], reps)) * (HEAD_DIM_A ** -0.5),
                  _split_half_columns(jnp.tile(k_gain[g], reps)), jnp.ones((d,), F32)]
    gains = jnp.stack(gains).reshape(3 * N_GROUPS, 1, d)
    orders = [_stream_order(tm, dil) for _, dil in DILATIONS]
    pos_tiles = positions.reshape(n // tm, tm)
    cos_t, sin_t = _rope_tables(jnp.stack([pos_tiles[:, o] for o in orders]))
    cos_t, sin_t = cos_t.reshape(N_GROUPS, n, LANES), sin_t.reshape(N_GROUPS, n, LANES)
    perms = [(o[:, None] == jnp.arange(tm)[None, :]).astype(BF16) for o in orders[1:]]
    lane = jnp.arange(256)
    head_of = (lane // LANES) * 2 + (lane // 32) % 2
    bd = (head_of[:, None] == head_of[None, :]).astype(BF16)
    dils = [DILATIONS[jj // 3][1] for jj in range(3 * N_GROUPS)]
    const = lambda shape: pl.BlockSpec(shape, lambda i: (0,) * len(shape))
    flat = pl.pallas_call(
        _qkv_kernel,
        grid=(n // tm,),
        in_specs=[
            pl.BlockSpec((tm, d), lambda i: (i, 0)),
            pl.BlockSpec((1, 6, d), lambda i: (i // per_b, 0, 0)),
            const((1, d)),
            pl.BlockSpec(memory_space=pl.ANY),
            const((3 * N_GROUPS, 1, d)),
            pl.BlockSpec((N_GROUPS, tm, LANES), lambda i: (0, i, 0)),
            pl.BlockSpec((N_GROUPS, tm, LANES), lambda i: (0, i, 0)),
            const((256, 256)),
            const((tm, tm)),
            const((tm, tm)),
        ],
        out_specs=[pl.BlockSpec((1, dil, tm // dil, d), lambda i: (i // per_b, 0, i % per_b, 0)) for dil in dils],
        out_shape=[jax.ShapeDtypeStruct((batch, dil, seq // dil, d), BF16) for dil in dils],
        scratch_shapes=[pltpu.VMEM((d, 3 * N_GROUPS * d), BF16), pltpu.SemaphoreType.DMA(())],
        compiler_params=_params("arbitrary"),
        name="qkv_proj",
    )(x2, mod, g1, w.astype(BF16), gains, cos_t, sin_t, bd, *perms)
    return [flat[3 * g:3 * g + 3] for g in range(N_GROUPS)]


def _attn_kernel(q_ref, kp_ref, kc_ref, vp_ref, vc_ref, o_ref, lse_ref, *, steps):
    q_ref, kp_ref, kc_ref, vp_ref, vc_ref, o_ref, lse_ref = (
        r.at[0, 0] for r in (q_ref, kp_ref, kc_ref, vp_ref, vc_ref, o_ref, lse_ref))
    n = pl.program_id(2)
    row = lax.broadcasted_iota(jnp.int32, (BAND, 2 * BAND), 0)
    col = lax.broadcasted_iota(jnp.int32, (BAND, 2 * BAND), 1)
    delta = row + BAND - col
    in_band = (delta >= 0) & (delta <= steps)
    lane = lax.broadcasted_iota(jnp.int32, (BAND, LANES), 1)
    first_head = lane < HEAD_DIM_A
    even_head = (lane & (HEAD_DIM_A // 2)) == 0
    for sub in range(q_ref.shape[0] // BAND):
        rows = slice(sub * BAND, (sub + 1) * BAND)
        if sub == 0:
            k_prev, v_prev = kp_ref, vp_ref
            valid = in_band & ((col >= BAND) | (n > 0))
        else:
            k_prev, v_prev = kc_ref.at[(sub - 1) * BAND:sub * BAND], vc_ref.at[(sub - 1) * BAND:sub * BAND]
            valid = in_band
        lse_tile = jnp.zeros((BAND, LANES), F32)
        for p in range(HEADS_A // 2):
            sl = slice(p * LANES, (p + 1) * LANES)
            qp = q_ref[rows, sl]
            kcat = jnp.concatenate([k_prev[:, sl], kc_ref[rows, sl]], axis=0)
            vcat = jnp.concatenate([v_prev[:, sl], vc_ref[rows, sl]], axis=0)
            outs = []
            for hh in range(2):
                keep = even_head if hh == 0 else jnp.logical_not(even_head)
                qm = jnp.where(keep, qp, jnp.zeros_like(qp))
                s = lax.dot_general(qm, kcat, (((1,), (1,)), ((), ())), preferred_element_type=F32)
                s = jnp.where(valid, s, NEG_INF)
                m = jnp.max(s, axis=-1, keepdims=True)
                pexp = jnp.exp(s - m)
                den = jnp.sum(pexp, axis=-1, keepdims=True)
                o = jnp.dot(pexp.astype(BF16), vcat, preferred_element_type=F32) / den
                outs.append(o)
                lse_tile = jnp.where(lane == (2 * p + hh), m + jnp.log(den), lse_tile)
            o_ref[rows, sl] = jnp.where(first_head, outs[0], outs[1]).astype(BF16)
        lse_ref[rows, :] = lse_tile


def _dilated_attention(q, k, v, g):
    window, dil = DILATIONS[g]
    batch, _, l, d = q.shape
    qb = min(ATTN_ROWS, l)
    cur = pl.BlockSpec((1, 1, qb, d), lambda b, r, i: (b, r, i, 0))
    prev = pl.BlockSpec((1, 1, BAND, d), lambda b, r, i: (b, r, jnp.maximum(i * (qb // BAND) - 1, 0), 0))
    return pl.pallas_call(
        functools.partial(_attn_kernel, steps=window // dil),
        grid=(batch, dil, l // qb),
        in_specs=[cur, prev, cur, prev, cur],
        out_specs=[cur, pl.BlockSpec((1, 1, qb, LANES), lambda b, r, i: (b, r, i, 0))],
        out_shape=[
            jax.ShapeDtypeStruct((batch, dil, l, d), BF16),
            jax.ShapeDtypeStruct((batch, dil, l, LANES), F32),
        ],
        compiler_params=_params("parallel", "parallel", "arbitrary"),
        name=f"dilated_attn_g{g}",
    )(q, k, k, v, v)


def _split3(v):
    hi = v.astype(BF16)
    rem = v - hi.astype(F32)
    mid = rem.astype(BF16)
    return hi, mid, (rem - mid.astype(F32)).astype(BF16)


def _dot3(mat_bf, v):
    return sum(jnp.dot(mat_bf, part, preferred_element_type=F32) for part in _split3(v))


def _merge_proj_kernel(o0_ref, o1_ref, o2_ref, l0_ref, l1_ref, l2_ref, p1_ref, p2_ref, x_ref, mod_ref, e_ref,
                       w_ref, out_ref):
    tm, d = x_ref.shape
    perms = (None, p1_ref, p2_ref)
    lses, outs = [], []
    for perm, l_ref, o_ref in zip(perms, (l0_ref, l1_ref, l2_ref), (o0_ref, o1_ref, o2_ref)):
        lse = l_ref[0].reshape(tm, LANES)
        o = o_ref[0].reshape(tm, d)
        if perm is None:
            lses.append(lse)
            outs.append(o.astype(F32))
        else:
            lses.append(_dot3(perm[...], lse))
            outs.append(jnp.dot(perm[...], o, preferred_element_type=F32))
    m = jnp.maximum(jnp.maximum(lses[0], lses[1]), lses[2])
    exps = [jnp.exp(l - m) for l in lses]
    inv = 1.0 / (exps[0] + exps[1] + exps[2])
    acc, rest = None, None
    for e, o in zip(exps, outs):
        if o is outs[-1]:
            wide = 1.0 - rest
        else:
            w = e * inv
            w_hi = w.astype(BF16)
            w_lo = (w - w_hi.astype(F32)).astype(BF16)
            wide = (jnp.dot(w_hi, e_ref[...], preferred_element_type=F32)
                    + jnp.dot(w_lo, e_ref[...], preferred_element_type=F32))
            rest = wide if rest is None else rest + wide
        acc = wide * o if acc is None else acc + wide * o
    y = jnp.dot(acc.astype(BF16), w_ref[...], preferred_element_type=F32)
    out_ref[...] = x_ref[...] + mod_ref[0, 2:3, :] * y


def _merge_proj(outs, lses, x2, mod, w_out_bf, seq):
    n, d = x2.shape
    tm = TM_PROJ
    per_b = seq // tm
    expand = (jnp.arange(LANES)[:, None] == jnp.arange(d)[None, :] // HEAD_DIM_A).astype(BF16)
    tok = jnp.arange(tm)
    perms = [(((tok % dil) * (tm // dil) + tok // dil)[:, None] == tok[None, :]).astype(BF16)
             for _, dil in DILATIONS[1:]]
    slab = lambda dil, width: pl.BlockSpec((1, dil, tm // dil, width), lambda i: (i // per_b, 0, i % per_b, 0))
    const = lambda shape: pl.BlockSpec(shape, lambda i: (0, 0))
    return pl.pallas_call(
        _merge_proj_kernel,
        grid=(n // tm,),
        in_specs=[slab(dil, d) for _, dil in DILATIONS] + [slab(dil, LANES) for _, dil in DILATIONS] + [
            const((tm, tm)),
            const((tm, tm)),
            pl.BlockSpec((tm, d), lambda i: (i, 0)),
            pl.BlockSpec((1, 6, d), lambda i: (i // per_b, 0, 0)),
            const((LANES, d)),
            const((d, d)),
        ],
        out_specs=pl.BlockSpec((tm, d), lambda i: (i, 0)),
        out_shape=jax.ShapeDtypeStruct((n, d), F32),
        compiler_params=_params("parallel"),
        name="merge_out_proj",
    )(*outs, *lses, *perms, x2, mod, expand, w_out_bf)


def _gla_in_kernel(x_ref, mod_ref, g_ref, w_ref, o_ref):
    h = _norm_mod(x_ref[...], g_ref[...], mod_ref[0, 0:1, :], mod_ref[0, 1:2, :])
    o_ref[...] = jnp.dot(h.astype(BF16), w_ref[...], preferred_element_type=F32).astype(BF16)


def _gla_in_proj(x2, mod, g1, w_bf, seq):
    n, d = x2.shape
    ncol = w_bf.shape[1]
    tm = TM_PROJ
    per_b = seq // tm
    return pl.pallas_call(
        _gla_in_kernel,
        grid=(n // tm,),
        in_specs=[
            pl.BlockSpec((tm, d), lambda i: (i, 0)),
            pl.BlockSpec((1, 6, d), lambda i: (i // per_b, 0, 0)),
            pl.BlockSpec((1, d), lambda i: (0, 0)),
            pl.BlockSpec((d, ncol), lambda i: (0, 0)),
        ],
        out_specs=pl.BlockSpec((tm, ncol), lambda i: (i, 0)),
        out_shape=jax.ShapeDtypeStruct((n, ncol), BF16),
        compiler_params=_params("parallel"),
        name="gla_in_proj",
    )(x2, mod, g1, w_bf)


def _gla_kernel(q_ref, k_ref, v_ref, r_ref, a_ref, wg_ref, gb_ref, og_ref, tri_ref, x_ref, mod_ref, wo_ref,
                out_ref, st_ref, o_scr, la_scr):
    t = pl.program_id(1)
    c = GLA_CHUNK
    nb = q_ref.shape[0]

    @pl.when(t == 0)
    def _():
        st_ref[...] = jnp.zeros_like(st_ref)

    for bb in range(nb):
        g = jnp.dot(a_ref[bb], wg_ref[...], preferred_element_type=F32) + gb_ref[...]
        la_scr[bb] = (jnp.minimum(g, 0.0) - jnp.log(1.0 + jnp.exp(-jnp.abs(g)))) * (1.0 / GATE_TAU)

    rr = lax.broadcasted_iota(jnp.int32, (c, c), 0)
    cc = lax.broadcasted_iota(jnp.int32, (c, c), 1)
    causal = cc <= rr

    def chunk(ci, carry):
        rows = pl.ds(pl.multiple_of(ci * c, c), c)
        for bb in range(nb):
            la = la_scr[bb, rows, :]
            b = _dot3(tri_ref[...], la)
            b_last = b[c - 1:c, :]
            q = q_ref[bb, rows, :].astype(F32) * (KEY_DIM_B ** -0.5)
            k = k_ref[bb, rows, :].astype(F32)
            q_dec = (q * jnp.exp(b)).astype(BF16)
            k_inv = (k * jnp.exp(-b)).astype(BF16)
            k_dec = (k * jnp.exp(b_last - b)).astype(BF16)
            decay = jnp.exp(b_last)
            v = v_ref[bb, rows, :]
            for h in range(HEADS_B):
                ks = slice(h * KEY_DIM_B, (h + 1) * KEY_DIM_B)
                vs = slice(h * VAL_DIM_B, (h + 1) * VAL_DIM_B)
                qd, ki, kd, vh = q_dec[:, ks], k_inv[:, ks], k_dec[:, ks], v[:, vs]
                att = lax.dot_general(qd, ki, (((1,), (1,)), ((), ())), preferred_element_type=F32)
                att = jnp.where(causal, att, 0.0)
                st = st_ref[bb, h]
                o = (jnp.dot(att.astype(BF16), vh, preferred_element_type=F32)
                     + lax.dot_general(qd, st.astype(BF16), (((1,), (1,)), ((), ())), preferred_element_type=F32))
                o_scr[bb, rows, vs] = o
                st_ref[bb, h] = st * decay[:, ks] + lax.dot_general(vh, kd, (((0,), (0,)), ((), ())),
                                                                     preferred_element_type=F32)
        return carry

    lax.fori_loop(0, q_ref.shape[1] // c, chunk, 0, unroll=2)

    for bb in range(nb):
        r = r_ref[bb].astype(F32)
        parts = []
        for h in range(HEADS_B):
            vs = slice(h * VAL_DIM_B, (h + 1) * VAL_DIM_B)
            oh = o_scr[bb, :, vs]
            ms = jnp.mean(oh * oh, axis=-1, keepdims=True)
            rh = r[:, vs]
            parts.append((oh * lax.rsqrt(ms + RMS_EPS) * og_ref[...] * (rh * jax.nn.sigmoid(rh))).astype(BF16))
        y = jnp.dot(jnp.concatenate(parts, axis=1), wo_ref[...], preferred_element_type=F32)
        out_ref[bb] = x_ref[bb] + mod_ref[bb, 2:3, :] * y


def _gla(proj, x3, mod, wg_bf, gate_bias, out_gain, w_out_bf):
    batch, seq, d = x3.shape
    ncol = proj.shape[1]
    t = T_GLA
    hk = HEADS_B * KEY_DIM_B
    hv = HEADS_B * VAL_DIM_B
    p3 = proj.reshape(batch, seq, ncol)
    tri = (jnp.arange(GLA_CHUNK)[None, :] <= jnp.arange(GLA_CHUNK)[:, None]).astype(BF16)
    a_blk = (2 * hk + 2 * hv) // LANES
    nb = GLA_SEQS if batch % GLA_SEQS == 0 else 1
    const = lambda shape: pl.BlockSpec(shape, lambda b, i: (0,) * len(shape))
    return pl.pallas_call(
        _gla_kernel,
        grid=(batch // nb, seq // t),
        in_specs=[
            pl.BlockSpec((nb, t, hk), lambda b, i: (b, i, 0)),
            pl.BlockSpec((nb, t, hk), lambda b, i: (b, i, 1)),
            pl.BlockSpec((nb, t, hv), lambda b, i: (b, i, (2 * hk) // hv)),
            pl.BlockSpec((nb, t, hv), lambda b, i: (b, i, (2 * hk + hv) // hv)),
            pl.BlockSpec((nb, t, LANES), lambda b, i: (b, i, a_blk)),
            const((LANES, hk)),
            const((1, hk)),
            const((1, VAL_DIM_B)),
            const((GLA_CHUNK, GLA_CHUNK)),
            pl.BlockSpec((nb, t, d), lambda b, i: (b, i, 0)),
            pl.BlockSpec((nb, 6, d), lambda b, i: (b, 0, 0)),
            const((hv, d)),
        ],
        out_specs=pl.BlockSpec((nb, t, d), lambda b, i: (b, i, 0)),
        out_shape=jax.ShapeDtypeStruct((batch, seq, d), F32),
        scratch_shapes=[
            pltpu.VMEM((nb, HEADS_B, VAL_DIM_B, KEY_DIM_B), F32),
            pltpu.VMEM((nb, t, hv), F32),
            pltpu.VMEM((nb, t, hk), F32),
        ],
        compiler_params=_params("parallel", "arbitrary"),
        name="gla",
    )(p3, p3, p3, p3, p3, wg_bf, gate_bias, out_gain, tri, x3, mod, w_out_bf)


def _router_kernel(x_ref, mod_ref, g_ref, rw_ref, rb_ref, tri_ref, upper_ref, h_ref, info_ref, cnt_ref, tcnt_ref,
                   tcar_ref, carry_ref):
    i = pl.program_id(0)

    @pl.when(i == 0)
    def _():
        carry_ref[...] = jnp.zeros_like(carry_ref)

    h = _norm_mod(x_ref[...], g_ref[...], mod_ref[0, 3:4, :], mod_ref[0, 4:5, :])
    h_ref[...] = h.astype(BF16)
    tm = h.shape[0]
    lane = lax.broadcasted_iota(jnp.int32, (tm, LANES), 1)
    lane_f = lane.astype(F32)
    h_hi = h.astype(BF16)
    h_lo = (h - h_hi.astype(F32)).astype(BF16)
    logits = (jnp.dot(h_hi, rw_ref[0], preferred_element_type=F32) + jnp.dot(h_lo, rw_ref[0], preferred_element_type=F32)
              + jnp.dot(h_hi, rw_ref[1], preferred_element_type=F32) + rb_ref[...])
    work = jnp.where(lane < N_EXPERTS, logits, -jnp.inf)
    picks, vals, idxs = [], [], []
    for _ in range(TOP_K):
        m = jnp.max(work, axis=-1, keepdims=True)
        idx = jnp.min(jnp.where(work == m, lane_f, float(LANES)), axis=-1, keepdims=True)
        pick = lane_f == idx
        work = jnp.where(pick, -jnp.inf, work)
        picks.append(pick)
        vals.append(m)
        idxs.append(idx)
    exps = [jnp.exp(v - vals[0]) for v in vals]
    inv = 1.0 / (exps[0] + exps[1] + exps[2] + exps[3])
    chosen = jnp.zeros((tm, LANES), F32)
    for pick in picks:
        chosen = jnp.where(pick, 1.0, chosen)
    here = jnp.floor((jnp.sum(chosen, axis=0, keepdims=True) + (RUN_ALIGN - 1)) * (1.0 / RUN_ALIGN)) * RUN_ALIGN
    run_start = jnp.dot(jnp.broadcast_to(here, (8, LANES)).astype(BF16), upper_ref[...],
                        preferred_element_type=F32)[0:1, :]
    sorted_row = jnp.dot(tri_ref[...], chosen.astype(BF16), preferred_element_type=F32) + run_start
    info = jnp.zeros((tm, LANES), F32)
    for kk in range(TOP_K):
        row = jnp.sum(jnp.where(picks[kk], sorted_row, 0.0), axis=-1, keepdims=True)
        info = jnp.where(lane == kk, idxs[kk], info)
        info = jnp.where(lane == TOP_K + kk, row, info)
        info = jnp.where(lane == 2 * TOP_K + kk, exps[kk] * inv, info)
    info_ref[...] = info
    tcnt_ref[0] = jnp.broadcast_to(here, tcnt_ref.shape[1:])
    tcar_ref[0] = carry_ref[...]
    total = carry_ref[0:1, :] + here
    carry_ref[...] = jnp.broadcast_to(total, carry_ref.shape)
    cnt_ref[...] = jnp.broadcast_to(total, cnt_ref.shape)


def _router(x2, mod, g2, router_w, router_b, seq):
    n, d = x2.shape
    tm = TM_ROUTE
    per_b = seq // tm
    rw = jnp.zeros((d, LANES), F32).at[:, :N_EXPERTS].set(router_w)
    rw_hi = rw.astype(BF16)
    rw = jnp.stack([rw_hi, (rw - rw_hi.astype(F32)).astype(BF16)])
    rb = jnp.zeros((1, LANES), F32).at[0, :N_EXPERTS].set(router_b)
    tri = (jnp.arange(tm)[None, :] < jnp.arange(tm)[:, None]).astype(BF16)
    upper = (jnp.arange(LANES)[:, None] < jnp.arange(LANES)[None, :]).astype(BF16)
    return pl.pallas_call(
        _router_kernel,
        grid=(n // tm,),
        in_specs=[
            pl.BlockSpec((tm, d), lambda i: (i, 0)),
            pl.BlockSpec((1, 6, d), lambda i: (i // per_b, 0, 0)),
            pl.BlockSpec((1, d), lambda i: (0, 0)),
            pl.BlockSpec((2, d, LANES), lambda i: (0, 0, 0)),
            pl.BlockSpec((1, LANES), lambda i: (0, 0)),
            pl.BlockSpec((tm, tm), lambda i: (0, 0)),
            pl.BlockSpec((LANES, LANES), lambda i: (0, 0)),
        ],
        out_specs=[
            pl.BlockSpec((tm, d), lambda i: (i, 0)),
            pl.BlockSpec((tm, LANES), lambda i: (i, 0)),
            pl.BlockSpec((8, LANES), lambda i: (0, 0)),
            pl.BlockSpec((1, 8, LANES), lambda i: (i, 0, 0)),
            pl.BlockSpec((1, 8, LANES), lambda i: (i, 0, 0)),
        ],
        out_shape=[
            jax.ShapeDtypeStruct((n, d), BF16),
            jax.ShapeDtypeStruct((n, LANES), F32),
            jax.ShapeDtypeStruct((8, LANES), F32),
            jax.ShapeDtypeStruct((n // tm, 8, LANES), F32),
            jax.ShapeDtypeStruct((n // tm, 8, LANES), F32),
        ],
        scratch_shapes=[pltpu.VMEM((8, LANES), F32)],
        compiler_params=_params("arbitrary"),
        name="moe_router",
    )(x2, mod, g2, rw, rb, tri, upper)


def _for_each_run(tile, cnt_ref, lo_ref, base_ref, act):
    bits = [1 << k for k in range(TM_MOVE.bit_length() - 1, RUN_ALIGN.bit_length() - 2, -1)]

    def one(e, carry):
        j = tile * N_EXPERTS + e
        cnt, lo, base = cnt_ref[j], lo_ref[j], base_ref[j]
        off = jnp.int32(0)
        for bit in bits:
            take = (cnt & bit) != 0
            pl.when(take)(functools.partial(act, pl.multiple_of(lo + off, RUN_ALIGN),
                                            pl.multiple_of(base + off, RUN_ALIGN), bit))
            off = off + jnp.where(take, bit, 0)
        return carry

    lax.fori_loop(0, N_EXPERTS, one, 0)


def _dispatch_kernel(seg_end_ref, padded_ref, cnt_ref, lo_ref, base_ref, pos_ref, h_ref, xs_hbm,
                     loc_ref, zero_ref, sems, zero_sem):
    i = pl.program_id(0)
    tm, d = h_ref.shape

    @pl.when(pl.program_id(0) == 0)
    def _():
        zero_ref[...] = jnp.zeros_like(zero_ref)

        def tail(e):
            return pltpu.make_async_copy(
                zero_ref, xs_hbm.at[pl.ds(pl.multiple_of(seg_end_ref[e] - TM_EXPERT, TM_EXPERT), TM_EXPERT)], zero_sem)

        def start(e, carry):
            pl.when(padded_ref[e] > 0)(lambda: tail(e).start())
            return carry

        def wait(e, carry):
            pl.when(padded_ref[e] > 0)(lambda: tail(e).wait())
            return carry

        lax.fori_loop(0, N_EXPERTS, start, 0)
        lax.fori_loop(0, N_EXPERTS, wait, 0)

        def spare(blk):
            return pltpu.make_async_copy(
                zero_ref, xs_hbm.at[pl.ds(pl.multiple_of(blk * TM_EXPERT, TM_EXPERT), TM_EXPERT)], zero_sem)

        used = seg_end_ref[N_EXPERTS - 1] // TM_EXPERT
        total = xs_hbm.shape[0] // TM_EXPERT
        lax.fori_loop(used, total, lambda blk, carry: (spare(blk).start(), carry)[1], 0)
        lax.fori_loop(used, total, lambda blk, carry: (spare(blk).wait(), carry)[1], 0)

    slot = lax.broadcasted_iota(jnp.int32, (RUN_ROWS, tm), 0).astype(F32)
    perm = jnp.zeros((RUN_ROWS, tm), F32)
    for kk in range(TOP_K):
        perm = jnp.where(slot == pos_ref[kk:kk + 1, :], 1.0, perm)
    perm = perm.astype(BF16)
    buf = i % 2
    loc_ref[buf] = _pack_pairs(jnp.dot(perm, h_ref[...], preferred_element_type=F32))

    def copy(which, local_row, global_row, size):
        return pltpu.make_async_copy(loc_ref.at[which, pl.ds(local_row, size)],
                                     xs_hbm.at[pl.ds(global_row, size)], sems.at[which])

    _for_each_run(i, cnt_ref, lo_ref, base_ref, lambda a, b, size: copy(buf, a, b, size).start())

    @pl.when(i > 0)
    def _():
        _for_each_run(i - 1, cnt_ref, lo_ref, base_ref, lambda a, b, size: copy(1 - buf, a, b, size).wait())

    @pl.when(i == pl.num_programs(0) - 1)
    def _():
        _for_each_run(i, cnt_ref, lo_ref, base_ref, lambda a, b, size: copy(buf, a, b, size).wait())


def _dispatch(h, pos_t, seg_end, padded, tile_cnt, tile_lo, tile_base, cap):
    n, d = h.shape
    tm = TM_MOVE
    smem = lambda i, *_: (0, i)
    return pl.pallas_call(
        _dispatch_kernel,
        grid_spec=pltpu.PrefetchScalarGridSpec(
            num_scalar_prefetch=5,
            grid=(n // tm,),
            in_specs=[
                pl.BlockSpec((8, tm), smem),
                pl.BlockSpec((tm, d), lambda i, *_: (i, 0)),
            ],
            out_specs=pl.BlockSpec(memory_space=pl.ANY),
            scratch_shapes=[pltpu.VMEM((2, RUN_ROWS, d // 2), jnp.uint32), pltpu.VMEM((TM_EXPERT, d // 2), jnp.uint32),
                            pltpu.SemaphoreType.DMA((2,)), pltpu.SemaphoreType.DMA(())],
        ),
        out_shape=jax.ShapeDtypeStruct((cap, d // 2), jnp.uint32),
        compiler_params=_params("arbitrary"),
        name="moe_dispatch",
    )(seg_end, padded, tile_cnt, tile_lo, tile_base, pos_t, h)


def _expert_kernel(be_ref, nxt_ref, na_ref, x_ref, wgu_hbm, bgu_ref, wd_hbm, bd_ref, o_ref,
                   gu_stage, d_stage, gu_bf, d_bf, sems, *, layer):
    i = pl.program_id(0)
    live = i < na_ref[0]
    e = be_ref[i]

    def fetch(expert):
        return (pltpu.make_async_copy(wgu_hbm.at[layer, expert], gu_stage, sems.at[0]),
                pltpu.make_async_copy(wd_hbm.at[layer, expert], d_stage, sems.at[1]))

    @pl.when(i == 0)
    def _():
        for c in fetch(e):
            c.start()

    @pl.when(live & ((i == 0) | (e != be_ref[jnp.maximum(i - 1, 0)])))
    def _():
        for c in fetch(e):
            c.wait()
        gu_bf[...] = gu_stage[...].astype(BF16)
        d_bf[...] = d_stage[...].astype(BF16)

        @pl.when(nxt_ref[i] >= 0)
        def _():
            for c in fetch(nxt_ref[i]):
                c.start()

    @pl.when(jnp.logical_not(live))
    def _():
        o_ref[...] = jnp.zeros_like(o_ref)

    @pl.when(live)
    def _():
        f = d_bf.shape[0]
        xb = _unpack_pairs(x_ref[...]).astype(BF16)
        gu = jnp.dot(xb, gu_bf[...], preferred_element_type=F32) + bgu_ref[0]
        gate = jnp.minimum(gu[:, :f], SWIGLU_LIMIT)
        up = jnp.clip(gu[:, f:], -SWIGLU_LIMIT, SWIGLU_LIMIT)
        act = (up + 1.0) * (gate * jax.nn.sigmoid(SWIGLU_ALPHA * gate))
        o_ref[...] = _pack_pairs(jnp.dot(act.astype(BF16), d_bf[...], preferred_element_type=F32) + bd_ref[0])


def _experts(xs, block_e, next_e, n_active, w_gu, b_gu, w_down, b_down, layer):
    cap, w = xs.shape
    _, e, d, f2 = w_gu.shape
    f = f2 // 2
    tm = TM_EXPERT
    nb = cap // tm
    return pl.pallas_call(
        functools.partial(_expert_kernel, layer=layer),
        grid_spec=pltpu.PrefetchScalarGridSpec(
            num_scalar_prefetch=3,
            grid=(nb,),
            in_specs=[
                pl.BlockSpec((tm, w), lambda i, be, nx, na: (jnp.minimum(i, na[0] - 1), 0)),
                pl.BlockSpec(memory_space=pl.ANY),
                pl.BlockSpec((1, 1, f2), lambda i, be, nx, na: (be[i], 0, 0)),
                pl.BlockSpec(memory_space=pl.ANY),
                pl.BlockSpec((1, 1, d), lambda i, be, nx, na: (be[i], 0, 0)),
            ],
            out_specs=pl.BlockSpec((tm, w), lambda i, be, nx, na: (i, 0)),
            scratch_shapes=[pltpu.VMEM((d, f2), F32), pltpu.VMEM((f, d), F32), pltpu.VMEM((d, f2), BF16),
                            pltpu.VMEM((f, d), BF16), pltpu.SemaphoreType.DMA((2,))],
        ),
        out_shape=jax.ShapeDtypeStruct((cap, w), xs.dtype),
        compiler_params=_params("arbitrary"),
        name="moe_experts",
    )(block_e, next_e, n_active, xs, w_gu, b_gu.reshape(e, 1, f2), w_down, b_down.reshape(e, 1, d))


def _combine_kernel(cnt_ref, lo_ref, base_ref, info_ref, x_ref, mod_ref, ys_hbm, out_ref, loc_ref, sems):
    i = pl.program_id(0)
    tm = x_ref.shape[0]
    buf = i % 2

    def copy(which, local_row, global_row, size):
        return pltpu.make_async_copy(ys_hbm.at[pl.ds(global_row, size)],
                                     loc_ref.at[which, pl.ds(local_row, size)], sems.at[which])

    def fetch(tile, which):
        _for_each_run(tile, cnt_ref, lo_ref, base_ref, lambda a, b, size: copy(which, a, b, size).start())

    @pl.when(i == 0)
    def _():
        loc_ref[...] = jnp.zeros_like(loc_ref)
        fetch(i, buf)

    @pl.when(i + 1 < pl.num_programs(0))
    def _():
        fetch(i + 1, 1 - buf)

    _for_each_run(i, cnt_ref, lo_ref, base_ref, lambda a, b, size: copy(buf, a, b, size).wait())

    info = info_ref[...]
    slot = lax.broadcasted_iota(jnp.int32, (tm, RUN_ROWS), 1).astype(F32)
    mix = jnp.zeros((tm, RUN_ROWS), F32)
    for kk in range(TOP_K):
        mix = jnp.where(slot == info[:, TOP_K + kk:TOP_K + kk + 1], info[:, 2 * TOP_K + kk:2 * TOP_K + kk + 1], mix)
    rows = _unpack_pairs(loc_ref[buf]).astype(BF16)
    y = jnp.dot(mix.astype(BF16), rows, preferred_element_type=F32)
    out_ref[...] = x_ref[...] + mod_ref[0, 5:6, :] * y


def _combine(ys, info, tile_cnt, tile_lo, tile_base, x2, mod, seq):
    n, d = x2.shape
    tm = TM_MOVE
    per_b = seq // tm
    return pl.pallas_call(
        _combine_kernel,
        grid_spec=pltpu.PrefetchScalarGridSpec(
            num_scalar_prefetch=3,
            grid=(n // tm,),
            in_specs=[
                pl.BlockSpec((tm, LANES), lambda i, *_: (i, 0)),
                pl.BlockSpec((tm, d), lambda i, *_: (i, 0)),
                pl.BlockSpec((1, 6, d), lambda i, *_: (i // per_b, 0, 0)),
                pl.BlockSpec(memory_space=pl.ANY),
            ],
            out_specs=pl.BlockSpec((tm, d), lambda i, *_: (i, 0)),
            scratch_shapes=[pltpu.VMEM((2, RUN_ROWS, ys.shape[1]), ys.dtype), pltpu.SemaphoreType.DMA((2,))],
        ),
        out_shape=jax.ShapeDtypeStruct((n, d), F32),
        compiler_params=_params("arbitrary"),
        name="moe_combine",
    )(tile_cnt, tile_lo, tile_base, info, x2, mod, ys)


def _moe(x2, mod, g2, router_w, router_b, w_gu, b_gu, w_down, b_down, layer, seq):
    n, d = x2.shape
    assert TM_MOVE == TM_ROUTE, "dispatch / combine tiles reuse the router's per-tile expert counts"
    h, info, cnt, tcnt, tcar = _router(x2, mod, g2, router_w, router_b, seq)
    counts = cnt[0, :N_EXPERTS].astype(jnp.int32)
    padded = (counts + TM_EXPERT - 1) // TM_EXPERT * TM_EXPERT
    seg_end = jnp.cumsum(padded)
    seg_start = seg_end - padded
    worst = n * TOP_K + N_EXPERTS * (n // TM_MOVE) * (RUN_ALIGN - 1)
    nb = -(-worst // TM_EXPERT) + N_EXPERTS
    cap = nb * TM_EXPERT
    block_start = jnp.arange(nb, dtype=jnp.int32) * TM_EXPERT
    block_e = jnp.minimum(jnp.sum(seg_end[None, :] <= block_start[:, None], axis=1), N_EXPERTS - 1).astype(jnp.int32)
    n_active = (seg_end[-1:] // TM_EXPERT).astype(jnp.int32)
    tile_cnt = tcnt[:, 0, :N_EXPERTS].astype(jnp.int32)
    tile_before = tcar[:, 0, :N_EXPERTS].astype(jnp.int32)
    tile_lo = jnp.cumsum(tile_cnt, axis=1) - tile_cnt
    tile_base = seg_start[None, :] + tile_before
    pos_t = info[:, TOP_K:TOP_K + 8].T
    tile_cnt, tile_lo, tile_base = (t.reshape(-1).astype(jnp.int32) for t in (tile_cnt, tile_lo, tile_base))
    ids = jnp.arange(N_EXPERTS, dtype=jnp.int32)
    later = (ids[None, :] > ids[:, None]) & (padded[None, :] > 0)
    next_of = jnp.min(jnp.where(later, ids[None, :], N_EXPERTS), axis=1)
    next_e = jnp.where(next_of < N_EXPERTS, next_of, -1).astype(jnp.int32)[block_e]
    xs = _dispatch(h, pos_t, seg_end.astype(jnp.int32), padded.astype(jnp.int32), tile_cnt, tile_lo, tile_base, cap)
    ys = _experts(xs, block_e, next_e, n_active, w_gu, b_gu, w_down, b_down, layer)
    return _combine(ys, info, tile_cnt, tile_lo, tile_base, x2, mod, seq)


def kernel(x, c, positions, ada_w, ada_b, norm1_g, norm2_g, a_w_in, a_q_gain, a_k_gain, a_w_out, b_w_in,
           b_w_gate_up, b_gate_bias, b_out_gain, b_w_out, router_w, router_b, moe_w_gu, moe_b_gu, moe_w_down,
           moe_b_down):
    batch, seq, d = x.shape
    depth = ada_w.shape[0]
    n = batch * seq
    mods = _adaln(c, ada_w, ada_b).reshape(depth, batch, 6, d)
    x2 = x.reshape(n, d)
    for layer in range(depth):
        mod = mods[layer]
        j = layer // 2
        g1 = norm1_g[layer].reshape(1, d)
        g2 = norm2_g[layer].reshape(1, d)
        if layer % 2 == 0:
            qkv = _qkv_proj(x2, mod, g1, a_w_in[j], a_q_gain[j], a_k_gain[j], positions, batch, seq)
            outs, lses = zip(*[_dilated_attention(*qkv[g], g) for g in range(N_GROUPS)])
            x2 = _merge_proj(outs, lses, x2, mod, a_w_out[j].astype(BF16), seq)
        else:
            hk = HEADS_B * KEY_DIM_B
            hv = HEADS_B * VAL_DIM_B
            width = 2 * hk + 2 * hv
            w_in = jnp.zeros((d, width + LANES), BF16).at[:, :width + GATE_RANK].set(b_w_in[j].astype(BF16))
            wg = jnp.zeros((LANES, hk), BF16).at[:GATE_RANK].set(b_w_gate_up[j].astype(BF16))
            proj = _gla_in_proj(x2, mod, g1, w_in, seq)
            x3 = _gla(proj, x2.reshape(batch, seq, d), mod, wg, b_gate_bias[j].reshape(1, hk),
                      b_out_gain[j].reshape(1, VAL_DIM_B), b_w_out[j].astype(BF16))
            x2 = x3.reshape(n, d)
        x2 = _moe(x2, mod, g2, router_w[layer], router_b[layer], moe_w_gu, moe_b_gu[layer],
                  moe_w_down, moe_b_down[layer], layer, seq)
    return x2.reshape(batch, seq, d)
```

```python
import functools

import jax
import jax.numpy as jnp
from jax import lax
from jax.experimental import pallas as pl
from jax.experimental.pallas import tpu as pltpu

F32 = jnp.float32
BF16 = jnp.bfloat16
HIGHEST = lax.Precision.HIGHEST

RMS_EPS = 1e-6
DILATIONS = ((128, 1), (512, 4), (2048, 16))
N_GROUPS = 3
HEADS_A = 16
HEAD_DIM_A = 64
BAND = 128
ROPE_THETA = 10000.0
NEG_INF = -1e30
HEADS_B = 4
KEY_DIM_B = 128
VAL_DIM_B = 256
GATE_RANK = 16
GATE_TAU = 16.0
GLA_CHUNK = 64
N_EXPERTS = 32
TOP_K = 4
SWIGLU_LIMIT = 7.0
SWIGLU_ALPHA = 1.702

LANES = 128
VMEM_LIMIT = 56 * 1024 * 1024

TM_PROJ = 512
TM_QKV = 256
TM_ROUTE = 512
TM_EXPERT = 512
TM_MOVE = 512
ATTN_ROWS = 512
RUN_ALIGN = 8
RUN_ROWS = TOP_K * TM_MOVE + N_EXPERTS * RUN_ALIGN
T_GLA = 512
GLA_SEQS = 1


def _params(*sem):
    return pltpu.CompilerParams(dimension_semantics=sem, vmem_limit_bytes=VMEM_LIMIT)


def _norm_mod(x, gain, shift, scale):
    ms = jnp.mean(x * x, axis=-1, keepdims=True)
    y = x * lax.rsqrt(ms + RMS_EPS) * gain
    return y * (1.0 + scale) + shift


def _pack_pairs(v):
    k = v.shape[1] // 2
    bits = lax.bitcast_convert_type(v.astype(BF16).astype(F32), jnp.uint32)
    return (bits[:, :k] >> 16) | bits[:, k:]


def _unpack_pairs(w):
    lo = lax.bitcast_convert_type(w << 16, F32)
    hi = lax.bitcast_convert_type(w & jnp.uint32(0xFFFF0000), F32)
    return jnp.concatenate([lo, hi], axis=1)


def _adaln_kernel(c_ref, w_ref, b_ref, o_ref):
    c = c_ref[...]
    cond = c * jax.nn.sigmoid(c)
    o_ref[0] = jnp.dot(cond, w_ref[0], precision=HIGHEST, preferred_element_type=F32) + b_ref[0]


def _adaln(c, ada_w, ada_b):
    depth, d, n6 = ada_w.shape
    b = c.shape[0]
    tn = 1536
    return pl.pallas_call(
        _adaln_kernel,
        grid=(depth, n6 // tn),
        in_specs=[
            pl.BlockSpec((b, d), lambda l, j: (0, 0)),
            pl.BlockSpec((1, d, tn), lambda l, j: (l, 0, j)),
            pl.BlockSpec((1, 1, tn), lambda l, j: (l, 0, j)),
        ],
        out_specs=pl.BlockSpec((1, b, tn), lambda l, j: (l, 0, j)),
        out_shape=jax.ShapeDtypeStruct((depth, b, n6), F32),
        compiler_params=_params("parallel", "parallel"),
        name="adaln",
    )(c, ada_w, ada_b.reshape(depth, 1, n6))


def _rope_kernel(pos_ref, cos_ref, sin_ref, stage_ref):
    pos = pos_ref[...].astype(F32)
    lane = lax.broadcasted_iota(jnp.int32, (1, LANES), 1)
    half = HEAD_DIM_A // 2
    fidx = (lane & (half - 1)).astype(F32)
    inv_freq = jnp.power(jnp.full((1, LANES), ROPE_THETA, F32), -(fidx / half))
    ang = pos * inv_freq
    s = jnp.sin(ang)
    tables = (jnp.cos(ang), jnp.where(lane < LANES // 2, -s, s))
    tm = pos.shape[0]
    for tab, out_ref, stage in zip(tables, (cos_ref, sin_ref), (stage_ref.at[0], stage_ref.at[1])):
        out_ref[0] = tab
        stage[...] = tab
        for g in range(1, N_GROUPS):
            dil = DILATIONS[g][1]
            rows = TM_QKV // dil
            for tile in range(tm // TM_QKV):
                for r in range(dil):
                    lo = tile * TM_QKV + r * rows
                    out_ref[g, lo:lo + rows, :] = stage[pl.ds(tile * TM_QKV + r, rows, stride=dil), :]


def _rope_tables(positions):
    n = positions.size
    tm = 1024
    return pl.pallas_call(
        _rope_kernel,
        grid=(n // tm,),
        in_specs=[pl.BlockSpec((tm, 1), lambda i: (i, 0))],
        out_specs=[pl.BlockSpec((N_GROUPS, tm, LANES), lambda i: (0, i, 0))] * 2,
        out_shape=[jax.ShapeDtypeStruct((N_GROUPS, n, LANES), F32)] * 2,
        scratch_shapes=[pltpu.VMEM((2, tm, LANES), F32)],
        compiler_params=_params("parallel"),
        name="rope_tables",
    )(positions.reshape(n, 1))


def _stream_order(tm, dil):
    s = jnp.arange(tm)
    return (s % (tm // dil)) * dil + s // (tm // dil)


def _qkv_kernel(x_ref, mod_ref, g_ref, w_hbm, gain_ref, cos_ref, sin_ref, bd_ref, p1_ref, p2_ref, *rest):
    o_refs, (w_ref, w_sem) = rest[:3 * N_GROUPS], rest[3 * N_GROUPS:]
    tm, d = x_ref.shape

    @pl.when(pl.program_id(0) == 0)
    def _():
        copy = pltpu.make_async_copy(w_hbm, w_ref, w_sem)
        copy.start()
        copy.wait()

    h = _norm_mod(x_ref[...], g_ref[...], mod_ref[0, 0:1, :], mod_ref[0, 1:2, :]).astype(BF16)
    hs = [h] + [jnp.dot(p[...], h, preferred_element_type=F32).astype(BF16) for p in (p1_ref, p2_ref)]
    for jj in range(3 * N_GROUPS):
        g, comp = divmod(jj, 3)
        dil = DILATIONS[g][1]
        rows = tm // dil
        out = o_refs[jj]
        acc = jnp.dot(hs[g], w_ref[:, jj * d:(jj + 1) * d], preferred_element_type=F32)
        if comp == 2:
            val = acc.astype(BF16)
            for r in range(dil):
                out[0, r] = val[r * rows:(r + 1) * rows]
            continue
        gain = gain_ref[jj]
        cos = cos_ref[g]
        sin = sin_ref[g]
        for cc in range(d // 256):
            a = acc[:, cc * 256:(cc + 1) * 256]
            ss = jnp.dot((a * a).astype(BF16), bd_ref[...], preferred_element_type=F32)
            qn = a * lax.rsqrt(ss * (1.0 / HEAD_DIM_A) + RMS_EPS) * gain[:, cc * 256:(cc + 1) * 256]
            for hh in range(2):
                xx = qn[:, hh * LANES:(hh + 1) * LANES]
                val = (xx * cos + pltpu.roll(xx, 64, 1) * sin).astype(BF16)
                c1 = cc * 256 + hh * LANES
                for r in range(dil):
                    out[0, r, :, c1:c1 + LANES] = val[r * rows:(r + 1) * rows]


def _split_half_columns(w):
    lead = w.shape[:-1]
    w = w.reshape(*lead, HEADS_A // 2, 2, 2, HEAD_DIM_A // 2)
    return jnp.swapaxes(w, -3, -2).reshape(*lead, HEADS_A * HEAD_DIM_A)


def _qkv_proj(x2, mod, g1, w_in, q_gain, k_gain, positions, batch, seq):
    n, d = x2.shape
    tm = TM_QKV
    per_b = seq // tm
    reps = d // HEAD_DIM_A
    w = w_in.astype(BF16).reshape(d, N_GROUPS, 3, d)
    w = jnp.concatenate([_split_half_columns(w[:, :, :2]), w[:, :, 2:]], axis=2).reshape(d, 3 * N_GROUPS * d)
    gains = []
    for g in range(N_GROUPS):
        gains += [_split_half_columns(jnp.tile(q_gain[g], reps)) * (HEAD_DIM_A ** -0.5),
                  _split_half_columns(jnp.tile(k_gain[g], reps)), jnp.ones((d,), F32)]
    gains = jnp.stack(gains).reshape(3 * N_GROUPS, 1, d)
    orders = [_stream_order(tm, dil) for _, dil in DILATIONS]
    cos_t, sin_t = _rope_tables(positions)
    perms = [(o[:, None] == jnp.arange(tm)[None, :]).astype(BF16) for o in orders[1:]]
    lane = jnp.arange(256)
    head_of = (lane // LANES) * 2 + (lane // 32) % 2
    bd = (head_of[:, None] == head_of[None, :]).astype(BF16)
    dils = [DILATIONS[jj // 3][1] for jj in range(3 * N_GROUPS)]
    const = lambda shape: pl.BlockSpec(shape, lambda i: (0,) * len(shape))
    flat = pl.pallas_call(
        _qkv_kernel,
        grid=(n // tm,),
        in_specs=[
            pl.BlockSpec((tm, d), lambda i: (i, 0)),
            pl.BlockSpec((1, 6, d), lambda i: (i // per_b, 0, 0)),
            const((1, d)),
            pl.BlockSpec(memory_space=pl.ANY),
            const((3 * N_GROUPS, 1, d)),
            pl.BlockSpec((N_GROUPS, tm, LANES), lambda i: (0, i, 0)),
            pl.BlockSpec((N_GROUPS, tm, LANES), lambda i: (0, i, 0)),
            const((256, 256)),
            const((tm, tm)),
            const((tm, tm)),
        ],
        out_specs=[pl.BlockSpec((1, dil, tm // dil, d), lambda i: (i // per_b, 0, i % per_b, 0)) for dil in dils],
        out_shape=[jax.ShapeDtypeStruct((batch, dil, seq // dil, d), BF16) for dil in dils],
        scratch_shapes=[pltpu.VMEM((d, 3 * N_GROUPS * d), BF16), pltpu.SemaphoreType.DMA(())],
        compiler_params=_params("arbitrary"),
        name="qkv_proj",
    )(x2, mod, g1, w, gains, cos_t, sin_t, bd, *perms)
    return [flat[3 * g:3 * g + 3] for g in range(N_GROUPS)]


def _attn_kernel(q_ref, kp_ref, kc_ref, vp_ref, vc_ref, o_ref, lse_ref, *, steps):
    q_ref, kp_ref, kc_ref, vp_ref, vc_ref, o_ref, lse_ref = (
        r.at[0, 0] for r in (q_ref, kp_ref, kc_ref, vp_ref, vc_ref, o_ref, lse_ref))
    n = pl.program_id(2)
    row = lax.broadcasted_iota(jnp.int32, (BAND, 2 * BAND), 0)
    col = lax.broadcasted_iota(jnp.int32, (BAND, 2 * BAND), 1)
    delta = row + BAND - col
    in_band = (delta >= 0) & (delta <= steps)
    lane = lax.broadcasted_iota(jnp.int32, (BAND, LANES), 1)
    first_head = lane < HEAD_DIM_A
    even_head = (lane & (HEAD_DIM_A // 2)) == 0
    for sub in range(q_ref.shape[0] // BAND):
        rows = slice(sub * BAND, (sub + 1) * BAND)
        if sub == 0:
            k_prev, v_prev = kp_ref, vp_ref
            valid = in_band & ((col >= BAND) | (n > 0))
        else:
            k_prev, v_prev = kc_ref.at[(sub - 1) * BAND:sub * BAND], vc_ref.at[(sub - 1) * BAND:sub * BAND]
            valid = in_band
        lse_tile = jnp.zeros((BAND, LANES), F32)
        for p in range(HEADS_A // 2):
            sl = slice(p * LANES, (p + 1) * LANES)
            qp = q_ref[rows, sl]
            kcat = jnp.concatenate([k_prev[:, sl], kc_ref[rows, sl]], axis=0)
            vcat = jnp.concatenate([v_prev[:, sl], vc_ref[rows, sl]], axis=0)
            outs = []
            for hh in range(2):
                keep = even_head if hh == 0 else jnp.logical_not(even_head)
                qm = jnp.where(keep, qp, jnp.zeros_like(qp))
                s = lax.dot_general(qm, kcat, (((1,), (1,)), ((), ())), preferred_element_type=F32)
                s = jnp.where(valid, s, NEG_INF)
                m = jnp.max(s, axis=-1, keepdims=True)
                pexp = jnp.exp(s - m)
                den = jnp.sum(pexp, axis=-1, keepdims=True)
                o = jnp.dot(pexp.astype(BF16), vcat, preferred_element_type=F32) / den
                outs.append(o)
                lse_tile = jnp.where(lane == (2 * p + hh), m + jnp.log(den), lse_tile)
            o_ref[rows, sl] = jnp.where(first_head, outs[0], outs[1]).astype(BF16)
        lse_ref[rows, :] = lse_tile


def _dilated_attention(q, k, v, g):
    window, dil = DILATIONS[g]
    batch, _, l, d = q.shape
    qb = min(ATTN_ROWS, l)
    cur = pl.BlockSpec((1, 1, qb, d), lambda b, r, i: (b, r, i, 0))
    prev = pl.BlockSpec((1, 1, BAND, d), lambda b, r, i: (b, r, jnp.maximum(i * (qb // BAND) - 1, 0), 0))
    return pl.pallas_call(
        functools.partial(_attn_kernel, steps=window // dil),
        grid=(batch, dil, l // qb),
        in_specs=[cur, prev, cur, prev, cur],
        out_specs=[cur, pl.BlockSpec((1, 1, qb, LANES), lambda b, r, i: (b, r, i, 0))],
        out_shape=[
            jax.ShapeDtypeStruct((batch, dil, l, d), BF16),
            jax.ShapeDtypeStruct((batch, dil, l, LANES), F32),
        ],
        compiler_params=_params("parallel", "parallel", "arbitrary"),
        name=f"dilated_attn_g{g}",
    )(q, k, k, v, v)


def _split3(v):
    hi = v.astype(BF16)
    rem = v - hi.astype(F32)
    mid = rem.astype(BF16)
    return hi, mid, (rem - mid.astype(F32)).astype(BF16)


def _dot3(mat_bf, v):
    return sum(jnp.dot(mat_bf, part, preferred_element_type=F32) for part in _split3(v))


def _merge_proj_kernel(o0_ref, o1_ref, o2_ref, l0_ref, l1_ref, l2_ref, p1_ref, p2_ref, x_ref, mod_ref, e_ref,
                       w_ref, out_ref):
    tm, d = x_ref.shape
    perms = (None, p1_ref, p2_ref)
    lses, outs = [], []
    for perm, l_ref, o_ref in zip(perms, (l0_ref, l1_ref, l2_ref), (o0_ref, o1_ref, o2_ref)):
        lse = l_ref[0].reshape(tm, LANES)
        o = o_ref[0].reshape(tm, d)
        if perm is None:
            lses.append(lse)
            outs.append(o.astype(F32))
        else:
            lses.append(_dot3(perm[...], lse))
            outs.append(jnp.dot(perm[...], o, preferred_element_type=F32))
    m = jnp.maximum(jnp.maximum(lses[0], lses[1]), lses[2])
    exps = [jnp.exp(l - m) for l in lses]
    inv = 1.0 / (exps[0] + exps[1] + exps[2])
    acc, rest = None, None
    for e, o in zip(exps, outs):
        if o is outs[-1]:
            wide = 1.0 - rest
        else:
            w = e * inv
            w_hi = w.astype(BF16)
            w_lo = (w - w_hi.astype(F32)).astype(BF16)
            wide = (jnp.dot(w_hi, e_ref[...], preferred_element_type=F32)
                    + jnp.dot(w_lo, e_ref[...], preferred_element_type=F32))
            rest = wide if rest is None else rest + wide
        acc = wide * o if acc is None else acc + wide * o
    y = jnp.dot(acc.astype(BF16), w_ref[...], preferred_element_type=F32)
    out_ref[...] = x_ref[...] + mod_ref[0, 2:3, :] * y


def _merge_proj(outs, lses, x2, mod, w_out_bf, seq):
    n, d = x2.shape
    tm = TM_PROJ
    per_b = seq // tm
    expand = (jnp.arange(LANES)[:, None] == jnp.arange(d)[None, :] // HEAD_DIM_A).astype(BF16)
    tok = jnp.arange(tm)
    perms = [(((tok % dil) * (tm // dil) + tok // dil)[:, None] == tok[None, :]).astype(BF16)
             for _, dil in DILATIONS[1:]]
    slab = lambda dil, width: pl.BlockSpec((1, dil, tm // dil, width), lambda i: (i // per_b, 0, i % per_b, 0))
    const = lambda shape: pl.BlockSpec(shape, lambda i: (0, 0))
    return pl.pallas_call(
        _merge_proj_kernel,
        grid=(n // tm,),
        in_specs=[slab(dil, d) for _, dil in DILATIONS] + [slab(dil, LANES) for _, dil in DILATIONS] + [
            const((tm, tm)),
            const((tm, tm)),
            pl.BlockSpec((tm, d), lambda i: (i, 0)),
            pl.BlockSpec((1, 6, d), lambda i: (i // per_b, 0, 0)),
            const((LANES, d)),
            const((d, d)),
        ],
        out_specs=pl.BlockSpec((tm, d), lambda i: (i, 0)),
        out_shape=jax.ShapeDtypeStruct((n, d), F32),
        compiler_params=_params("parallel"),
        name="merge_out_proj",
    )(*outs, *lses, *perms, x2, mod, expand, w_out_bf)


def _gla_in_kernel(x_ref, mod_ref, g_ref, w_ref, o_ref):
    h = _norm_mod(x_ref[...], g_ref[...], mod_ref[0, 0:1, :], mod_ref[0, 1:2, :])
    o_ref[...] = jnp.dot(h.astype(BF16), w_ref[...], preferred_element_type=F32).astype(BF16)


def _gla_in_proj(x2, mod, g1, w_bf, seq):
    n, d = x2.shape
    ncol = w_bf.shape[1]
    tm = TM_PROJ
    per_b = seq // tm
    return pl.pallas_call(
        _gla_in_kernel,
        grid=(n // tm,),
        in_specs=[
            pl.BlockSpec((tm, d), lambda i: (i, 0)),
            pl.BlockSpec((1, 6, d), lambda i: (i // per_b, 0, 0)),
            pl.BlockSpec((1, d), lambda i: (0, 0)),
            pl.BlockSpec((d, ncol), lambda i: (0, 0)),
        ],
        out_specs=pl.BlockSpec((tm, ncol), lambda i: (i, 0)),
        out_shape=jax.ShapeDtypeStruct((n, ncol), BF16),
        compiler_params=_params("parallel"),
        name="gla_in_proj",
    )(x2, mod, g1, w_bf)


def _gla_kernel(q_ref, k_ref, v_ref, r_ref, a_ref, wg_ref, gb_ref, og_ref, tri_ref, x_ref, mod_ref, wo_ref,
                out_ref, st_ref, o_scr, la_scr):
    t = pl.program_id(1)
    c = GLA_CHUNK
    nb = q_ref.shape[0]

    @pl.when(t == 0)
    def _():
        st_ref[...] = jnp.zeros_like(st_ref)

    for bb in range(nb):
        g = jnp.dot(a_ref[bb], wg_ref[...], preferred_element_type=F32) + gb_ref[...]
        la_scr[bb] = (jnp.minimum(g, 0.0) - jnp.log(1.0 + jnp.exp(-jnp.abs(g)))) * (1.0 / GATE_TAU)

    rr = lax.broadcasted_iota(jnp.int32, (c, c), 0)
    cc = lax.broadcasted_iota(jnp.int32, (c, c), 1)
    causal = cc <= rr

    def chunk(ci, carry):
        rows = pl.ds(pl.multiple_of(ci * c, c), c)
        for bb in range(nb):
            la = la_scr[bb, rows, :]
            b = _dot3(tri_ref[...], la)
            b_last = b[c - 1:c, :]
            q = q_ref[bb, rows, :].astype(F32) * (KEY_DIM_B ** -0.5)
            k = k_ref[bb, rows, :].astype(F32)
            q_dec = (q * jnp.exp(b)).astype(BF16)
            k_inv = (k * jnp.exp(-b)).astype(BF16)
            k_dec = (k * jnp.exp(b_last - b)).astype(BF16)
            decay = jnp.exp(b_last)
            v = v_ref[bb, rows, :]
            for h in range(HEADS_B):
                ks = slice(h * KEY_DIM_B, (h + 1) * KEY_DIM_B)
                vs = slice(h * VAL_DIM_B, (h + 1) * VAL_DIM_B)
                qd, ki, kd, vh = q_dec[:, ks], k_inv[:, ks], k_dec[:, ks], v[:, vs]
                att = lax.dot_general(qd, ki, (((1,), (1,)), ((), ())), preferred_element_type=F32)
                att = jnp.where(causal, att, 0.0)
                st = st_ref[bb, h]
                o = (jnp.dot(att.astype(BF16), vh, preferred_element_type=F32)
                     + lax.dot_general(qd, st.astype(BF16), (((1,), (1,)), ((), ())), preferred_element_type=F32))
                o_scr[bb, rows, vs] = o
                st_ref[bb, h] = st * decay[:, ks] + lax.dot_general(vh, kd, (((0,), (0,)), ((), ())),
                                                                     preferred_element_type=F32)
        return carry

    lax.fori_loop(0, q_ref.shape[1] // c, chunk, 0, unroll=2)

    for bb in range(nb):
        r = r_ref[bb].astype(F32)
        parts = []
        for h in range(HEADS_B):
            vs = slice(h * VAL_DIM_B, (h + 1) * VAL_DIM_B)
            oh = o_scr[bb, :, vs]
            ms = jnp.mean(oh * oh, axis=-1, keepdims=True)
            rh = r[:, vs]
            parts.append((oh * lax.rsqrt(ms + RMS_EPS) * og_ref[...] * (rh * jax.nn.sigmoid(rh))).astype(BF16))
        y = jnp.dot(jnp.concatenate(parts, axis=1), wo_ref[...], preferred_element_type=F32)
        out_ref[bb] = x_ref[bb] + mod_ref[bb, 2:3, :] * y


def _gla(proj, x3, mod, wg_bf, gate_bias, out_gain, w_out_bf):
    batch, seq, d = x3.shape
    ncol = proj.shape[1]
    t = T_GLA
    hk = HEADS_B * KEY_DIM_B
    hv = HEADS_B * VAL_DIM_B
    p3 = proj.reshape(batch, seq, ncol)
    tri = (jnp.arange(GLA_CHUNK)[None, :] <= jnp.arange(GLA_CHUNK)[:, None]).astype(BF16)
    a_blk = (2 * hk + 2 * hv) // LANES
    nb = GLA_SEQS if batch % GLA_SEQS == 0 else 1
    const = lambda shape: pl.BlockSpec(shape, lambda b, i: (0,) * len(shape))
    return pl.pallas_call(
        _gla_kernel,
        grid=(batch // nb, seq // t),
        in_specs=[
            pl.BlockSpec((nb, t, hk), lambda b, i: (b, i, 0)),
            pl.BlockSpec((nb, t, hk), lambda b, i: (b, i, 1)),
            pl.BlockSpec((nb, t, hv), lambda b, i: (b, i, (2 * hk) // hv)),
            pl.BlockSpec((nb, t, hv), lambda b, i: (b, i, (2 * hk + hv) // hv)),
            pl.BlockSpec((nb, t, LANES), lambda b, i: (b, i, a_blk)),
            const((LANES, hk)),
            const((1, hk)),
            const((1, VAL_DIM_B)),
            const((GLA_CHUNK, GLA_CHUNK)),
            pl.BlockSpec((nb, t, d), lambda b, i: (b, i, 0)),
            pl.BlockSpec((nb, 6, d), lambda b, i: (b, 0, 0)),
            const((hv, d)),
        ],
        out_specs=pl.BlockSpec((nb, t, d), lambda b, i: (b, i, 0)),
        out_shape=jax.ShapeDtypeStruct((batch, seq, d), F32),
        scratch_shapes=[
            pltpu.VMEM((nb, HEADS_B, VAL_DIM_B, KEY_DIM_B), F32),
            pltpu.VMEM((nb, t, hv), F32),
            pltpu.VMEM((nb, t, hk), F32),
        ],
        compiler_params=_params("parallel", "arbitrary"),
        name="gla",
    )(p3, p3, p3, p3, p3, wg_bf, gate_bias, out_gain, tri, x3, mod, w_out_bf)


def _router_kernel(x_ref, mod_ref, g_ref, rw_ref, rb_ref, tri_ref, upper_ref, h_ref, info_ref, cnt_ref, tcnt_ref,
                   tcar_ref, carry_ref):
    i = pl.program_id(0)

    @pl.when(i == 0)
    def _():
        carry_ref[...] = jnp.zeros_like(carry_ref)

    h = _norm_mod(x_ref[...], g_ref[...], mod_ref[0, 3:4, :], mod_ref[0, 4:5, :])
    h_ref[...] = h.astype(BF16)
    tm = h.shape[0]
    lane = lax.broadcasted_iota(jnp.int32, (tm, LANES), 1)
    lane_f = lane.astype(F32)
    h_hi = h.astype(BF16)
    h_lo = (h - h_hi.astype(F32)).astype(BF16)
    logits = (jnp.dot(h_hi, rw_ref[0], preferred_element_type=F32) + jnp.dot(h_lo, rw_ref[0], preferred_element_type=F32)
              + jnp.dot(h_hi, rw_ref[1], preferred_element_type=F32) + rb_ref[...])
    work = jnp.where(lane < N_EXPERTS, logits, -jnp.inf)
    picks, vals, idxs = [], [], []
    for _ in range(TOP_K):
        m = jnp.max(work, axis=-1, keepdims=True)
        idx = jnp.min(jnp.where(work == m, lane_f, float(LANES)), axis=-1, keepdims=True)
        pick = lane_f == idx
        work = jnp.where(pick, -jnp.inf, work)
        picks.append(pick)
        vals.append(m)
        idxs.append(idx)
    exps = [jnp.exp(v - vals[0]) for v in vals]
    inv = 1.0 / (exps[0] + exps[1] + exps[2] + exps[3])
    chosen = jnp.zeros((tm, LANES), F32)
    for pick in picks:
        chosen = jnp.where(pick, 1.0, chosen)
    here = jnp.floor((jnp.sum(chosen, axis=0, keepdims=True) + (RUN_ALIGN - 1)) * (1.0 / RUN_ALIGN)) * RUN_ALIGN
    run_start = jnp.dot(jnp.broadcast_to(here, (8, LANES)).astype(BF16), upper_ref[...],
                        preferred_element_type=F32)[0:1, :]
    sorted_row = jnp.dot(tri_ref[...], chosen.astype(BF16), preferred_element_type=F32) + run_start
    info = jnp.zeros((tm, LANES), F32)
    for kk in range(TOP_K):
        row = jnp.sum(jnp.where(picks[kk], sorted_row, 0.0), axis=-1, keepdims=True)
        info = jnp.where(lane == kk, idxs[kk], info)
        info = jnp.where(lane == TOP_K + kk, row, info)
        info = jnp.where(lane == 2 * TOP_K + kk, exps[kk] * inv, info)
    info_ref[...] = info
    tcnt_ref[0] = jnp.broadcast_to(here, tcnt_ref.shape[1:])
    tcar_ref[0] = carry_ref[...]
    total = carry_ref[0:1, :] + here
    carry_ref[...] = jnp.broadcast_to(total, carry_ref.shape)
    cnt_ref[...] = jnp.broadcast_to(total, cnt_ref.shape)


def _router(x2, mod, g2, router_w, router_b, seq):
    n, d = x2.shape
    tm = TM_ROUTE
    per_b = seq // tm
    rw = jnp.zeros((d, LANES), F32).at[:, :N_EXPERTS].set(router_w)
    rw_hi = rw.astype(BF16)
    rw = jnp.stack([rw_hi, (rw - rw_hi.astype(F32)).astype(BF16)])
    rb = jnp.zeros((1, LANES), F32).at[0, :N_EXPERTS].set(router_b)
    tri = (jnp.arange(tm)[None, :] < jnp.arange(tm)[:, None]).astype(BF16)
    upper = (jnp.arange(LANES)[:, None] < jnp.arange(LANES)[None, :]).astype(BF16)
    return pl.pallas_call(
        _router_kernel,
        grid=(n // tm,),
        in_specs=[
            pl.BlockSpec((tm, d), lambda i: (i, 0)),
            pl.BlockSpec((1, 6, d), lambda i: (i // per_b, 0, 0)),
            pl.BlockSpec((1, d), lambda i: (0, 0)),
            pl.BlockSpec((2, d, LANES), lambda i: (0, 0, 0)),
            pl.BlockSpec((1, LANES), lambda i: (0, 0)),
            pl.BlockSpec((tm, tm), lambda i: (0, 0)),
            pl.BlockSpec((LANES, LANES), lambda i: (0, 0)),
        ],
        out_specs=[
            pl.BlockSpec((tm, d), lambda i: (i, 0)),
            pl.BlockSpec((tm, LANES), lambda i: (i, 0)),
            pl.BlockSpec((8, LANES), lambda i: (0, 0)),
            pl.BlockSpec((1, 8, LANES), lambda i: (i, 0, 0)),
            pl.BlockSpec((1, 8, LANES), lambda i: (i, 0, 0)),
        ],
        out_shape=[
            jax.ShapeDtypeStruct((n, d), BF16),
            jax.ShapeDtypeStruct((n, LANES), F32),
            jax.ShapeDtypeStruct((8, LANES), F32),
            jax.ShapeDtypeStruct((n // tm, 8, LANES), F32),
            jax.ShapeDtypeStruct((n // tm, 8, LANES), F32),
        ],
        scratch_shapes=[pltpu.VMEM((8, LANES), F32)],
        compiler_params=_params("arbitrary"),
        name="moe_router",
    )(x2, mod, g2, rw, rb, tri, upper)


def _for_each_run(tile, cnt_ref, lo_ref, base_ref, act):
    bits = [1 << k for k in range(TM_MOVE.bit_length() - 1, RUN_ALIGN.bit_length() - 2, -1)]

    def one(e, carry):
        j = tile * N_EXPERTS + e
        cnt, lo, base = cnt_ref[j], lo_ref[j], base_ref[j]
        off = jnp.int32(0)
        for bit in bits:
            take = (cnt & bit) != 0
            pl.when(take)(functools.partial(act, pl.multiple_of(lo + off, RUN_ALIGN),
                                            pl.multiple_of(base + off, RUN_ALIGN), bit))
            off = off + jnp.where(take, bit, 0)
        return carry

    lax.fori_loop(0, N_EXPERTS, one, 0)


def _dispatch_kernel(seg_end_ref, padded_ref, cnt_ref, lo_ref, base_ref, pos_ref, h_ref, xs_hbm,
                     loc_ref, zero_ref, sems, zero_sem):
    i = pl.program_id(0)
    tm, d = h_ref.shape

    @pl.when(pl.program_id(0) == 0)
    def _():
        zero_ref[...] = jnp.zeros_like(zero_ref)

        def tail(e):
            return pltpu.make_async_copy(
                zero_ref, xs_hbm.at[pl.ds(pl.multiple_of(seg_end_ref[e] - TM_EXPERT, TM_EXPERT), TM_EXPERT)], zero_sem)

        def start(e, carry):
            pl.when(padded_ref[e] > 0)(lambda: tail(e).start())
            return carry

        def wait(e, carry):
            pl.when(padded_ref[e] > 0)(lambda: tail(e).wait())
            return carry

        lax.fori_loop(0, N_EXPERTS, start, 0)
        lax.fori_loop(0, N_EXPERTS, wait, 0)

        def spare(blk):
            return pltpu.make_async_copy(
                zero_ref, xs_hbm.at[pl.ds(pl.multiple_of(blk * TM_EXPERT, TM_EXPERT), TM_EXPERT)], zero_sem)

        used = seg_end_ref[N_EXPERTS - 1] // TM_EXPERT
        total = xs_hbm.shape[0] // TM_EXPERT
        lax.fori_loop(used, total, lambda blk, carry: (spare(blk).start(), carry)[1], 0)
        lax.fori_loop(used, total, lambda blk, carry: (spare(blk).wait(), carry)[1], 0)

    slot = lax.broadcasted_iota(jnp.int32, (RUN_ROWS, tm), 0).astype(F32)
    perm = jnp.zeros((RUN_ROWS, tm), F32)
    for kk in range(TOP_K):
        perm = jnp.where(slot == pos_ref[kk:kk + 1, :], 1.0, perm)
    perm = perm.astype(BF16)
    buf = i % 2
    loc_ref[buf] = _pack_pairs(jnp.dot(perm, h_ref[...], preferred_element_type=F32))

    def copy(which, local_row, global_row, size):
        return pltpu.make_async_copy(loc_ref.at[which, pl.ds(local_row, size)],
                                     xs_hbm.at[pl.ds(global_row, size)], sems.at[which])

    _for_each_run(i, cnt_ref, lo_ref, base_ref, lambda a, b, size: copy(buf, a, b, size).start())

    @pl.when(i > 0)
    def _():
        _for_each_run(i - 1, cnt_ref, lo_ref, base_ref, lambda a, b, size: copy(1 - buf, a, b, size).wait())

    @pl.when(i == pl.num_programs(0) - 1)
    def _():
        _for_each_run(i, cnt_ref, lo_ref, base_ref, lambda a, b, size: copy(buf, a, b, size).wait())


def _dispatch(h, pos_t, seg_end, padded, tile_cnt, tile_lo, tile_base, cap):
    n, d = h.shape
    tm = TM_MOVE
    smem = lambda i, *_: (0, i)
    return pl.pallas_call(
        _dispatch_kernel,
        grid_spec=pltpu.PrefetchScalarGridSpec(
            num_scalar_prefetch=5,
            grid=(n // tm,),
            in_specs=[
                pl.BlockSpec((8, tm), smem),
                pl.BlockSpec((tm, d), lambda i, *_: (i, 0)),
            ],
            out_specs=pl.BlockSpec(memory_space=pl.ANY),
            scratch_shapes=[pltpu.VMEM((2, RUN_ROWS, d // 2), jnp.uint32), pltpu.VMEM((TM_EXPERT, d // 2), jnp.uint32),
                            pltpu.SemaphoreType.DMA((2,)), pltpu.SemaphoreType.DMA(())],
        ),
        out_shape=jax.ShapeDtypeStruct((cap, d // 2), jnp.uint32),
        compiler_params=_params("arbitrary"),
        name="moe_dispatch",
    )(seg_end, padded, tile_cnt, tile_lo, tile_base, pos_t, h)


def _expert_kernel(be_ref, nxt_ref, na_ref, x_ref, wgu_hbm, bgu_ref, wd_hbm, bd_ref, o_ref,
                   gu_stage, d_stage, gu_bf, d_bf, sems, *, layer):
    i = pl.program_id(0)
    live = i < na_ref[0]
    e = be_ref[i]

    def fetch(expert):
        return (pltpu.make_async_copy(wgu_hbm.at[layer, expert], gu_stage, sems.at[0]),
                pltpu.make_async_copy(wd_hbm.at[layer, expert], d_stage, sems.at[1]))

    @pl.when(i == 0)
    def _():
        for c in fetch(e):
            c.start()

    @pl.when(live & ((i == 0) | (e != be_ref[jnp.maximum(i - 1, 0)])))
    def _():
        for c in fetch(e):
            c.wait()
        gu_bf[...] = gu_stage[...].astype(BF16)
        d_bf[...] = d_stage[...].astype(BF16)

        @pl.when(nxt_ref[i] >= 0)
        def _():
            for c in fetch(nxt_ref[i]):
                c.start()

    @pl.when(jnp.logical_not(live))
    def _():
        o_ref[...] = jnp.zeros_like(o_ref)

    @pl.when(live)
    def _():
        f = d_bf.shape[0]
        xb = _unpack_pairs(x_ref[...]).astype(BF16)
        gu = jnp.dot(xb, gu_bf[...], preferred_element_type=F32) + bgu_ref[0]
        gate = jnp.minimum(gu[:, :f], SWIGLU_LIMIT)
        up = jnp.clip(gu[:, f:], -SWIGLU_LIMIT, SWIGLU_LIMIT)
        act = (up + 1.0) * (gate * jax.nn.sigmoid(SWIGLU_ALPHA * gate))
        o_ref[...] = _pack_pairs(jnp.dot(act.astype(BF16), d_bf[...], preferred_element_type=F32) + bd_ref[0])


def _experts(xs, block_e, next_e, n_active, w_gu, b_gu, w_down, b_down, layer):
    cap, w = xs.shape
    _, e, d, f2 = w_gu.shape
    f = f2 // 2
    tm = TM_EXPERT
    nb = cap // tm
    return pl.pallas_call(
        functools.partial(_expert_kernel, layer=layer),
        grid_spec=pltpu.PrefetchScalarGridSpec(
            num_scalar_prefetch=3,
            grid=(nb,),
            in_specs=[
                pl.BlockSpec((tm, w), lambda i, be, nx, na: (jnp.minimum(i, na[0] - 1), 0)),
                pl.BlockSpec(memory_space=pl.ANY),
                pl.BlockSpec((1, 1, f2), lambda i, be, nx, na: (be[i], 0, 0)),
                pl.BlockSpec(memory_space=pl.ANY),
                pl.BlockSpec((1, 1, d), lambda i, be, nx, na: (be[i], 0, 0)),
            ],
            out_specs=pl.BlockSpec((tm, w), lambda i, be, nx, na: (i, 0)),
            scratch_shapes=[pltpu.VMEM((d, f2), F32), pltpu.VMEM((f, d), F32), pltpu.VMEM((d, f2), BF16),
                            pltpu.VMEM((f, d), BF16), pltpu.SemaphoreType.DMA((2,))],
        ),
        out_shape=jax.ShapeDtypeStruct((cap, w), xs.dtype),
        compiler_params=_params("arbitrary"),
        name="moe_experts",
    )(block_e, next_e, n_active, xs, w_gu, b_gu.reshape(e, 1, f2), w_down, b_down.reshape(e, 1, d))


def _combine_kernel(cnt_ref, lo_ref, base_ref, info_ref, x_ref, mod_ref, ys_hbm, out_ref, loc_ref, sems):
    i = pl.program_id(0)
    tm = x_ref.shape[0]
    buf = i % 2

    def copy(which, local_row, global_row, size):
        return pltpu.make_async_copy(ys_hbm.at[pl.ds(global_row, size)],
                                     loc_ref.at[which, pl.ds(local_row, size)], sems.at[which])

    def fetch(tile, which):
        _for_each_run(tile, cnt_ref, lo_ref, base_ref, lambda a, b, size: copy(which, a, b, size).start())

    @pl.when(i == 0)
    def _():
        loc_ref[...] = jnp.zeros_like(loc_ref)
        fetch(i, buf)

    @pl.when(i + 1 < pl.num_programs(0))
    def _():
        fetch(i + 1, 1 - buf)

    _for_each_run(i, cnt_ref, lo_ref, base_ref, lambda a, b, size: copy(buf, a, b, size).wait())

    info = info_ref[...]
    slot = lax.broadcasted_iota(jnp.int32, (tm, RUN_ROWS), 1).astype(F32)
    mix = jnp.zeros((tm, RUN_ROWS), F32)
    for kk in range(TOP_K):
        mix = jnp.where(slot == info[:, TOP_K + kk:TOP_K + kk + 1], info[:, 2 * TOP_K + kk:2 * TOP_K + kk + 1], mix)
    rows = _unpack_pairs(loc_ref[buf]).astype(BF16)
    y = jnp.dot(mix.astype(BF16), rows, preferred_element_type=F32)
    out_ref[...] = x_ref[...] + mod_ref[0, 5:6, :] * y


def _combine(ys, info, tile_cnt, tile_lo, tile_base, x2, mod, seq):
    n, d = x2.shape
    tm = TM_MOVE
    per_b = seq // tm
    return pl.pallas_call(
        _combine_kernel,
        grid_spec=pltpu.PrefetchScalarGridSpec(
            num_scalar_prefetch=3,
            grid=(n // tm,),
            in_specs=[
                pl.BlockSpec((tm, LANES), lambda i, *_: (i, 0)),
                pl.BlockSpec((tm, d), lambda i, *_: (i, 0)),
                pl.BlockSpec((1, 6, d), lambda i, *_: (i // per_b, 0, 0)),
                pl.BlockSpec(memory_space=pl.ANY),
            ],
            out_specs=pl.BlockSpec((tm, d), lambda i, *_: (i, 0)),
            scratch_shapes=[pltpu.VMEM((2, RUN_ROWS, ys.shape[1]), ys.dtype), pltpu.SemaphoreType.DMA((2,))],
        ),
        out_shape=jax.ShapeDtypeStruct((n, d), F32),
        compiler_params=_params("arbitrary"),
        name="moe_combine",
    )(tile_cnt, tile_lo, tile_base, info, x2, mod, ys)


def _moe(x2, mod, g2, router_w, router_b, w_gu, b_gu, w_down, b_down, layer, seq):
    n, d = x2.shape
    assert TM_MOVE == TM_ROUTE, "dispatch / combine tiles reuse the router's per-tile expert counts"
    h, info, cnt, tcnt, tcar = _router(x2, mod, g2, router_w, router_b, seq)
    counts = cnt[0, :N_EXPERTS].astype(jnp.int32)
    padded = (counts + TM_EXPERT - 1) // TM_EXPERT * TM_EXPERT
    seg_end = jnp.cumsum(padded)
    seg_start = seg_end - padded
    worst = n * TOP_K + N_EXPERTS * (n // TM_MOVE) * (RUN_ALIGN - 1)
    nb = -(-worst // TM_EXPERT) + N_EXPERTS
    cap = nb * TM_EXPERT
    block_start = jnp.arange(nb, dtype=jnp.int32) * TM_EXPERT
    block_e = jnp.minimum(jnp.sum(seg_end[None, :] <= block_start[:, None], axis=1), N_EXPERTS - 1).astype(jnp.int32)
    n_active = (seg_end[-1:] // TM_EXPERT).astype(jnp.int32)
    tile_cnt = tcnt[:, 0, :N_EXPERTS].astype(jnp.int32)
    tile_before = tcar[:, 0, :N_EXPERTS].astype(jnp.int32)
    tile_lo = jnp.cumsum(tile_cnt, axis=1) - tile_cnt
    tile_base = seg_start[None, :] + tile_before
    pos_t = info[:, TOP_K:TOP_K + 8].T
    tile_cnt, tile_lo, tile_base = (t.reshape(-1).astype(jnp.int32) for t in (tile_cnt, tile_lo, tile_base))
    ids = jnp.arange(N_EXPERTS, dtype=jnp.int32)
    later = (ids[None, :] > ids[:, None]) & (padded[None, :] > 0)
    next_of = jnp.min(jnp.where(later, ids[None, :], N_EXPERTS), axis=1)
    next_e = jnp.where(next_of < N_EXPERTS, next_of, -1).astype(jnp.int32)[block_e]
    xs = _dispatch(h, pos_t, seg_end.astype(jnp.int32), padded.astype(jnp.int32), tile_cnt, tile_lo, tile_base, cap)
    ys = _experts(xs, block_e, next_e, n_active, w_gu, b_gu, w_down, b_down, layer)
    return _combine(ys, info, tile_cnt, tile_lo, tile_base, x2, mod, seq)


def kernel(x, c, positions, ada_w, ada_b, norm1_g, norm2_g, a_w_in, a_q_gain, a_k_gain, a_w_out, b_w_in,
           b_w_gate_up, b_gate_bias, b_out_gain, b_w_out, router_w, router_b, moe_w_gu, moe_b_gu, moe_w_down,
           moe_b_down):
    batch, seq, d = x.shape
    depth = ada_w.shape[0]
    n = batch * seq
    mods = _adaln(c, ada_w, ada_b).reshape(depth, batch, 6, d)
    x2 = x.reshape(n, d)
    for layer in range(depth):
        mod = mods[layer]
        j = layer // 2
        g1 = norm1_g[layer].reshape(1, d)
        g2 = norm2_g[layer].reshape(1, d)
        if layer % 2 == 0:
            qkv = _qkv_proj(x2, mod, g1, a_w_in[j], a_q_gain[j], a_k_gain[j], positions, batch, seq)
            outs, lses = zip(*[_dilated_attention(*qkv[g], g) for g in range(N_GROUPS)])
            x2 = _merge_proj(outs, lses, x2, mod, a_w_out[j].astype(BF16), seq)
        else:
            hk = HEADS_B * KEY_DIM_B
            hv = HEADS_B * VAL_DIM_B
            width = 2 * hk + 2 * hv
            w_in = jnp.zeros((d, width + LANES), BF16).at[:, :width + GATE_RANK].set(b_w_in[j].astype(BF16))
            wg = jnp.zeros((LANES, hk), BF16).at[:GATE_RANK].set(b_w_gate_up[j].astype(BF16))
            proj = _gla_in_proj(x2, mod, g1, w_in, seq)
            x3 = _gla(proj, x2.reshape(batch, seq, d), mod, wg, b_gate_bias[j].reshape(1, hk),
                      b_out_gain[j].reshape(1, VAL_DIM_B), b_w_out[j].astype(BF16))
            x2 = x3.reshape(n, d)
        x2 = _moe(x2, mod, g2, router_w[layer], router_b[layer], moe_w_gu, moe_b_gu[layer],
                  moe_w_down, moe_b_down[layer], layer, seq)
    return x2.reshape(batch, seq, d)
```

```python
import functools

import jax
import jax.numpy as jnp
from jax import lax
from jax.experimental import pallas as pl
from jax.experimental.pallas import tpu as pltpu

F32 = jnp.float32
BF16 = jnp.bfloat16
HIGHEST = lax.Precision.HIGHEST

RMS_EPS = 1e-6
DILATIONS = ((128, 1), (512, 4), (2048, 16))
N_GROUPS = 3
HEADS_A = 16
HEAD_DIM_A = 64
BAND = 128
ROPE_THETA = 10000.0
NEG_INF = -1e30
HEADS_B = 4
KEY_DIM_B = 128
VAL_DIM_B = 256
GATE_RANK = 16
GATE_TAU = 16.0
GLA_CHUNK = 64
N_EXPERTS = 32
TOP_K = 4
SWIGLU_LIMIT = 7.0
SWIGLU_ALPHA = 1.702

LANES = 128
MXU_WIDTH = 256
VMEM_LIMIT = 56 * 1024 * 1024

TN_ADALN = 1536
TM_ROPE = 1024
TM_PROJ = 512
TM_QKV = 256
TM_ROUTE = 512
TM_EXPERT = 512
TM_MOVE = 512
ATTN_ROWS = 512
RUN_ALIGN = 8
RUN_ROWS = TOP_K * TM_MOVE + N_EXPERTS * RUN_ALIGN
T_GLA = 512


def _params(*sem):
    return pltpu.CompilerParams(dimension_semantics=sem, vmem_limit_bytes=VMEM_LIMIT)


def _norm_mod(x, gain, shift, scale):
    ms = jnp.mean(x * x, axis=-1, keepdims=True)
    y = x * lax.rsqrt(ms + RMS_EPS) * gain
    return y * (1.0 + scale) + shift


def _pack_pairs(v):
    k = v.shape[1] // 2
    bits = lax.bitcast_convert_type(v.astype(BF16).astype(F32), jnp.uint32)
    return (bits[:, :k] >> 16) | bits[:, k:]


def _unpack_pairs(w):
    lo = lax.bitcast_convert_type(w << 16, F32)
    hi = lax.bitcast_convert_type(w & jnp.uint32(0xFFFF0000), F32)
    return jnp.concatenate([lo, hi], axis=1)


def _adaln_kernel(c_ref, w_ref, b_ref, o_ref):
    c = c_ref[...]
    cond = c * jax.nn.sigmoid(c)
    o_ref[0] = jnp.dot(cond, w_ref[0], precision=HIGHEST, preferred_element_type=F32) + b_ref[0]


def _adaln(c, ada_w, ada_b):
    depth, d, n6 = ada_w.shape
    b = c.shape[0]
    tn = TN_ADALN
    return pl.pallas_call(
        _adaln_kernel,
        grid=(depth, n6 // tn),
        in_specs=[
            pl.BlockSpec((b, d), lambda l, j: (0, 0)),
            pl.BlockSpec((1, d, tn), lambda l, j: (l, 0, j)),
            pl.BlockSpec((1, 1, tn), lambda l, j: (l, 0, j)),
        ],
        out_specs=pl.BlockSpec((1, b, tn), lambda l, j: (l, 0, j)),
        out_shape=jax.ShapeDtypeStruct((depth, b, n6), F32),
        compiler_params=_params("parallel", "parallel"),
        name="adaln",
    )(c, ada_w, ada_b.reshape(depth, 1, n6))


def _rope_kernel(pos_ref, cos_ref, sin_ref, stage_ref):
    pos = pos_ref[...].astype(F32)
    lane = lax.broadcasted_iota(jnp.int32, (1, LANES), 1)
    half = HEAD_DIM_A // 2
    fidx = (lane & (half - 1)).astype(F32)
    inv_freq = jnp.power(jnp.full((1, LANES), ROPE_THETA, F32), -(fidx / half))
    ang = pos * inv_freq
    s = jnp.sin(ang)
    tables = (jnp.cos(ang), jnp.where(lane < LANES // 2, -s, s))
    tm = pos.shape[0]
    for tab, out_ref, stage in zip(tables, (cos_ref, sin_ref), (stage_ref.at[0], stage_ref.at[1])):
        out_ref[0] = tab
        stage[...] = tab
        for g in range(1, N_GROUPS):
            dil = DILATIONS[g][1]
            rows = TM_QKV // dil
            for tile in range(tm // TM_QKV):
                for r in range(dil):
                    lo = tile * TM_QKV + r * rows
                    out_ref[g, lo:lo + rows, :] = stage[pl.ds(tile * TM_QKV + r, rows, stride=dil), :]


def _rope_tables(positions):
    n = positions.size
    tm = TM_ROPE
    return pl.pallas_call(
        _rope_kernel,
        grid=(n // tm,),
        in_specs=[pl.BlockSpec((tm, 1), lambda i: (i, 0))],
        out_specs=[pl.BlockSpec((N_GROUPS, tm, LANES), lambda i: (0, i, 0))] * 2,
        out_shape=[jax.ShapeDtypeStruct((N_GROUPS, n, LANES), F32)] * 2,
        scratch_shapes=[pltpu.VMEM((2, tm, LANES), F32)],
        compiler_params=_params("parallel"),
        name="rope_tables",
    )(positions.reshape(n, 1))


def _stream_order(tm, dil):
    s = jnp.arange(tm)
    return (s % (tm // dil)) * dil + s // (tm // dil)


def _qkv_kernel(x_ref, mod_ref, g_ref, w_hbm, gain_ref, cos_ref, sin_ref, bd_ref, p1_ref, p2_ref, *rest):
    o_refs, (w_ref, w_sem) = rest[:3 * N_GROUPS], rest[3 * N_GROUPS:]
    tm, d = x_ref.shape

    @pl.when(pl.program_id(0) == 0)
    def _():
        copy = pltpu.make_async_copy(w_hbm, w_ref, w_sem)
        copy.start()
        copy.wait()

    h = _norm_mod(x_ref[...], g_ref[...], mod_ref[0, 0:1, :], mod_ref[0, 1:2, :]).astype(BF16)
    hs = [h] + [jnp.dot(p[...], h, preferred_element_type=F32).astype(BF16) for p in (p1_ref, p2_ref)]
    for jj in range(3 * N_GROUPS):
        g, comp = divmod(jj, 3)
        dil = DILATIONS[g][1]
        rows = tm // dil
        out = o_refs[jj]
        acc = jnp.dot(hs[g], w_ref[:, jj * d:(jj + 1) * d], preferred_element_type=F32)
        if comp == 2:
            val = acc.astype(BF16)
            for r in range(dil):
                out[0, r] = val[r * rows:(r + 1) * rows]
            continue
        gain = gain_ref[jj]
        cos = cos_ref[g]
        sin = sin_ref[g]
        for cc in range(d // MXU_WIDTH):
            cols = slice(cc * MXU_WIDTH, (cc + 1) * MXU_WIDTH)
            a = acc[:, cols]
            ss = jnp.dot((a * a).astype(BF16), bd_ref[...], preferred_element_type=F32)
            qn = a * lax.rsqrt(ss * (1.0 / HEAD_DIM_A) + RMS_EPS) * gain[:, cols]
            for hh in range(MXU_WIDTH // LANES):
                xx = qn[:, hh * LANES:(hh + 1) * LANES]
                val = (xx * cos + pltpu.roll(xx, LANES // 2, 1) * sin).astype(BF16)
                c1 = cc * MXU_WIDTH + hh * LANES
                for r in range(dil):
                    out[0, r, :, c1:c1 + LANES] = val[r * rows:(r + 1) * rows]


def _split_half_columns(w):
    lead = w.shape[:-1]
    w = w.reshape(*lead, HEADS_A // 2, 2, 2, HEAD_DIM_A // 2)
    return jnp.swapaxes(w, -3, -2).reshape(*lead, HEADS_A * HEAD_DIM_A)


def _qkv_proj(x2, mod, g1, w_in, q_gain, k_gain, positions, batch, seq):
    n, d = x2.shape
    tm = TM_QKV
    per_b = seq // tm
    reps = d // HEAD_DIM_A
    w = w_in.astype(BF16).reshape(d, N_GROUPS, 3, d)
    w = jnp.concatenate([_split_half_columns(w[:, :, :2]), w[:, :, 2:]], axis=2).reshape(d, 3 * N_GROUPS * d)
    gains = []
    for g in range(N_GROUPS):
        gains += [_split_half_columns(jnp.tile(q_gain[g], reps)) * (HEAD_DIM_A ** -0.5),
                  _split_half_columns(jnp.tile(k_gain[g], reps)), jnp.ones((d,), F32)]
    gains = jnp.stack(gains).reshape(3 * N_GROUPS, 1, d)
    orders = [_stream_order(tm, dil) for _, dil in DILATIONS]
    cos_t, sin_t = _rope_tables(positions)
    perms = [(o[:, None] == jnp.arange(tm)[None, :]).astype(BF16) for o in orders[1:]]
    lane = jnp.arange(MXU_WIDTH)
    head_of = (lane // LANES) * 2 + (lane // (HEAD_DIM_A // 2)) % 2
    bd = (head_of[:, None] == head_of[None, :]).astype(BF16)
    dils = [DILATIONS[jj // 3][1] for jj in range(3 * N_GROUPS)]
    const = lambda shape: pl.BlockSpec(shape, lambda i: (0,) * len(shape))
    flat = pl.pallas_call(
        _qkv_kernel,
        grid=(n // tm,),
        in_specs=[
            pl.BlockSpec((tm, d), lambda i: (i, 0)),
            pl.BlockSpec((1, 6, d), lambda i: (i // per_b, 0, 0)),
            const((1, d)),
            pl.BlockSpec(memory_space=pl.ANY),
            const((3 * N_GROUPS, 1, d)),
            pl.BlockSpec((N_GROUPS, tm, LANES), lambda i: (0, i, 0)),
            pl.BlockSpec((N_GROUPS, tm, LANES), lambda i: (0, i, 0)),
            const((MXU_WIDTH, MXU_WIDTH)),
            const((tm, tm)),
            const((tm, tm)),
        ],
        out_specs=[pl.BlockSpec((1, dil, tm // dil, d), lambda i: (i // per_b, 0, i % per_b, 0)) for dil in dils],
        out_shape=[jax.ShapeDtypeStruct((batch, dil, seq // dil, d), BF16) for dil in dils],
        scratch_shapes=[pltpu.VMEM((d, 3 * N_GROUPS * d), BF16), pltpu.SemaphoreType.DMA(())],
        compiler_params=_params("arbitrary"),
        name="qkv_proj",
    )(x2, mod, g1, w, gains, cos_t, sin_t, bd, *perms)
    return [flat[3 * g:3 * g + 3] for g in range(N_GROUPS)]


def _attn_kernel(q_ref, kp_ref, kc_ref, vp_ref, vc_ref, o_ref, lse_ref, *, steps):
    q_ref, kp_ref, kc_ref, vp_ref, vc_ref, o_ref, lse_ref = (
        r.at[0, 0] for r in (q_ref, kp_ref, kc_ref, vp_ref, vc_ref, o_ref, lse_ref))
    n = pl.program_id(2)
    row = lax.broadcasted_iota(jnp.int32, (BAND, 2 * BAND), 0)
    col = lax.broadcasted_iota(jnp.int32, (BAND, 2 * BAND), 1)
    delta = row + BAND - col
    in_band = (delta >= 0) & (delta <= steps)
    lane = lax.broadcasted_iota(jnp.int32, (BAND, LANES), 1)
    first_head = lane < HEAD_DIM_A
    even_head = (lane & (HEAD_DIM_A // 2)) == 0
    for sub in range(q_ref.shape[0] // BAND):
        rows = slice(sub * BAND, (sub + 1) * BAND)
        if sub == 0:
            k_prev, v_prev = kp_ref, vp_ref
            valid = in_band & ((col >= BAND) | (n > 0))
        else:
            k_prev, v_prev = kc_ref.at[(sub - 1) * BAND:sub * BAND], vc_ref.at[(sub - 1) * BAND:sub * BAND]
            valid = in_band
        lse_tile = jnp.zeros((BAND, LANES), F32)
        for p in range(HEADS_A // 2):
            sl = slice(p * LANES, (p + 1) * LANES)
            qp = q_ref[rows, sl]
            kcat = jnp.concatenate([k_prev[:, sl], kc_ref[rows, sl]], axis=0)
            vcat = jnp.concatenate([v_prev[:, sl], vc_ref[rows, sl]], axis=0)
            outs = []
            for hh in range(2):
                keep = even_head if hh == 0 else jnp.logical_not(even_head)
                qm = jnp.where(keep, qp, jnp.zeros_like(qp))
                s = lax.dot_general(qm, kcat, (((1,), (1,)), ((), ())), preferred_element_type=F32)
                s = jnp.where(valid, s, NEG_INF)
                m = jnp.max(s, axis=-1, keepdims=True)
                pexp = jnp.exp(s - m)
                den = jnp.sum(pexp, axis=-1, keepdims=True)
                o = jnp.dot(pexp.astype(BF16), vcat, preferred_element_type=F32) / den
                outs.append(o)
                lse_tile = jnp.where(lane == (2 * p + hh), m + jnp.log(den), lse_tile)
            o_ref[rows, sl] = jnp.where(first_head, outs[0], outs[1]).astype(BF16)
        lse_ref[rows, :] = lse_tile


def _dilated_attention(q, k, v, g):
    window, dil = DILATIONS[g]
    batch, _, l, d = q.shape
    qb = min(ATTN_ROWS, l)
    cur = pl.BlockSpec((1, 1, qb, d), lambda b, r, i: (b, r, i, 0))
    prev = pl.BlockSpec((1, 1, BAND, d), lambda b, r, i: (b, r, jnp.maximum(i * (qb // BAND) - 1, 0), 0))
    return pl.pallas_call(
        functools.partial(_attn_kernel, steps=window // dil),
        grid=(batch, dil, l // qb),
        in_specs=[cur, prev, cur, prev, cur],
        out_specs=[cur, pl.BlockSpec((1, 1, qb, LANES), lambda b, r, i: (b, r, i, 0))],
        out_shape=[
            jax.ShapeDtypeStruct((batch, dil, l, d), BF16),
            jax.ShapeDtypeStruct((batch, dil, l, LANES), F32),
        ],
        compiler_params=_params("parallel", "parallel", "arbitrary"),
        name=f"dilated_attn_g{g}",
    )(q, k, k, v, v)


def _split3(v):
    hi = v.astype(BF16)
    rem = v - hi.astype(F32)
    mid = rem.astype(BF16)
    return hi, mid, (rem - mid.astype(F32)).astype(BF16)


def _dot3(mat_bf, v):
    return sum(jnp.dot(mat_bf, part, preferred_element_type=F32) for part in _split3(v))


def _merge_proj_kernel(o0_ref, o1_ref, o2_ref, l0_ref, l1_ref, l2_ref, p1_ref, p2_ref, x_ref, mod_ref, e_ref,
                       w_ref, out_ref):
    tm, d = x_ref.shape
    perms = (None, p1_ref, p2_ref)
    lses, outs = [], []
    for perm, l_ref, o_ref in zip(perms, (l0_ref, l1_ref, l2_ref), (o0_ref, o1_ref, o2_ref)):
        lse = l_ref[0].reshape(tm, LANES)
        o = o_ref[0].reshape(tm, d)
        if perm is None:
            lses.append(lse)
            outs.append(o.astype(F32))
        else:
            lses.append(_dot3(perm[...], lse))
            outs.append(jnp.dot(perm[...], o, preferred_element_type=F32))
    m = jnp.maximum(jnp.maximum(lses[0], lses[1]), lses[2])
    exps = [jnp.exp(l - m) for l in lses]
    inv = 1.0 / (exps[0] + exps[1] + exps[2])
    acc, rest = None, None
    for e, o in zip(exps, outs):
        if o is outs[-1]:
            wide = 1.0 - rest
        else:
            w = e * inv
            w_hi = w.astype(BF16)
            w_lo = (w - w_hi.astype(F32)).astype(BF16)
            wide = (jnp.dot(w_hi, e_ref[...], preferred_element_type=F32)
                    + jnp.dot(w_lo, e_ref[...], preferred_element_type=F32))
            rest = wide if rest is None else rest + wide
        acc = wide * o if acc is None else acc + wide * o
    y = jnp.dot(acc.astype(BF16), w_ref[...], preferred_element_type=F32)
    out_ref[...] = x_ref[...] + mod_ref[0, 2:3, :] * y


def _merge_proj(outs, lses, x2, mod, w_out_bf, seq):
    n, d = x2.shape
    tm = TM_PROJ
    per_b = seq // tm
    expand = (jnp.arange(LANES)[:, None] == jnp.arange(d)[None, :] // HEAD_DIM_A).astype(BF16)
    tok = jnp.arange(tm)
    perms = [(((tok % dil) * (tm // dil) + tok // dil)[:, None] == tok[None, :]).astype(BF16)
             for _, dil in DILATIONS[1:]]
    slab = lambda dil, width: pl.BlockSpec((1, dil, tm // dil, width), lambda i: (i // per_b, 0, i % per_b, 0))
    const = lambda shape: pl.BlockSpec(shape, lambda i: (0, 0))
    return pl.pallas_call(
        _merge_proj_kernel,
        grid=(n // tm,),
        in_specs=[slab(dil, d) for _, dil in DILATIONS] + [slab(dil, LANES) for _, dil in DILATIONS] + [
            const((tm, tm)),
            const((tm, tm)),
            pl.BlockSpec((tm, d), lambda i: (i, 0)),
            pl.BlockSpec((1, 6, d), lambda i: (i // per_b, 0, 0)),
            const((LANES, d)),
            const((d, d)),
        ],
        out_specs=pl.BlockSpec((tm, d), lambda i: (i, 0)),
        out_shape=jax.ShapeDtypeStruct((n, d), F32),
        compiler_params=_params("parallel"),
        name="merge_out_proj",
    )(*outs, *lses, *perms, x2, mod, expand, w_out_bf)


def _gla_in_kernel(x_ref, mod_ref, g_ref, w_ref, o_ref):
    h = _norm_mod(x_ref[...], g_ref[...], mod_ref[0, 0:1, :], mod_ref[0, 1:2, :])
    o_ref[...] = jnp.dot(h.astype(BF16), w_ref[...], preferred_element_type=F32).astype(BF16)


def _gla_in_proj(x2, mod, g1, w_bf, seq):
    n, d = x2.shape
    ncol = w_bf.shape[1]
    tm = TM_PROJ
    per_b = seq // tm
    return pl.pallas_call(
        _gla_in_kernel,
        grid=(n // tm,),
        in_specs=[
            pl.BlockSpec((tm, d), lambda i: (i, 0)),
            pl.BlockSpec((1, 6, d), lambda i: (i // per_b, 0, 0)),
            pl.BlockSpec((1, d), lambda i: (0, 0)),
            pl.BlockSpec((d, ncol), lambda i: (0, 0)),
        ],
        out_specs=pl.BlockSpec((tm, ncol), lambda i: (i, 0)),
        out_shape=jax.ShapeDtypeStruct((n, ncol), BF16),
        compiler_params=_params("parallel"),
        name="gla_in_proj",
    )(x2, mod, g1, w_bf)


def _gla_kernel(q_ref, k_ref, v_ref, r_ref, a_ref, wg_ref, gb_ref, og_ref, tri_ref, x_ref, mod_ref, wo_ref,
                out_ref, st_ref, o_scr, la_scr):
    t = pl.program_id(1)
    c = GLA_CHUNK
    nb = q_ref.shape[0]

    @pl.when(t == 0)
    def _():
        st_ref[...] = jnp.zeros_like(st_ref)

    for bb in range(nb):
        g = jnp.dot(a_ref[bb], wg_ref[...], preferred_element_type=F32) + gb_ref[...]
        la_scr[bb] = (jnp.minimum(g, 0.0) - jnp.log(1.0 + jnp.exp(-jnp.abs(g)))) * (1.0 / GATE_TAU)

    rr = lax.broadcasted_iota(jnp.int32, (c, c), 0)
    cc = lax.broadcasted_iota(jnp.int32, (c, c), 1)
    causal = cc <= rr

    def chunk(ci, carry):
        rows = pl.ds(pl.multiple_of(ci * c, c), c)
        for bb in range(nb):
            la = la_scr[bb, rows, :]
            b = _dot3(tri_ref[...], la)
            b_last = b[c - 1:c, :]
            q = q_ref[bb, rows, :].astype(F32) * (KEY_DIM_B ** -0.5)
            k = k_ref[bb, rows, :].astype(F32)
            q_dec = (q * jnp.exp(b)).astype(BF16)
            k_inv = (k * jnp.exp(-b)).astype(BF16)
            k_dec = (k * jnp.exp(b_last - b)).astype(BF16)
            decay = jnp.exp(b_last)
            v = v_ref[bb, rows, :]
            for h in range(HEADS_B):
                ks = slice(h * KEY_DIM_B, (h + 1) * KEY_DIM_B)
                vs = slice(h * VAL_DIM_B, (h + 1) * VAL_DIM_B)
                qd, ki, kd, vh = q_dec[:, ks], k_inv[:, ks], k_dec[:, ks], v[:, vs]
                att = lax.dot_general(qd, ki, (((1,), (1,)), ((), ())), preferred_element_type=F32)
                att = jnp.where(causal, att, 0.0)
                st = st_ref[bb, h]
                o = (jnp.dot(att.astype(BF16), vh, preferred_element_type=F32)
                     + lax.dot_general(qd, st.astype(BF16), (((1,), (1,)), ((), ())), preferred_element_type=F32))
                o_scr[bb, rows, vs] = o
                st_ref[bb, h] = st * decay[:, ks] + lax.dot_general(vh, kd, (((0,), (0,)), ((), ())),
                                                                     preferred_element_type=F32)
        return carry

    lax.fori_loop(0, q_ref.shape[1] // c, chunk, 0, unroll=2)

    for bb in range(nb):
        r = r_ref[bb].astype(F32)
        parts = []
        for h in range(HEADS_B):
            vs = slice(h * VAL_DIM_B, (h + 1) * VAL_DIM_B)
            oh = o_scr[bb, :, vs]
            ms = jnp.mean(oh * oh, axis=-1, keepdims=True)
            rh = r[:, vs]
            parts.append((oh * lax.rsqrt(ms + RMS_EPS) * og_ref[...] * (rh * jax.nn.sigmoid(rh))).astype(BF16))
        y = jnp.dot(jnp.concatenate(parts, axis=1), wo_ref[...], preferred_element_type=F32)
        out_ref[bb] = x_ref[bb] + mod_ref[bb, 2:3, :] * y


def _gla(proj, x3, mod, wg_bf, gate_bias, out_gain, w_out_bf):
    batch, seq, d = x3.shape
    ncol = proj.shape[1]
    t = T_GLA
    hk = HEADS_B * KEY_DIM_B
    hv = HEADS_B * VAL_DIM_B
    p3 = proj.reshape(batch, seq, ncol)
    tri = (jnp.arange(GLA_CHUNK)[None, :] <= jnp.arange(GLA_CHUNK)[:, None]).astype(BF16)
    a_blk = (2 * hk + 2 * hv) // LANES
    nb = 1
    const = lambda shape: pl.BlockSpec(shape, lambda b, i: (0,) * len(shape))
    return pl.pallas_call(
        _gla_kernel,
        grid=(batch // nb, seq // t),
        in_specs=[
            pl.BlockSpec((nb, t, hk), lambda b, i: (b, i, 0)),
            pl.BlockSpec((nb, t, hk), lambda b, i: (b, i, 1)),
            pl.BlockSpec((nb, t, hv), lambda b, i: (b, i, (2 * hk) // hv)),
            pl.BlockSpec((nb, t, hv), lambda b, i: (b, i, (2 * hk + hv) // hv)),
            pl.BlockSpec((nb, t, LANES), lambda b, i: (b, i, a_blk)),
            const((LANES, hk)),
            const((1, hk)),
            const((1, VAL_DIM_B)),
            const((GLA_CHUNK, GLA_CHUNK)),
            pl.BlockSpec((nb, t, d), lambda b, i: (b, i, 0)),
            pl.BlockSpec((nb, 6, d), lambda b, i: (b, 0, 0)),
            const((hv, d)),
        ],
        out_specs=pl.BlockSpec((nb, t, d), lambda b, i: (b, i, 0)),
        out_shape=jax.ShapeDtypeStruct((batch, seq, d), F32),
        scratch_shapes=[
            pltpu.VMEM((nb, HEADS_B, VAL_DIM_B, KEY_DIM_B), F32),
            pltpu.VMEM((nb, t, hv), F32),
            pltpu.VMEM((nb, t, hk), F32),
        ],
        compiler_params=_params("parallel", "arbitrary"),
        name="gla",
    )(p3, p3, p3, p3, p3, wg_bf, gate_bias, out_gain, tri, x3, mod, w_out_bf)


def _router_kernel(x_ref, mod_ref, g_ref, rw_ref, rb_ref, tri_ref, upper_ref, h_ref, info_ref, cnt_ref, tcnt_ref,
                   tcar_ref, carry_ref):
    i = pl.program_id(0)

    @pl.when(i == 0)
    def _():
        carry_ref[...] = jnp.zeros_like(carry_ref)

    h = _norm_mod(x_ref[...], g_ref[...], mod_ref[0, 3:4, :], mod_ref[0, 4:5, :])
    h_ref[...] = h.astype(BF16)
    tm = h.shape[0]
    lane = lax.broadcasted_iota(jnp.int32, (tm, LANES), 1)
    lane_f = lane.astype(F32)
    h_hi = h.astype(BF16)
    h_lo = (h - h_hi.astype(F32)).astype(BF16)
    logits = (jnp.dot(h_hi, rw_ref[0], preferred_element_type=F32) + jnp.dot(h_lo, rw_ref[0], preferred_element_type=F32)
              + jnp.dot(h_hi, rw_ref[1], preferred_element_type=F32) + rb_ref[...])
    work = jnp.where(lane < N_EXPERTS, logits, -jnp.inf)
    picks, vals, idxs = [], [], []
    for _ in range(TOP_K):
        m = jnp.max(work, axis=-1, keepdims=True)
        idx = jnp.min(jnp.where(work == m, lane_f, float(LANES)), axis=-1, keepdims=True)
        pick = lane_f == idx
        work = jnp.where(pick, -jnp.inf, work)
        picks.append(pick)
        vals.append(m)
        idxs.append(idx)
    exps = [jnp.exp(v - vals[0]) for v in vals]
    inv = 1.0 / (exps[0] + exps[1] + exps[2] + exps[3])
    chosen = jnp.zeros((tm, LANES), F32)
    for pick in picks:
        chosen = jnp.where(pick, 1.0, chosen)
    here = jnp.floor((jnp.sum(chosen, axis=0, keepdims=True) + (RUN_ALIGN - 1)) * (1.0 / RUN_ALIGN)) * RUN_ALIGN
    run_start = jnp.dot(jnp.broadcast_to(here, (8, LANES)).astype(BF16), upper_ref[...],
                        preferred_element_type=F32)[0:1, :]
    sorted_row = jnp.dot(tri_ref[...], chosen.astype(BF16), preferred_element_type=F32) + run_start
    info = jnp.zeros((tm, LANES), F32)
    for kk in range(TOP_K):
        row = jnp.sum(jnp.where(picks[kk], sorted_row, 0.0), axis=-1, keepdims=True)
        info = jnp.where(lane == kk, idxs[kk], info)
        info = jnp.where(lane == TOP_K + kk, row, info)
        info = jnp.where(lane == 2 * TOP_K + kk, exps[kk] * inv, info)
    info_ref[...] = info
    tcnt_ref[0] = jnp.broadcast_to(here, tcnt_ref.shape[1:])
    tcar_ref[0] = carry_ref[...]
    total = carry_ref[0:1, :] + here
    carry_ref[...] = jnp.broadcast_to(total, carry_ref.shape)
    cnt_ref[...] = jnp.broadcast_to(total, cnt_ref.shape)


def _router(x2, mod, g2, router_w, router_b, seq):
    n, d = x2.shape
    tm = TM_ROUTE
    per_b = seq // tm
    rw = jnp.zeros((d, LANES), F32).at[:, :N_EXPERTS].set(router_w)
    rw_hi = rw.astype(BF16)
    rw = jnp.stack([rw_hi, (rw - rw_hi.astype(F32)).astype(BF16)])
    rb = jnp.zeros((1, LANES), F32).at[0, :N_EXPERTS].set(router_b)
    tri = (jnp.arange(tm)[None, :] < jnp.arange(tm)[:, None]).astype(BF16)
    upper = (jnp.arange(LANES)[:, None] < jnp.arange(LANES)[None, :]).astype(BF16)
    return pl.pallas_call(
        _router_kernel,
        grid=(n // tm,),
        in_specs=[
            pl.BlockSpec((tm, d), lambda i: (i, 0)),
            pl.BlockSpec((1, 6, d), lambda i: (i // per_b, 0, 0)),
            pl.BlockSpec((1, d), lambda i: (0, 0)),
            pl.BlockSpec((2, d, LANES), lambda i: (0, 0, 0)),
            pl.BlockSpec((1, LANES), lambda i: (0, 0)),
            pl.BlockSpec((tm, tm), lambda i: (0, 0)),
            pl.BlockSpec((LANES, LANES), lambda i: (0, 0)),
        ],
        out_specs=[
            pl.BlockSpec((tm, d), lambda i: (i, 0)),
            pl.BlockSpec((tm, LANES), lambda i: (i, 0)),
            pl.BlockSpec((8, LANES), lambda i: (0, 0)),
            pl.BlockSpec((1, 8, LANES), lambda i: (i, 0, 0)),
            pl.BlockSpec((1, 8, LANES), lambda i: (i, 0, 0)),
        ],
        out_shape=[
            jax.ShapeDtypeStruct((n, d), BF16),
            jax.ShapeDtypeStruct((n, LANES), F32),
            jax.ShapeDtypeStruct((8, LANES), F32),
            jax.ShapeDtypeStruct((n // tm, 8, LANES), F32),
            jax.ShapeDtypeStruct((n // tm, 8, LANES), F32),
        ],
        scratch_shapes=[pltpu.VMEM((8, LANES), F32)],
        compiler_params=_params("arbitrary"),
        name="moe_router",
    )(x2, mod, g2, rw, rb, tri, upper)


def _for_each_run(tile, cnt_ref, lo_ref, base_ref, act):
    bits = [1 << k for k in range(TM_MOVE.bit_length() - 1, RUN_ALIGN.bit_length() - 2, -1)]

    def one(e, carry):
        j = tile * N_EXPERTS + e
        cnt, lo, base = cnt_ref[j], lo_ref[j], base_ref[j]
        off = jnp.int32(0)
        for bit in bits:
            take = (cnt & bit) != 0
            pl.when(take)(functools.partial(act, pl.multiple_of(lo + off, RUN_ALIGN),
                                            pl.multiple_of(base + off, RUN_ALIGN), bit))
            off = off + jnp.where(take, bit, 0)
        return carry

    lax.fori_loop(0, N_EXPERTS, one, 0)


def _dispatch_kernel(seg_end_ref, padded_ref, cnt_ref, lo_ref, base_ref, pos_ref, h_ref, xs_hbm,
                     loc_ref, zero_ref, sems, zero_sem):
    i = pl.program_id(0)
    tm, d = h_ref.shape

    @pl.when(pl.program_id(0) == 0)
    def _():
        zero_ref[...] = jnp.zeros_like(zero_ref)

        def tail(e):
            return pltpu.make_async_copy(
                zero_ref, xs_hbm.at[pl.ds(pl.multiple_of(seg_end_ref[e] - TM_EXPERT, TM_EXPERT), TM_EXPERT)], zero_sem)

        def start(e, carry):
            pl.when(padded_ref[e] > 0)(lambda: tail(e).start())
            return carry

        def wait(e, carry):
            pl.when(padded_ref[e] > 0)(lambda: tail(e).wait())
            return carry

        lax.fori_loop(0, N_EXPERTS, start, 0)
        lax.fori_loop(0, N_EXPERTS, wait, 0)

        def spare(blk):
            return pltpu.make_async_copy(
                zero_ref, xs_hbm.at[pl.ds(pl.multiple_of(blk * TM_EXPERT, TM_EXPERT), TM_EXPERT)], zero_sem)

        used = seg_end_ref[N_EXPERTS - 1] // TM_EXPERT
        total = xs_hbm.shape[0] // TM_EXPERT
        lax.fori_loop(used, total, lambda blk, carry: (spare(blk).start(), carry)[1], 0)
        lax.fori_loop(used, total, lambda blk, carry: (spare(blk).wait(), carry)[1], 0)

    slot = lax.broadcasted_iota(jnp.int32, (RUN_ROWS, tm), 0).astype(F32)
    perm = jnp.zeros((RUN_ROWS, tm), F32)
    for kk in range(TOP_K):
        perm = jnp.where(slot == pos_ref[kk:kk + 1, :], 1.0, perm)
    perm = perm.astype(BF16)
    buf = i % 2
    loc_ref[buf] = _pack_pairs(jnp.dot(perm, h_ref[...], preferred_element_type=F32))

    def copy(which, local_row, global_row, size):
        return pltpu.make_async_copy(loc_ref.at[which, pl.ds(local_row, size)],
                                     xs_hbm.at[pl.ds(global_row, size)], sems.at[which])

    _for_each_run(i, cnt_ref, lo_ref, base_ref, lambda a, b, size: copy(buf, a, b, size).start())

    @pl.when(i > 0)
    def _():
        _for_each_run(i - 1, cnt_ref, lo_ref, base_ref, lambda a, b, size: copy(1 - buf, a, b, size).wait())

    @pl.when(i == pl.num_programs(0) - 1)
    def _():
        _for_each_run(i, cnt_ref, lo_ref, base_ref, lambda a, b, size: copy(buf, a, b, size).wait())


def _dispatch(h, pos_t, seg_end, padded, tile_cnt, tile_lo, tile_base, cap):
    n, d = h.shape
    tm = TM_MOVE
    smem = lambda i, *_: (0, i)
    return pl.pallas_call(
        _dispatch_kernel,
        grid_spec=pltpu.PrefetchScalarGridSpec(
            num_scalar_prefetch=5,
            grid=(n // tm,),
            in_specs=[
                pl.BlockSpec((8, tm), smem),
                pl.BlockSpec((tm, d), lambda i, *_: (i, 0)),
            ],
            out_specs=pl.BlockSpec(memory_space=pl.ANY),
            scratch_shapes=[pltpu.VMEM((2, RUN_ROWS, d // 2), jnp.uint32), pltpu.VMEM((TM_EXPERT, d // 2), jnp.uint32),
                            pltpu.SemaphoreType.DMA((2,)), pltpu.SemaphoreType.DMA(())],
        ),
        out_shape=jax.ShapeDtypeStruct((cap, d // 2), jnp.uint32),
        compiler_params=_params("arbitrary"),
        name="moe_dispatch",
    )(seg_end, padded, tile_cnt, tile_lo, tile_base, pos_t, h)


def _expert_kernel(be_ref, nxt_ref, na_ref, x_ref, wgu_hbm, bgu_ref, wd_hbm, bd_ref, o_ref,
                   gu_stage, d_stage, gu_bf, d_bf, sems, *, layer):
    i = pl.program_id(0)
    live = i < na_ref[0]
    e = be_ref[i]

    def fetch(expert):
        return (pltpu.make_async_copy(wgu_hbm.at[layer, expert], gu_stage, sems.at[0]),
                pltpu.make_async_copy(wd_hbm.at[layer, expert], d_stage, sems.at[1]))

    @pl.when(i == 0)
    def _():
        for c in fetch(e):
            c.start()

    @pl.when(live & ((i == 0) | (e != be_ref[jnp.maximum(i - 1, 0)])))
    def _():
        for c in fetch(e):
            c.wait()
        gu_bf[...] = gu_stage[...].astype(BF16)
        d_bf[...] = d_stage[...].astype(BF16)

        @pl.when(nxt_ref[i] >= 0)
        def _():
            for c in fetch(nxt_ref[i]):
                c.start()

    @pl.when(jnp.logical_not(live))
    def _():
        o_ref[...] = jnp.zeros_like(o_ref)

    @pl.when(live)
    def _():
        f = d_bf.shape[0]
        xb = _unpack_pairs(x_ref[...]).astype(BF16)
        gu = jnp.dot(xb, gu_bf[...], preferred_element_type=F32) + bgu_ref[0]
        gate = jnp.minimum(gu[:, :f], SWIGLU_LIMIT)
        up = jnp.clip(gu[:, f:], -SWIGLU_LIMIT, SWIGLU_LIMIT)
        act = (up + 1.0) * (gate * jax.nn.sigmoid(SWIGLU_ALPHA * gate))
        o_ref[...] = _pack_pairs(jnp.dot(act.astype(BF16), d_bf[...], preferred_element_type=F32) + bd_ref[0])


def _experts(xs, block_e, next_e, n_active, w_gu, b_gu, w_down, b_down, layer):
    cap, w = xs.shape
    _, e, d, f2 = w_gu.shape
    f = f2 // 2
    tm = TM_EXPERT
    nb = cap // tm
    return pl.pallas_call(
        functools.partial(_expert_kernel, layer=layer),
        grid_spec=pltpu.PrefetchScalarGridSpec(
            num_scalar_prefetch=3,
            grid=(nb,),
            in_specs=[
                pl.BlockSpec((tm, w), lambda i, be, nx, na: (jnp.minimum(i, na[0] - 1), 0)),
                pl.BlockSpec(memory_space=pl.ANY),
                pl.BlockSpec((1, 1, f2), lambda i, be, nx, na: (be[i], 0, 0)),
                pl.BlockSpec(memory_space=pl.ANY),
                pl.BlockSpec((1, 1, d), lambda i, be, nx, na: (be[i], 0, 0)),
            ],
            out_specs=pl.BlockSpec((tm, w), lambda i, be, nx, na: (i, 0)),
            scratch_shapes=[pltpu.VMEM((d, f2), F32), pltpu.VMEM((f, d), F32), pltpu.VMEM((d, f2), BF16),
                            pltpu.VMEM((f, d), BF16), pltpu.SemaphoreType.DMA((2,))],
        ),
        out_shape=jax.ShapeDtypeStruct((cap, w), xs.dtype),
        compiler_params=_params("arbitrary"),
        name="moe_experts",
    )(block_e, next_e, n_active, xs, w_gu, b_gu.reshape(e, 1, f2), w_down, b_down.reshape(e, 1, d))


def _combine_kernel(cnt_ref, lo_ref, base_ref, info_ref, x_ref, mod_ref, ys_hbm, out_ref, loc_ref, sems):
    i = pl.program_id(0)
    tm = x_ref.shape[0]
    buf = i % 2

    def copy(which, local_row, global_row, size):
        return pltpu.make_async_copy(ys_hbm.at[pl.ds(global_row, size)],
                                     loc_ref.at[which, pl.ds(local_row, size)], sems.at[which])

    def fetch(tile, which):
        _for_each_run(tile, cnt_ref, lo_ref, base_ref, lambda a, b, size: copy(which, a, b, size).start())

    @pl.when(i == 0)
    def _():
        loc_ref[...] = jnp.zeros_like(loc_ref)
        fetch(i, buf)

    @pl.when(i + 1 < pl.num_programs(0))
    def _():
        fetch(i + 1, 1 - buf)

    _for_each_run(i, cnt_ref, lo_ref, base_ref, lambda a, b, size: copy(buf, a, b, size).wait())

    info = info_ref[...]
    slot = lax.broadcasted_iota(jnp.int32, (tm, RUN_ROWS), 1).astype(F32)
    mix = jnp.zeros((tm, RUN_ROWS), F32)
    for kk in range(TOP_K):
        mix = jnp.where(slot == info[:, TOP_K + kk:TOP_K + kk + 1], info[:, 2 * TOP_K + kk:2 * TOP_K + kk + 1], mix)
    rows = _unpack_pairs(loc_ref[buf]).astype(BF16)
    y = jnp.dot(mix.astype(BF16), rows, preferred_element_type=F32)
    out_ref[...] = x_ref[...] + mod_ref[0, 5:6, :] * y


def _combine(ys, info, tile_cnt, tile_lo, tile_base, x2, mod, seq):
    n, d = x2.shape
    tm = TM_MOVE
    per_b = seq // tm
    return pl.pallas_call(
        _combine_kernel,
        grid_spec=pltpu.PrefetchScalarGridSpec(
            num_scalar_prefetch=3,
            grid=(n // tm,),
            in_specs=[
                pl.BlockSpec((tm, LANES), lambda i, *_: (i, 0)),
                pl.BlockSpec((tm, d), lambda i, *_: (i, 0)),
                pl.BlockSpec((1, 6, d), lambda i, *_: (i // per_b, 0, 0)),
                pl.BlockSpec(memory_space=pl.ANY),
            ],
            out_specs=pl.BlockSpec((tm, d), lambda i, *_: (i, 0)),
            scratch_shapes=[pltpu.VMEM((2, RUN_ROWS, ys.shape[1]), ys.dtype), pltpu.SemaphoreType.DMA((2,))],
        ),
        out_shape=jax.ShapeDtypeStruct((n, d), F32),
        compiler_params=_params("arbitrary"),
        name="moe_combine",
    )(tile_cnt, tile_lo, tile_base, info, x2, mod, ys)


def _moe(x2, mod, g2, router_w, router_b, w_gu, b_gu, w_down, b_down, layer, seq):
    n, d = x2.shape
    assert TM_MOVE == TM_ROUTE, "dispatch / combine tiles reuse the router's per-tile expert counts"
    h, info, cnt, tcnt, tcar = _router(x2, mod, g2, router_w, router_b, seq)
    counts = cnt[0, :N_EXPERTS].astype(jnp.int32)
    padded = (counts + TM_EXPERT - 1) // TM_EXPERT * TM_EXPERT
    seg_end = jnp.cumsum(padded)
    seg_start = seg_end - padded
    worst = n * TOP_K + N_EXPERTS * (n // TM_MOVE) * (RUN_ALIGN - 1)
    nb = -(-worst // TM_EXPERT) + N_EXPERTS
    cap = nb * TM_EXPERT
    block_start = jnp.arange(nb, dtype=jnp.int32) * TM_EXPERT
    block_e = jnp.minimum(jnp.sum(seg_end[None, :] <= block_start[:, None], axis=1), N_EXPERTS - 1).astype(jnp.int32)
    n_active = (seg_end[-1:] // TM_EXPERT).astype(jnp.int32)
    tile_cnt = tcnt[:, 0, :N_EXPERTS].astype(jnp.int32)
    tile_before = tcar[:, 0, :N_EXPERTS].astype(jnp.int32)
    tile_lo = jnp.cumsum(tile_cnt, axis=1) - tile_cnt
    tile_base = seg_start[None, :] + tile_before
    pos_t = info[:, TOP_K:TOP_K + 8].T
    tile_cnt, tile_lo, tile_base = (t.reshape(-1).astype(jnp.int32) for t in (tile_cnt, tile_lo, tile_base))
    ids = jnp.arange(N_EXPERTS, dtype=jnp.int32)
    later = (ids[None, :] > ids[:, None]) & (padded[None, :] > 0)
    next_of = jnp.min(jnp.where(later, ids[None, :], N_EXPERTS), axis=1)
    next_e = jnp.where(next_of < N_EXPERTS, next_of, -1).astype(jnp.int32)[block_e]
    xs = _dispatch(h, pos_t, seg_end.astype(jnp.int32), padded.astype(jnp.int32), tile_cnt, tile_lo, tile_base, cap)
    ys = _experts(xs, block_e, next_e, n_active, w_gu, b_gu, w_down, b_down, layer)
    return _combine(ys, info, tile_cnt, tile_lo, tile_base, x2, mod, seq)


def kernel(x, c, positions, ada_w, ada_b, norm1_g, norm2_g, a_w_in, a_q_gain, a_k_gain, a_w_out, b_w_in,
           b_w_gate_up, b_gate_bias, b_out_gain, b_w_out, router_w, router_b, moe_w_gu, moe_b_gu, moe_w_down,
           moe_b_down):
    batch, seq, d = x.shape
    depth = ada_w.shape[0]
    n = batch * seq
    mods = _adaln(c, ada_w, ada_b).reshape(depth, batch, 6, d)
    x2 = x.reshape(n, d)
    for layer in range(depth):
        mod = mods[layer]
        j = layer // 2
        g1 = norm1_g[layer].reshape(1, d)
        g2 = norm2_g[layer].reshape(1, d)
        if layer % 2 == 0:
            qkv = _qkv_proj(x2, mod, g1, a_w_in[j], a_q_gain[j], a_k_gain[j], positions, batch, seq)
            outs, lses = zip(*[_dilated_attention(*qkv[g], g) for g in range(N_GROUPS)])
            x2 = _merge_proj(outs, lses, x2, mod, a_w_out[j].astype(BF16), seq)
        else:
            hk = HEADS_B * KEY_DIM_B
            hv = HEADS_B * VAL_DIM_B
            width = 2 * hk + 2 * hv
            w_in = jnp.zeros((d, width + LANES), BF16).at[:, :width + GATE_RANK].set(b_w_in[j].astype(BF16))
            wg = jnp.zeros((LANES, hk), BF16).at[:GATE_RANK].set(b_w_gate_up[j].astype(BF16))
            proj = _gla_in_proj(x2, mod, g1, w_in, seq)
            x3 = _gla(proj, x2.reshape(batch, seq, d), mod, wg, b_gate_bias[j].reshape(1, hk),
                      b_out_gain[j].reshape(1, VAL_DIM_B), b_w_out[j].astype(BF16))
            x2 = x3.reshape(n, d)
        x2 = _moe(x2, mod, g2, router_w[layer], router_b[layer], moe_w_gu, moe_b_gu[layer],
                  moe_w_down, moe_b_down[layer], layer, seq)
    return x2.reshape(batch, seq, d)
```

```python
import functools

import jax
import jax.numpy as jnp
from jax import lax
from jax.experimental import pallas as pl
from jax.experimental.pallas import tpu as pltpu

F32 = jnp.float32
BF16 = jnp.bfloat16
HIGHEST = lax.Precision.HIGHEST

RMS_EPS = 1e-6
DILATIONS = ((128, 1), (512, 4), (2048, 16))
N_GROUPS = 3
HEADS_A = 16
HEAD_DIM_A = 64
BAND = 128
ROPE_THETA = 10000.0
NEG_INF = -1e30
HEADS_B = 4
KEY_DIM_B = 128
VAL_DIM_B = 256
GATE_RANK = 16
GATE_TAU = 16.0
GLA_CHUNK = 64
N_EXPERTS = 32
TOP_K = 4
SWIGLU_LIMIT = 7.0
SWIGLU_ALPHA = 1.702

LANES = 128
MXU_WIDTH = 256
VMEM_LIMIT = 56 * 1024 * 1024

TN_ADALN = 1536
TM_ROPE = 1024
TM_PROJ = 512
TM_QKV = 256
TM_MERGE = 256
TM_ROUTE = 512
TM_EXPERT = 512
TM_MOVE = 512
ATTN_ROWS = 512
RUN_ALIGN = 8
RUN_ROWS = TOP_K * TM_MOVE + N_EXPERTS * RUN_ALIGN
T_GLA = 512


def _params(*sem):
    return pltpu.CompilerParams(dimension_semantics=sem, vmem_limit_bytes=VMEM_LIMIT)


def _norm_mod(x, gain, shift, scale):
    ms = jnp.mean(x * x, axis=-1, keepdims=True)
    y = x * lax.rsqrt(ms + RMS_EPS) * gain
    return y * (1.0 + scale) + shift


def _pack_pairs(v):
    k = v.shape[1] // 2
    bits = lax.bitcast_convert_type(v.astype(BF16).astype(F32), jnp.uint32)
    return (bits[:, :k] >> 16) | bits[:, k:]


def _unpack_pairs(w):
    lo = lax.bitcast_convert_type(w << 16, F32)
    hi = lax.bitcast_convert_type(w & jnp.uint32(0xFFFF0000), F32)
    return jnp.concatenate([lo, hi], axis=1)


def _adaln_kernel(c_ref, w_ref, b_ref, o_ref):
    c = c_ref[...]
    cond = c * jax.nn.sigmoid(c)
    o_ref[0] = jnp.dot(cond, w_ref[0], precision=HIGHEST, preferred_element_type=F32) + b_ref[0]


def _adaln(c, ada_w, ada_b):
    depth, d, n6 = ada_w.shape
    b = c.shape[0]
    tn = TN_ADALN
    return pl.pallas_call(
        _adaln_kernel,
        grid=(depth, n6 // tn),
        in_specs=[
            pl.BlockSpec((b, d), lambda l, j: (0, 0)),
            pl.BlockSpec((1, d, tn), lambda l, j: (l, 0, j)),
            pl.BlockSpec((1, 1, tn), lambda l, j: (l, 0, j)),
        ],
        out_specs=pl.BlockSpec((1, b, tn), lambda l, j: (l, 0, j)),
        out_shape=jax.ShapeDtypeStruct((depth, b, n6), F32),
        compiler_params=_params("parallel", "parallel"),
        name="adaln",
    )(c, ada_w, ada_b.reshape(depth, 1, n6))


def _rope_kernel(pos_ref, cos_ref, sin_ref, stage_ref):
    pos = pos_ref[...].astype(F32)
    lane = lax.broadcasted_iota(jnp.int32, (1, LANES), 1)
    half = HEAD_DIM_A // 2
    fidx = (lane & (half - 1)).astype(F32)
    inv_freq = jnp.power(jnp.full((1, LANES), ROPE_THETA, F32), -(fidx / half))
    ang = pos * inv_freq
    s = jnp.sin(ang)
    tables = (jnp.cos(ang), jnp.where(lane < LANES // 2, -s, s))
    tm = pos.shape[0]
    for tab, out_ref, stage in zip(tables, (cos_ref, sin_ref), (stage_ref.at[0], stage_ref.at[1])):
        out_ref[0] = tab
        stage[...] = tab
        for g in range(1, N_GROUPS):
            dil = DILATIONS[g][1]
            rows = TM_QKV // dil
            for tile in range(tm // TM_QKV):
                for r in range(dil):
                    lo = tile * TM_QKV + r * rows
                    out_ref[g, lo:lo + rows, :] = stage[pl.ds(tile * TM_QKV + r, rows, stride=dil), :]


def _rope_tables(positions):
    n = positions.size
    tm = TM_ROPE
    return pl.pallas_call(
        _rope_kernel,
        grid=(n // tm,),
        in_specs=[pl.BlockSpec((tm, 1), lambda i: (i, 0))],
        out_specs=[pl.BlockSpec((N_GROUPS, tm, LANES), lambda i: (0, i, 0))] * 2,
        out_shape=[jax.ShapeDtypeStruct((N_GROUPS, n, LANES), F32)] * 2,
        scratch_shapes=[pltpu.VMEM((2, tm, LANES), F32)],
        compiler_params=_params("parallel"),
        name="rope_tables",
    )(positions.reshape(n, 1))


def _stream_order(tm, dil):
    s = jnp.arange(tm)
    return (s % (tm // dil)) * dil + s // (tm // dil)


def _qkv_kernel(x_ref, mod_ref, g_ref, w_hbm, gain_ref, cos_ref, sin_ref, bd_ref, p1_ref, p2_ref, *rest):
    o_refs, (w_ref, w_sem) = rest[:3 * N_GROUPS], rest[3 * N_GROUPS:]
    tm, d = x_ref.shape

    @pl.when(pl.program_id(0) == 0)
    def _():
        copy = pltpu.make_async_copy(w_hbm, w_ref, w_sem)
        copy.start()
        copy.wait()

    h = _norm_mod(x_ref[...], g_ref[...], mod_ref[0, 0:1, :], mod_ref[0, 1:2, :]).astype(BF16)
    hs = [h] + [jnp.dot(p[...], h, preferred_element_type=F32).astype(BF16) for p in (p1_ref, p2_ref)]
    for jj in range(3 * N_GROUPS):
        g, comp = divmod(jj, 3)
        dil = DILATIONS[g][1]
        rows = tm // dil
        out = o_refs[jj]
        acc = jnp.dot(hs[g], w_ref[:, jj * d:(jj + 1) * d], preferred_element_type=F32)
        if comp == 2:
            val = acc.astype(BF16)
            for r in range(dil):
                out[0, r] = val[r * rows:(r + 1) * rows]
            continue
        gain = gain_ref[jj]
        cos = cos_ref[g]
        sin = sin_ref[g]
        for cc in range(d // MXU_WIDTH):
            cols = slice(cc * MXU_WIDTH, (cc + 1) * MXU_WIDTH)
            a = acc[:, cols]
            ss = jnp.dot((a * a).astype(BF16), bd_ref[...], preferred_element_type=F32)
            qn = a * lax.rsqrt(ss * (1.0 / HEAD_DIM_A) + RMS_EPS) * gain[:, cols]
            for hh in range(MXU_WIDTH // LANES):
                xx = qn[:, hh * LANES:(hh + 1) * LANES]
                val = (xx * cos + pltpu.roll(xx, LANES // 2, 1) * sin).astype(BF16)
                c1 = cc * MXU_WIDTH + hh * LANES
                for r in range(dil):
                    out[0, r, :, c1:c1 + LANES] = val[r * rows:(r + 1) * rows]


def _split_half_columns(w):
    lead = w.shape[:-1]
    w = w.reshape(*lead, HEADS_A // 2, 2, 2, HEAD_DIM_A // 2)
    return jnp.swapaxes(w, -3, -2).reshape(*lead, HEADS_A * HEAD_DIM_A)


def _qkv_proj(x2, mod, g1, w_in, q_gain, k_gain, positions, batch, seq):
    n, d = x2.shape
    tm = TM_QKV
    per_b = seq // tm
    reps = d // HEAD_DIM_A
    w = w_in.astype(BF16).reshape(d, N_GROUPS, 3, d)
    w = jnp.concatenate([_split_half_columns(w[:, :, :2]), w[:, :, 2:]], axis=2).reshape(d, 3 * N_GROUPS * d)
    gains = []
    for g in range(N_GROUPS):
        gains += [_split_half_columns(jnp.tile(q_gain[g], reps)) * (HEAD_DIM_A ** -0.5),
                  _split_half_columns(jnp.tile(k_gain[g], reps)), jnp.ones((d,), F32)]
    gains = jnp.stack(gains).reshape(3 * N_GROUPS, 1, d)
    orders = [_stream_order(tm, dil) for _, dil in DILATIONS]
    cos_t, sin_t = _rope_tables(positions)
    perms = [(o[:, None] == jnp.arange(tm)[None, :]).astype(BF16) for o in orders[1:]]
    lane = jnp.arange(MXU_WIDTH)
    head_of = (lane // LANES) * 2 + (lane // (HEAD_DIM_A // 2)) % 2
    bd = (head_of[:, None] == head_of[None, :]).astype(BF16)
    dils = [DILATIONS[jj // 3][1] for jj in range(3 * N_GROUPS)]
    const = lambda shape: pl.BlockSpec(shape, lambda i: (0,) * len(shape))
    flat = pl.pallas_call(
        _qkv_kernel,
        grid=(n // tm,),
        in_specs=[
            pl.BlockSpec((tm, d), lambda i: (i, 0)),
            pl.BlockSpec((1, 6, d), lambda i: (i // per_b, 0, 0)),
            const((1, d)),
            pl.BlockSpec(memory_space=pl.ANY),
            const((3 * N_GROUPS, 1, d)),
            pl.BlockSpec((N_GROUPS, tm, LANES), lambda i: (0, i, 0)),
            pl.BlockSpec((N_GROUPS, tm, LANES), lambda i: (0, i, 0)),
            const((MXU_WIDTH, MXU_WIDTH)),
            const((tm, tm)),
            const((tm, tm)),
        ],
        out_specs=[pl.BlockSpec((1, dil, tm // dil, d), lambda i: (i // per_b, 0, i % per_b, 0)) for dil in dils],
        out_shape=[jax.ShapeDtypeStruct((batch, dil, seq // dil, d), BF16) for dil in dils],
        scratch_shapes=[pltpu.VMEM((d, 3 * N_GROUPS * d), BF16), pltpu.SemaphoreType.DMA(())],
        compiler_params=_params("arbitrary"),
        name="qkv_proj",
    )(x2, mod, g1, w, gains, cos_t, sin_t, bd, *perms)
    return [flat[3 * g:3 * g + 3] for g in range(N_GROUPS)]


def _attn_kernel(q_ref, kp_ref, kc_ref, vp_ref, vc_ref, o_ref, lse_ref, *, steps):
    q_ref, kp_ref, kc_ref, vp_ref, vc_ref, o_ref, lse_ref = (
        r.at[0, 0] for r in (q_ref, kp_ref, kc_ref, vp_ref, vc_ref, o_ref, lse_ref))
    n = pl.program_id(2)
    row = lax.broadcasted_iota(jnp.int32, (BAND, 2 * BAND), 0)
    col = lax.broadcasted_iota(jnp.int32, (BAND, 2 * BAND), 1)
    delta = row + BAND - col
    in_band = (delta >= 0) & (delta <= steps)
    lane = lax.broadcasted_iota(jnp.int32, (BAND, LANES), 1)
    first_head = lane < HEAD_DIM_A
    even_head = (lane & (HEAD_DIM_A // 2)) == 0
    for sub in range(q_ref.shape[0] // BAND):
        rows = slice(sub * BAND, (sub + 1) * BAND)
        if sub == 0:
            k_prev, v_prev = kp_ref, vp_ref
            valid = in_band & ((col >= BAND) | (n > 0))
        else:
            k_prev, v_prev = kc_ref.at[(sub - 1) * BAND:sub * BAND], vc_ref.at[(sub - 1) * BAND:sub * BAND]
            valid = in_band
        lse_tile = jnp.zeros((BAND, LANES), F32)
        for p in range(HEADS_A // 2):
            sl = slice(p * LANES, (p + 1) * LANES)
            qp = q_ref[rows, sl]
            kcat = jnp.concatenate([k_prev[:, sl], kc_ref[rows, sl]], axis=0)
            vcat = jnp.concatenate([v_prev[:, sl], vc_ref[rows, sl]], axis=0)
            outs = []
            for hh in range(2):
                keep = even_head if hh == 0 else jnp.logical_not(even_head)
                qm = jnp.where(keep, qp, jnp.zeros_like(qp))
                s = lax.dot_general(qm, kcat, (((1,), (1,)), ((), ())), preferred_element_type=F32)
                s = jnp.where(valid, s, NEG_INF)
                m = jnp.max(s, axis=-1, keepdims=True)
                pexp = jnp.exp(s - m)
                den = jnp.sum(pexp, axis=-1, keepdims=True)
                o = jnp.dot(pexp.astype(BF16), vcat, preferred_element_type=F32) / den
                outs.append(o)
                lse_tile = jnp.where(lane == (2 * p + hh), m + jnp.log(den), lse_tile)
            o_ref[rows, sl] = jnp.where(first_head, outs[0], outs[1]).astype(BF16)
        lse_ref[rows, :] = lse_tile


def _dilated_attention(q, k, v, g):
    window, dil = DILATIONS[g]
    batch, _, l, d = q.shape
    qb = min(ATTN_ROWS, l)
    cur = pl.BlockSpec((1, 1, qb, d), lambda b, r, i: (b, r, i, 0))
    prev = pl.BlockSpec((1, 1, BAND, d), lambda b, r, i: (b, r, jnp.maximum(i * (qb // BAND) - 1, 0), 0))
    return pl.pallas_call(
        functools.partial(_attn_kernel, steps=window // dil),
        grid=(batch, dil, l // qb),
        in_specs=[cur, prev, cur, prev, cur],
        out_specs=[cur, pl.BlockSpec((1, 1, qb, LANES), lambda b, r, i: (b, r, i, 0))],
        out_shape=[
            jax.ShapeDtypeStruct((batch, dil, l, d), BF16),
            jax.ShapeDtypeStruct((batch, dil, l, LANES), F32),
        ],
        compiler_params=_params("parallel", "parallel", "arbitrary"),
        name=f"dilated_attn_g{g}",
    )(q, k, k, v, v)


def _split3(v):
    hi = v.astype(BF16)
    rem = v - hi.astype(F32)
    mid = rem.astype(BF16)
    return hi, mid, (rem - mid.astype(F32)).astype(BF16)


def _dot3(mat_bf, v):
    return sum(jnp.dot(mat_bf, part, preferred_element_type=F32) for part in _split3(v))


def _merge_proj_kernel(o0_ref, o1_ref, o2_ref, l0_ref, l1_ref, l2_ref, p1_ref, p2_ref, x_ref, mod_ref, e_ref,
                       w_ref, out_ref):
    tm, d = x_ref.shape
    perms = (None, p1_ref, p2_ref)
    lses, outs = [], []
    for perm, l_ref, o_ref in zip(perms, (l0_ref, l1_ref, l2_ref), (o0_ref, o1_ref, o2_ref)):
        lse = l_ref[0].reshape(tm, LANES)
        o = o_ref[0].reshape(tm, d)
        if perm is None:
            lses.append(lse)
            outs.append(o.astype(F32))
        else:
            lses.append(_dot3(perm[...], lse))
            outs.append(jnp.dot(perm[...], o, preferred_element_type=F32))
    m = jnp.maximum(jnp.maximum(lses[0], lses[1]), lses[2])
    exps = [jnp.exp(l - m) for l in lses]
    inv = 1.0 / (exps[0] + exps[1] + exps[2])
    acc, rest = None, None
    for e, o in zip(exps, outs):
        if o is outs[-1]:
            wide = 1.0 - rest
        else:
            w = e * inv
            w_hi = w.astype(BF16)
            w_lo = (w - w_hi.astype(F32)).astype(BF16)
            wide = (jnp.dot(w_hi, e_ref[...], preferred_element_type=F32)
                    + jnp.dot(w_lo, e_ref[...], preferred_element_type=F32))
            rest = wide if rest is None else rest + wide
        acc = wide * o if acc is None else acc + wide * o
    y = jnp.dot(acc.astype(BF16), w_ref[...], preferred_element_type=F32)
    out_ref[...] = x_ref[...] + mod_ref[0, 2:3, :] * y


def _merge_proj(outs, lses, x2, mod, w_out_bf, seq):
    n, d = x2.shape
    tm = TM_MERGE
    per_b = seq // tm
    expand = (jnp.arange(LANES)[:, None] == jnp.arange(d)[None, :] // HEAD_DIM_A).astype(BF16)
    tok = jnp.arange(tm)
    perms = [(((tok % dil) * (tm // dil) + tok // dil)[:, None] == tok[None, :]).astype(BF16)
             for _, dil in DILATIONS[1:]]
    slab = lambda dil, width: pl.BlockSpec((1, dil, tm // dil, width), lambda i: (i // per_b, 0, i % per_b, 0))
    const = lambda shape: pl.BlockSpec(shape, lambda i: (0, 0))
    return pl.pallas_call(
        _merge_proj_kernel,
        grid=(n // tm,),
        in_specs=[slab(dil, d) for _, dil in DILATIONS] + [slab(dil, LANES) for _, dil in DILATIONS] + [
            const((tm, tm)),
            const((tm, tm)),
            pl.BlockSpec((tm, d), lambda i: (i, 0)),
            pl.BlockSpec((1, 6, d), lambda i: (i // per_b, 0, 0)),
            const((LANES, d)),
            const((d, d)),
        ],
        out_specs=pl.BlockSpec((tm, d), lambda i: (i, 0)),
        out_shape=jax.ShapeDtypeStruct((n, d), F32),
        compiler_params=_params("parallel"),
        name="merge_out_proj",
    )(*outs, *lses, *perms, x2, mod, expand, w_out_bf)


def _gla_in_kernel(x_ref, mod_ref, g_ref, w_ref, o_ref):
    h = _norm_mod(x_ref[...], g_ref[...], mod_ref[0, 0:1, :], mod_ref[0, 1:2, :])
    o_ref[...] = jnp.dot(h.astype(BF16), w_ref[...], preferred_element_type=F32).astype(BF16)


def _gla_in_proj(x2, mod, g1, w_bf, seq):
    n, d = x2.shape
    ncol = w_bf.shape[1]
    tm = TM_PROJ
    per_b = seq // tm
    return pl.pallas_call(
        _gla_in_kernel,
        grid=(n // tm,),
        in_specs=[
            pl.BlockSpec((tm, d), lambda i: (i, 0)),
            pl.BlockSpec((1, 6, d), lambda i: (i // per_b, 0, 0)),
            pl.BlockSpec((1, d), lambda i: (0, 0)),
            pl.BlockSpec((d, ncol), lambda i: (0, 0)),
        ],
        out_specs=pl.BlockSpec((tm, ncol), lambda i: (i, 0)),
        out_shape=jax.ShapeDtypeStruct((n, ncol), BF16),
        compiler_params=_params("parallel"),
        name="gla_in_proj",
    )(x2, mod, g1, w_bf)


def _gla_kernel(q_ref, k_ref, v_ref, r_ref, a_ref, wg_ref, gb_ref, og_ref, tri_ref, x_ref, mod_ref, wo_ref,
                out_ref, st_ref, o_scr, la_scr):
    t = pl.program_id(1)
    c = GLA_CHUNK
    nb = q_ref.shape[0]

    @pl.when(t == 0)
    def _():
        st_ref[...] = jnp.zeros_like(st_ref)

    for bb in range(nb):
        g = jnp.dot(a_ref[bb], wg_ref[...], preferred_element_type=F32) + gb_ref[...]
        la_scr[bb] = (jnp.minimum(g, 0.0) - jnp.log(1.0 + jnp.exp(-jnp.abs(g)))) * (1.0 / GATE_TAU)

    rr = lax.broadcasted_iota(jnp.int32, (c, c), 0)
    cc = lax.broadcasted_iota(jnp.int32, (c, c), 1)
    causal = cc <= rr

    def chunk(ci, carry):
        rows = pl.ds(pl.multiple_of(ci * c, c), c)
        for bb in range(nb):
            la = la_scr[bb, rows, :]
            b = _dot3(tri_ref[...], la)
            b_last = b[c - 1:c, :]
            q = q_ref[bb, rows, :].astype(F32) * (KEY_DIM_B ** -0.5)
            k = k_ref[bb, rows, :].astype(F32)
            q_dec = (q * jnp.exp(b)).astype(BF16)
            k_inv = (k * jnp.exp(-b)).astype(BF16)
            k_dec = (k * jnp.exp(b_last - b)).astype(BF16)
            decay = jnp.exp(b_last)
            v = v_ref[bb, rows, :]
            for h in range(HEADS_B):
                ks = slice(h * KEY_DIM_B, (h + 1) * KEY_DIM_B)
                vs = slice(h * VAL_DIM_B, (h + 1) * VAL_DIM_B)
                qd, ki, kd, vh = q_dec[:, ks], k_inv[:, ks], k_dec[:, ks], v[:, vs]
                att = lax.dot_general(qd, ki, (((1,), (1,)), ((), ())), preferred_element_type=F32)
                att = jnp.where(causal, att, 0.0)
                st = st_ref[bb, h]
                o = (jnp.dot(att.astype(BF16), vh, preferred_element_type=F32)
                     + lax.dot_general(qd, st.astype(BF16), (((1,), (1,)), ((), ())), preferred_element_type=F32))
                o_scr[bb, rows, vs] = o
                st_ref[bb, h] = st * decay[:, ks] + lax.dot_general(vh, kd, (((0,), (0,)), ((), ())),
                                                                     preferred_element_type=F32)
        return carry

    lax.fori_loop(0, q_ref.shape[1] // c, chunk, 0, unroll=2)

    for bb in range(nb):
        r = r_ref[bb].astype(F32)
        parts = []
        for h in range(HEADS_B):
            vs = slice(h * VAL_DIM_B, (h + 1) * VAL_DIM_B)
            oh = o_scr[bb, :, vs]
            ms = jnp.mean(oh * oh, axis=-1, keepdims=True)
            rh = r[:, vs]
            parts.append((oh * lax.rsqrt(ms + RMS_EPS) * og_ref[...] * (rh * jax.nn.sigmoid(rh))).astype(BF16))
        y = jnp.dot(jnp.concatenate(parts, axis=1), wo_ref[...], preferred_element_type=F32)
        out_ref[bb] = x_ref[bb] + mod_ref[bb, 2:3, :] * y


def _gla(proj, x3, mod, wg_bf, gate_bias, out_gain, w_out_bf):
    batch, seq, d = x3.shape
    ncol = proj.shape[1]
    t = T_GLA
    hk = HEADS_B * KEY_DIM_B
    hv = HEADS_B * VAL_DIM_B
    p3 = proj.reshape(batch, seq, ncol)
    tri = (jnp.arange(GLA_CHUNK)[None, :] <= jnp.arange(GLA_CHUNK)[:, None]).astype(BF16)
    a_blk = (2 * hk + 2 * hv) // LANES
    nb = 1
    const = lambda shape: pl.BlockSpec(shape, lambda b, i: (0,) * len(shape))
    return pl.pallas_call(
        _gla_kernel,
        grid=(batch // nb, seq // t),
        in_specs=[
            pl.BlockSpec((nb, t, hk), lambda b, i: (b, i, 0)),
            pl.BlockSpec((nb, t, hk), lambda b, i: (b, i, 1)),
            pl.BlockSpec((nb, t, hv), lambda b, i: (b, i, (2 * hk) // hv)),
            pl.BlockSpec((nb, t, hv), lambda b, i: (b, i, (2 * hk + hv) // hv)),
            pl.BlockSpec((nb, t, LANES), lambda b, i: (b, i, a_blk)),
            const((LANES, hk)),
            const((1, hk)),
            const((1, VAL_DIM_B)),
            const((GLA_CHUNK, GLA_CHUNK)),
            pl.BlockSpec((nb, t, d), lambda b, i: (b, i, 0)),
            pl.BlockSpec((nb, 6, d), lambda b, i: (b, 0, 0)),
            const((hv, d)),
        ],
        out_specs=pl.BlockSpec((nb, t, d), lambda b, i: (b, i, 0)),
        out_shape=jax.ShapeDtypeStruct((batch, seq, d), F32),
        scratch_shapes=[
            pltpu.VMEM((nb, HEADS_B, VAL_DIM_B, KEY_DIM_B), F32),
            pltpu.VMEM((nb, t, hv), F32),
            pltpu.VMEM((nb, t, hk), F32),
        ],
        compiler_params=_params("parallel", "arbitrary"),
        name="gla",
    )(p3, p3, p3, p3, p3, wg_bf, gate_bias, out_gain, tri, x3, mod, w_out_bf)


def _router_kernel(x_ref, mod_ref, g_ref, rw_ref, rb_ref, tri_ref, upper_ref, h_ref, info_ref, cnt_ref, tcnt_ref,
                   tcar_ref, carry_ref):
    i = pl.program_id(0)

    @pl.when(i == 0)
    def _():
        carry_ref[...] = jnp.zeros_like(carry_ref)

    h = _norm_mod(x_ref[...], g_ref[...], mod_ref[0, 3:4, :], mod_ref[0, 4:5, :])
    h_ref[...] = h.astype(BF16)
    tm = h.shape[0]
    lane = lax.broadcasted_iota(jnp.int32, (tm, LANES), 1)
    lane_f = lane.astype(F32)
    h_hi = h.astype(BF16)
    h_lo = (h - h_hi.astype(F32)).astype(BF16)
    logits = (jnp.dot(h_hi, rw_ref[0], preferred_element_type=F32) + jnp.dot(h_lo, rw_ref[0], preferred_element_type=F32)
              + jnp.dot(h_hi, rw_ref[1], preferred_element_type=F32) + rb_ref[...])
    work = jnp.where(lane < N_EXPERTS, logits, -jnp.inf)
    picks, vals, idxs = [], [], []
    for _ in range(TOP_K):
        m = jnp.max(work, axis=-1, keepdims=True)
        idx = jnp.min(jnp.where(work == m, lane_f, float(LANES)), axis=-1, keepdims=True)
        pick = lane_f == idx
        work = jnp.where(pick, -jnp.inf, work)
        picks.append(pick)
        vals.append(m)
        idxs.append(idx)
    exps = [jnp.exp(v - vals[0]) for v in vals]
    inv = 1.0 / (exps[0] + exps[1] + exps[2] + exps[3])
    chosen = jnp.zeros((tm, LANES), F32)
    for pick in picks:
        chosen = jnp.where(pick, 1.0, chosen)
    here = jnp.floor((jnp.sum(chosen, axis=0, keepdims=True) + (RUN_ALIGN - 1)) * (1.0 / RUN_ALIGN)) * RUN_ALIGN
    run_start = jnp.dot(jnp.broadcast_to(here, (8, LANES)).astype(BF16), upper_ref[...],
                        preferred_element_type=F32)[0:1, :]
    sorted_row = jnp.dot(tri_ref[...], chosen.astype(BF16), preferred_element_type=F32) + run_start
    info = jnp.zeros((tm, LANES), F32)
    for kk in range(TOP_K):
        row = jnp.sum(jnp.where(picks[kk], sorted_row, 0.0), axis=-1, keepdims=True)
        info = jnp.where(lane == kk, idxs[kk], info)
        info = jnp.where(lane == TOP_K + kk, row, info)
        info = jnp.where(lane == 2 * TOP_K + kk, exps[kk] * inv, info)
    info_ref[...] = info
    tcnt_ref[0] = jnp.broadcast_to(here, tcnt_ref.shape[1:])
    tcar_ref[0] = carry_ref[...]
    total = carry_ref[0:1, :] + here
    carry_ref[...] = jnp.broadcast_to(total, carry_ref.shape)
    cnt_ref[...] = jnp.broadcast_to(total, cnt_ref.shape)


def _router(x2, mod, g2, router_w, router_b, seq):
    n, d = x2.shape
    tm = TM_ROUTE
    per_b = seq // tm
    rw = jnp.zeros((d, LANES), F32).at[:, :N_EXPERTS].set(router_w)
    rw_hi = rw.astype(BF16)
    rw = jnp.stack([rw_hi, (rw - rw_hi.astype(F32)).astype(BF16)])
    rb = jnp.zeros((1, LANES), F32).at[0, :N_EXPERTS].set(router_b)
    tri = (jnp.arange(tm)[None, :] < jnp.arange(tm)[:, None]).astype(BF16)
    upper = (jnp.arange(LANES)[:, None] < jnp.arange(LANES)[None, :]).astype(BF16)
    return pl.pallas_call(
        _router_kernel,
        grid=(n // tm,),
        in_specs=[
            pl.BlockSpec((tm, d), lambda i: (i, 0)),
            pl.BlockSpec((1, 6, d), lambda i: (i // per_b, 0, 0)),
            pl.BlockSpec((1, d), lambda i: (0, 0)),
            pl.BlockSpec((2, d, LANES), lambda i: (0, 0, 0)),
            pl.BlockSpec((1, LANES), lambda i: (0, 0)),
            pl.BlockSpec((tm, tm), lambda i: (0, 0)),
            pl.BlockSpec((LANES, LANES), lambda i: (0, 0)),
        ],
        out_specs=[
            pl.BlockSpec((tm, d), lambda i: (i, 0)),
            pl.BlockSpec((tm, LANES), lambda i: (i, 0)),
            pl.BlockSpec((8, LANES), lambda i: (0, 0)),
            pl.BlockSpec((1, 8, LANES), lambda i: (i, 0, 0)),
            pl.BlockSpec((1, 8, LANES), lambda i: (i, 0, 0)),
        ],
        out_shape=[
            jax.ShapeDtypeStruct((n, d), BF16),
            jax.ShapeDtypeStruct((n, LANES), F32),
            jax.ShapeDtypeStruct((8, LANES), F32),
            jax.ShapeDtypeStruct((n // tm, 8, LANES), F32),
            jax.ShapeDtypeStruct((n // tm, 8, LANES), F32),
        ],
        scratch_shapes=[pltpu.VMEM((8, LANES), F32)],
        compiler_params=_params("arbitrary"),
        name="moe_router",
    )(x2, mod, g2, rw, rb, tri, upper)


def _for_each_run(tile, cnt_ref, lo_ref, base_ref, act):
    bits = [1 << k for k in range(TM_MOVE.bit_length() - 1, RUN_ALIGN.bit_length() - 2, -1)]

    def one(e, carry):
        j = tile * N_EXPERTS + e
        cnt, lo, base = cnt_ref[j], lo_ref[j], base_ref[j]
        off = jnp.int32(0)
        for bit in bits:
            take = (cnt & bit) != 0
            pl.when(take)(functools.partial(act, pl.multiple_of(lo + off, RUN_ALIGN),
                                            pl.multiple_of(base + off, RUN_ALIGN), bit))
            off = off + jnp.where(take, bit, 0)
        return carry

    lax.fori_loop(0, N_EXPERTS, one, 0)


def _dispatch_kernel(seg_end_ref, padded_ref, cnt_ref, lo_ref, base_ref, pos_ref, h_ref, xs_hbm,
                     loc_ref, zero_ref, sems, zero_sem):
    i = pl.program_id(0)
    tm, d = h_ref.shape

    @pl.when(pl.program_id(0) == 0)
    def _():
        zero_ref[...] = jnp.zeros_like(zero_ref)

        def tail(e):
            return pltpu.make_async_copy(
                zero_ref, xs_hbm.at[pl.ds(pl.multiple_of(seg_end_ref[e] - TM_EXPERT, TM_EXPERT), TM_EXPERT)], zero_sem)

        def start(e, carry):
            pl.when(padded_ref[e] > 0)(lambda: tail(e).start())
            return carry

        def wait(e, carry):
            pl.when(padded_ref[e] > 0)(lambda: tail(e).wait())
            return carry

        lax.fori_loop(0, N_EXPERTS, start, 0)
        lax.fori_loop(0, N_EXPERTS, wait, 0)

        def spare(blk):
            return pltpu.make_async_copy(
                zero_ref, xs_hbm.at[pl.ds(pl.multiple_of(blk * TM_EXPERT, TM_EXPERT), TM_EXPERT)], zero_sem)

        used = seg_end_ref[N_EXPERTS - 1] // TM_EXPERT
        total = xs_hbm.shape[0] // TM_EXPERT
        lax.fori_loop(used, total, lambda blk, carry: (spare(blk).start(), carry)[1], 0)
        lax.fori_loop(used, total, lambda blk, carry: (spare(blk).wait(), carry)[1], 0)

    slot = lax.broadcasted_iota(jnp.int32, (RUN_ROWS, tm), 0).astype(F32)
    perm = jnp.zeros((RUN_ROWS, tm), F32)
    for kk in range(TOP_K):
        perm = jnp.where(slot == pos_ref[kk:kk + 1, :], 1.0, perm)
    perm = perm.astype(BF16)
    buf = i % 2
    loc_ref[buf] = _pack_pairs(jnp.dot(perm, h_ref[...], preferred_element_type=F32))

    def copy(which, local_row, global_row, size):
        return pltpu.make_async_copy(loc_ref.at[which, pl.ds(local_row, size)],
                                     xs_hbm.at[pl.ds(global_row, size)], sems.at[which])

    _for_each_run(i, cnt_ref, lo_ref, base_ref, lambda a, b, size: copy(buf, a, b, size).start())

    @pl.when(i > 0)
    def _():
        _for_each_run(i - 1, cnt_ref, lo_ref, base_ref, lambda a, b, size: copy(1 - buf, a, b, size).wait())

    @pl.when(i == pl.num_programs(0) - 1)
    def _():
        _for_each_run(i, cnt_ref, lo_ref, base_ref, lambda a, b, size: copy(buf, a, b, size).wait())


def _dispatch(h, pos_t, seg_end, padded, tile_cnt, tile_lo, tile_base, cap):
    n, d = h.shape
    tm = TM_MOVE
    smem = lambda i, *_: (0, i)
    return pl.pallas_call(
        _dispatch_kernel,
        grid_spec=pltpu.PrefetchScalarGridSpec(
            num_scalar_prefetch=5,
            grid=(n // tm,),
            in_specs=[
                pl.BlockSpec((8, tm), smem),
                pl.BlockSpec((tm, d), lambda i, *_: (i, 0)),
            ],
            out_specs=pl.BlockSpec(memory_space=pl.ANY),
            scratch_shapes=[pltpu.VMEM((2, RUN_ROWS, d // 2), jnp.uint32), pltpu.VMEM((TM_EXPERT, d // 2), jnp.uint32),
                            pltpu.SemaphoreType.DMA((2,)), pltpu.SemaphoreType.DMA(())],
        ),
        out_shape=jax.ShapeDtypeStruct((cap, d // 2), jnp.uint32),
        compiler_params=_params("arbitrary"),
        name="moe_dispatch",
    )(seg_end, padded, tile_cnt, tile_lo, tile_base, pos_t, h)


def _expert_kernel(be_ref, nxt_ref, na_ref, x_ref, wgu_hbm, bgu_ref, wd_hbm, bd_ref, o_ref,
                   gu_stage, d_stage, gu_bf, d_bf, sems, *, layer):
    i = pl.program_id(0)
    live = i < na_ref[0]
    e = be_ref[i]

    def fetch(expert):
        return (pltpu.make_async_copy(wgu_hbm.at[layer, expert], gu_stage, sems.at[0]),
                pltpu.make_async_copy(wd_hbm.at[layer, expert], d_stage, sems.at[1]))

    @pl.when(i == 0)
    def _():
        for c in fetch(e):
            c.start()

    @pl.when(live & ((i == 0) | (e != be_ref[jnp.maximum(i - 1, 0)])))
    def _():
        for c in fetch(e):
            c.wait()
        gu_bf[...] = gu_stage[...].astype(BF16)
        d_bf[...] = d_stage[...].astype(BF16)

        @pl.when(nxt_ref[i] >= 0)
        def _():
            for c in fetch(nxt_ref[i]):
                c.start()

    @pl.when(jnp.logical_not(live))
    def _():
        o_ref[...] = jnp.zeros_like(o_ref)

    @pl.when(live)
    def _():
        f = d_bf.shape[0]
        xb = _unpack_pairs(x_ref[...]).astype(BF16)
        gu = jnp.dot(xb, gu_bf[...], preferred_element_type=F32) + bgu_ref[0]
        gate = jnp.minimum(gu[:, :f], SWIGLU_LIMIT)
        up = jnp.clip(gu[:, f:], -SWIGLU_LIMIT, SWIGLU_LIMIT)
        act = (up + 1.0) * (gate * jax.nn.sigmoid(SWIGLU_ALPHA * gate))
        o_ref[...] = _pack_pairs(jnp.dot(act.astype(BF16), d_bf[...], preferred_element_type=F32) + bd_ref[0])


def _experts(xs, block_e, next_e, n_active, w_gu, b_gu, w_down, b_down, layer):
    cap, w = xs.shape
    _, e, d, f2 = w_gu.shape
    f = f2 // 2
    tm = TM_EXPERT
    nb = cap // tm
    return pl.pallas_call(
        functools.partial(_expert_kernel, layer=layer),
        grid_spec=pltpu.PrefetchScalarGridSpec(
            num_scalar_prefetch=3,
            grid=(nb,),
            in_specs=[
                pl.BlockSpec((tm, w), lambda i, be, nx, na: (jnp.minimum(i, na[0] - 1), 0)),
                pl.BlockSpec(memory_space=pl.ANY),
                pl.BlockSpec((1, 1, f2), lambda i, be, nx, na: (be[i], 0, 0)),
                pl.BlockSpec(memory_space=pl.ANY),
                pl.BlockSpec((1, 1, d), lambda i, be, nx, na: (be[i], 0, 0)),
            ],
            out_specs=pl.BlockSpec((tm, w), lambda i, be, nx, na: (i, 0)),
            scratch_shapes=[pltpu.VMEM((d, f2), F32), pltpu.VMEM((f, d), F32), pltpu.VMEM((d, f2), BF16),
                            pltpu.VMEM((f, d), BF16), pltpu.SemaphoreType.DMA((2,))],
        ),
        out_shape=jax.ShapeDtypeStruct((cap, w), xs.dtype),
        compiler_params=_params("arbitrary"),
        name="moe_experts",
    )(block_e, next_e, n_active, xs, w_gu, b_gu.reshape(e, 1, f2), w_down, b_down.reshape(e, 1, d))


def _combine_kernel(cnt_ref, lo_ref, base_ref, info_ref, x_ref, mod_ref, ys_hbm, out_ref, loc_ref, sems):
    i = pl.program_id(0)
    tm = x_ref.shape[0]
    buf = i % 2

    def copy(which, local_row, global_row, size):
        return pltpu.make_async_copy(ys_hbm.at[pl.ds(global_row, size)],
                                     loc_ref.at[which, pl.ds(local_row, size)], sems.at[which])

    def fetch(tile, which):
        _for_each_run(tile, cnt_ref, lo_ref, base_ref, lambda a, b, size: copy(which, a, b, size).start())

    @pl.when(i == 0)
    def _():
        loc_ref[...] = jnp.zeros_like(loc_ref)
        fetch(i, buf)

    @pl.when(i + 1 < pl.num_programs(0))
    def _():
        fetch(i + 1, 1 - buf)

    _for_each_run(i, cnt_ref, lo_ref, base_ref, lambda a, b, size: copy(buf, a, b, size).wait())

    info = info_ref[...]
    slot = lax.broadcasted_iota(jnp.int32, (tm, RUN_ROWS), 1).astype(F32)
    mix = jnp.zeros((tm, RUN_ROWS), F32)
    for kk in range(TOP_K):
        mix = jnp.where(slot == info[:, TOP_K + kk:TOP_K + kk + 1], info[:, 2 * TOP_K + kk:2 * TOP_K + kk + 1], mix)
    rows = _unpack_pairs(loc_ref[buf]).astype(BF16)
    y = jnp.dot(mix.astype(BF16), rows, preferred_element_type=F32)
    out_ref[...] = x_ref[...] + mod_ref[0, 5:6, :] * y


def _combine(ys, info, tile_cnt, tile_lo, tile_base, x2, mod, seq):
    n, d = x2.shape
    tm = TM_MOVE
    per_b = seq // tm
    return pl.pallas_call(
        _combine_kernel,
        grid_spec=pltpu.PrefetchScalarGridSpec(
            num_scalar_prefetch=3,
            grid=(n // tm,),
            in_specs=[
                pl.BlockSpec((tm, LANES), lambda i, *_: (i, 0)),
                pl.BlockSpec((tm, d), lambda i, *_: (i, 0)),
                pl.BlockSpec((1, 6, d), lambda i, *_: (i // per_b, 0, 0)),
                pl.BlockSpec(memory_space=pl.ANY),
            ],
            out_specs=pl.BlockSpec((tm, d), lambda i, *_: (i, 0)),
            scratch_shapes=[pltpu.VMEM((2, RUN_ROWS, ys.shape[1]), ys.dtype), pltpu.SemaphoreType.DMA((2,))],
        ),
        out_shape=jax.ShapeDtypeStruct((n, d), F32),
        compiler_params=_params("arbitrary"),
        name="moe_combine",
    )(tile_cnt, tile_lo, tile_base, info, x2, mod, ys)


def _moe(x2, mod, g2, router_w, router_b, w_gu, b_gu, w_down, b_down, layer, seq):
    n, d = x2.shape
    assert TM_MOVE == TM_ROUTE, "dispatch / combine tiles reuse the router's per-tile expert counts"
    h, info, cnt, tcnt, tcar = _router(x2, mod, g2, router_w, router_b, seq)
    counts = cnt[0, :N_EXPERTS].astype(jnp.int32)
    padded = (counts + TM_EXPERT - 1) // TM_EXPERT * TM_EXPERT
    seg_end = jnp.cumsum(padded)
    seg_start = seg_end - padded
    worst = n * TOP_K + N_EXPERTS * (n // TM_MOVE) * (RUN_ALIGN - 1)
    nb = -(-worst // TM_EXPERT) + N_EXPERTS
    cap = nb * TM_EXPERT
    block_start = jnp.arange(nb, dtype=jnp.int32) * TM_EXPERT
    block_e = jnp.minimum(jnp.sum(seg_end[None, :] <= block_start[:, None], axis=1), N_EXPERTS - 1).astype(jnp.int32)
    n_active = (seg_end[-1:] // TM_EXPERT).astype(jnp.int32)
    tile_cnt = tcnt[:, 0, :N_EXPERTS].astype(jnp.int32)
    tile_before = tcar[:, 0, :N_EXPERTS].astype(jnp.int32)
    tile_lo = jnp.cumsum(tile_cnt, axis=1) - tile_cnt
    tile_base = seg_start[None, :] + tile_before
    pos_t = info[:, TOP_K:TOP_K + 8].T
    tile_cnt, tile_lo, tile_base = (t.reshape(-1).astype(jnp.int32) for t in (tile_cnt, tile_lo, tile_base))
    ids = jnp.arange(N_EXPERTS, dtype=jnp.int32)
    later = (ids[None, :] > ids[:, None]) & (padded[None, :] > 0)
    next_of = jnp.min(jnp.where(later, ids[None, :], N_EXPERTS), axis=1)
    next_e = jnp.where(next_of < N_EXPERTS, next_of, -1).astype(jnp.int32)[block_e]
    xs = _dispatch(h, pos_t, seg_end.astype(jnp.int32), padded.astype(jnp.int32), tile_cnt, tile_lo, tile_base, cap)
    ys = _experts(xs, block_e, next_e, n_active, w_gu, b_gu, w_down, b_down, layer)
    return _combine(ys, info, tile_cnt, tile_lo, tile_base, x2, mod, seq)


def kernel(x, c, positions, ada_w, ada_b, norm1_g, norm2_g, a_w_in, a_q_gain, a_k_gain, a_w_out, b_w_in,
           b_w_gate_up, b_gate_bias, b_out_gain, b_w_out, router_w, router_b, moe_w_gu, moe_b_gu, moe_w_down,
           moe_b_down):
    batch, seq, d = x.shape
    depth = ada_w.shape[0]
    n = batch * seq
    mods = _adaln(c, ada_w, ada_b).reshape(depth, batch, 6, d)
    x2 = x.reshape(n, d)
    for layer in range(depth):
        mod = mods[layer]
        j = layer // 2
        g1 = norm1_g[layer].reshape(1, d)
        g2 = norm2_g[layer].reshape(1, d)
        if layer % 2 == 0:
            qkv = _qkv_proj(x2, mod, g1, a_w_in[j], a_q_gain[j], a_k_gain[j], positions, batch, seq)
            outs, lses = zip(*[_dilated_attention(*qkv[g], g) for g in range(N_GROUPS)])
            x2 = _merge_proj(outs, lses, x2, mod, a_w_out[j].astype(BF16), seq)
        else:
            hk = HEADS_B * KEY_DIM_B
            hv = HEADS_B * VAL_DIM_B
            width = 2 * hk + 2 * hv
            w_in = jnp.zeros((d, width + LANES), BF16).at[:, :width + GATE_RANK].set(b_w_in[j].astype(BF16))
            wg = jnp.zeros((LANES, hk), BF16).at[:GATE_RANK].set(b_w_gate_up[j].astype(BF16))
            proj = _gla_in_proj(x2, mod, g1, w_in, seq)
            x3 = _gla(proj, x2.reshape(batch, seq, d), mod, wg, b_gate_bias[j].reshape(1, hk),
                      b_out_gain[j].reshape(1, VAL_DIM_B), b_w_out[j].astype(BF16))
            x2 = x3.reshape(n, d)
        x2 = _moe(x2, mod, g2, router_w[layer], router_b[layer], moe_w_gu, moe_b_gu[layer],
                  moe_w_down, moe_b_down[layer], layer, seq)
    return x2.reshape(batch, seq, d)
```
